```python
import math
import jax, jax.numpy as jnp
from jax import lax
import numpy as np

D_MODEL = 1024
BATCH = 8
SEQ = 4096
DEPTH = 1
DEC_BATCH = 32
DEC_SEQ = 8
PAST_LEN = 16384
PAGE_SIZE = 128

D_MIX = D_MODEL
D_A = D_MIX // 2
H_A = 4
DK_A = D_A // H_A
DV_A = D_A // H_A
D_B = D_MIX - D_A
H_B = 4
DH_B = D_B // (2 * H_B)
DV_B = 2 * DH_B
N_BUCKETS = 32
MAX_EXACT = N_BUCKETS // 2
MAX_DIST = 1024
CHUNK = 64
Q_BLOCK = 128
EPS = 1e-6

kernel_name = 'hymba_hgrn2_diffattn_adaln_step'


def rms_norm(x, w):
    xf = x.astype(jnp.float32)
    return xf * lax.rsqrt(jnp.mean(xf * xf, axis=-1, keepdims=True) + EPS) * w.astype(jnp.float32)


def t5_bucket(dist):
    d = jnp.maximum(dist, 0)
    large = MAX_EXACT + (jnp.log(jnp.maximum(d, 1).astype(jnp.float32) / MAX_EXACT)
                         / math.log(MAX_DIST / MAX_EXACT) * (N_BUCKETS - MAX_EXACT)).astype(jnp.int32)
    large = jnp.minimum(large, N_BUCKETS - 1)
    return jnp.where(d < MAX_EXACT, d, large)


def hgrn2_chunked(q, k, v, logf, S0):
    B, L, H, K = q.shape
    V = v.shape[-1]
    C = CHUNK if L % CHUNK == 0 else L
    n = L // C

    def blocks(a):
        return jnp.moveaxis(a.reshape(B, n, C, H, a.shape[-1]), 1, 0)

    tril = jnp.tril(jnp.ones((C, C), dtype=bool))[None, :, :, None, None]

    def step(S, inp):
        qc, kc, vc, lfc = inp
        b = jnp.cumsum(lfc, axis=1)
        o_inter = jnp.einsum('bthk,bhkv->bthv', qc * jnp.exp(b), S)
        diff = b[:, :, None] - b[:, None]
        decay = jnp.where(tril, jnp.exp(jnp.where(tril, diff, 0.0)), 0.0)
        att = jnp.einsum('btjhk,bjhk->bhtj', qc[:, :, None] * decay, kc)
        o_intra = jnp.einsum('bhtj,bjhv->bthv', att, vc)
        b_last = b[:, -1]
        S_new = jnp.exp(b_last)[..., None] * S + jnp.einsum(
            'bjhk,bjhv->bhkv', kc * jnp.exp(b_last[:, None] - b), vc)
        return S_new, o_inter + o_intra

    S_fin, o = lax.scan(step, S0, (blocks(q), blocks(k), blocks(v), blocks(logf)))
    o = jnp.moveaxis(o, 0, 1).reshape(B, L, H, V)
    return o, S_fin


def diff_attention(q, k, v, q_pos, k_pos, rel_table, lam):
    B, Lq = q.shape[:2]
    scale = DH_B ** -0.5

    def block(args):
        qb, qp = args
        s = jnp.einsum('bqhcd,bkhcd->bchqk', qb, k) * scale
        bias = jnp.moveaxis(rel_table.astype(jnp.float32)[t5_bucket(qp[:, None] - k_pos[None, :])], -1, 0)
        causal = qp[:, None] >= k_pos[None, :]
        s = jnp.where(causal, s.astype(jnp.float32) + bias, -jnp.inf)
        p = jax.nn.softmax(s, axis=-1)
        a = p[:, 0] - lam * p[:, 1]
        return jnp.einsum('bhqk,bkhe->bqhe', a, v)

    if Lq > Q_BLOCK and Lq % Q_BLOCK == 0:
        nb = Lq // Q_BLOCK
        qs = jnp.moveaxis(q.reshape(B, nb, Q_BLOCK, *q.shape[2:]), 1, 0)
        ps = q_pos.reshape(nb, Q_BLOCK)
        o = lax.map(block, (qs, ps))
        return jnp.moveaxis(o, 0, 1).reshape(B, Lq, *o.shape[3:])
    return block((q, q_pos))


def mixer_layer(x, c, S0, past_k, past_v, norm_w, w_ada, b_ada, w_in, w_out, lb,
                hg_norm_w, q_norm_w, k_norm_w, lam_q1, lam_k1, lam_q2, lam_k2,
                sub_norm_w, rel_table, lambda_init):
    f32 = jnp.float32
    B, L, _ = x.shape
    mod = jax.nn.silu(c.astype(f32)) @ w_ada.astype(f32) + b_ada.astype(f32)
    shift, scale, gate = jnp.split(mod, 3, axis=-1)
    h = rms_norm(x, norm_w) * (1.0 + scale[:, None]) + shift[:, None]
    proj = (h.astype(x.dtype) @ w_in).astype(f32)
    splits = [int(s) for s in np.cumsum([D_A] * 5 + [D_B] * 4)[:-1]]
    qa, fa, ia, ga, za, qb, kb, vb, zb = jnp.split(proj, splits, axis=-1)

    lb = lb.astype(f32).reshape(H_A, DK_A)
    fa = fa.reshape(B, L, H_A, DK_A)
    logf = jnp.log(lb + (1.0 - lb) * jax.nn.sigmoid(fa))
    k_in = (1.0 - lb) * jax.nn.sigmoid(-fa)
    o_a, S_fin = hgrn2_chunked(qa.reshape(B, L, H_A, DK_A) * DK_A ** -0.5, k_in,
                               ia.reshape(B, L, H_A, DV_A), logf, S0.astype(f32))
    o_a = rms_norm(o_a * jax.nn.sigmoid(ga.reshape(B, L, H_A, DV_A)), hg_norm_w)
    y_a = o_a.reshape(B, L, D_A) * jax.nn.silu(za)

    q = rms_norm(qb.reshape(B, L, H_B, 2, DH_B), q_norm_w)
    k = rms_norm(kb.reshape(B, L, H_B, 2, DH_B), k_norm_w)
    v = vb.reshape(B, L, H_B, DV_B)
    if past_k is None:
        k_all, v_all, pos0 = k, v, 0
    else:
        pos0 = past_k.shape[1]
        k_all = jnp.concatenate([past_k.astype(f32).reshape(B, pos0, H_B, 2, DH_B), k], axis=1)
        v_all = jnp.concatenate([past_v.astype(f32), v], axis=1)
    q_pos = pos0 + jnp.arange(L, dtype=jnp.int32)
    k_pos = jnp.arange(k_all.shape[1], dtype=jnp.int32)
    lam = (jnp.exp(jnp.sum(lam_q1.astype(f32) * lam_k1.astype(f32)))
           - jnp.exp(jnp.sum(lam_q2.astype(f32) * lam_k2.astype(f32))) + lambda_init)
    o_b = diff_attention(q, k_all, v_all, q_pos, k_pos, rel_table, lam)
    o_b = rms_norm(o_b, sub_norm_w) * (1.0 - lambda_init)
    y_b = o_b.reshape(B, L, D_B) * jax.nn.silu(zb)

    out = (jnp.concatenate([y_a, y_b], axis=-1).astype(x.dtype) @ w_out).astype(f32)
    x_new = (x.astype(f32) + gate[:, None] * out).astype(x.dtype)
    return (x_new, k.reshape(B, L, H_B, DV_B).astype(x.dtype), v.astype(x.dtype),
            S_fin.astype(x.dtype))


def setup_inputs(seed: int = 0) -> dict:
    key = jax.random.key(seed)
    ks = jax.random.split(key, 24)
    n_pages = PAST_LEN // PAGE_SIZE
    n_used = DEC_BATCH * n_pages
    n_pool = -(-5 * n_used // 4)
    d_in = 5 * D_A + 4 * D_B
    nrm = jax.random.normal
    page_table = jax.random.permutation(ks[0], n_pool)[:n_used].reshape(DEC_BATCH, n_pages).astype(jnp.int32)
    return {
        'x_prompt': nrm(ks[1], (BATCH, SEQ, D_MODEL), jnp.float32),
        'x_sample': nrm(ks[2], (DEC_BATCH, DEC_SEQ, D_MODEL), jnp.float32),
        'c_prompt': nrm(ks[3], (BATCH, D_MODEL), jnp.float32),
        'c_sample': nrm(ks[4], (DEC_BATCH, D_MODEL), jnp.float32),
        'cache_k': nrm(ks[5], (DEPTH, n_pool, PAGE_SIZE, H_B, DV_B), jnp.float32),
        'cache_v': nrm(ks[6], (DEPTH, n_pool, PAGE_SIZE, H_B, DV_B), jnp.float32),
        'state_hgrn': 0.3 * nrm(ks[7], (DEPTH, DEC_BATCH, H_A, DK_A, DV_A), jnp.float32),
        'page_table': page_table,
        'norm_w': 1.0 + 0.02 * nrm(ks[8], (DEPTH, D_MODEL), jnp.float32),
        'w_ada': 0.5 * D_MODEL ** -0.5 * nrm(ks[9], (DEPTH, D_MODEL, 3 * D_MODEL), jnp.float32),
        'b_ada': 0.1 * nrm(ks[10], (DEPTH, 3 * D_MODEL), jnp.float32),
        'w_in': D_MODEL ** -0.5 * nrm(ks[11], (DEPTH, D_MODEL, d_in), jnp.float32),
        'w_out': D_MIX ** -0.5 * nrm(ks[12], (DEPTH, D_MIX, D_MODEL), jnp.float32),
        'lb_param': 0.5 * nrm(ks[13], (DEPTH + 1, D_A), jnp.float32),
        'hg_norm_w': 1.0 + 0.02 * nrm(ks[14], (DEPTH, DV_A), jnp.float32),
        'q_norm_w': 1.0 + 0.02 * nrm(ks[15], (DEPTH, DH_B), jnp.float32),
        'k_norm_w': 1.0 + 0.02 * nrm(ks[16], (DEPTH, DH_B), jnp.float32),
        'lam_q1': 0.1 * nrm(ks[17], (DEPTH, DH_B), jnp.float32),
        'lam_k1': 0.1 * nrm(ks[18], (DEPTH, DH_B), jnp.float32),
        'lam_q2': 0.1 * nrm(ks[19], (DEPTH, DH_B), jnp.float32),
        'lam_k2': 0.1 * nrm(ks[20], (DEPTH, DH_B), jnp.float32),
        'sub_norm_w': 1.0 + 0.02 * nrm(ks[21], (DEPTH, DV_B), jnp.float32),
        'rel_table': 0.5 * nrm(ks[22], (N_BUCKETS, H_B), jnp.float32),
    }


def reference(x_prompt, x_sample, c_prompt, c_sample, cache_k, cache_v, state_hgrn, page_table,
              norm_w, w_ada, b_ada, w_in, w_out, lb_param, hg_norm_w, q_norm_w, k_norm_w,
              lam_q1, lam_k1, lam_q2, lam_k2, sub_norm_w, rel_table):
    lb_all = jnp.cumsum(jax.nn.softmax(lb_param.astype(jnp.float32), axis=0), axis=0)
    dec_b = x_sample.shape[0]
    yp, ys = x_prompt, x_sample
    kp_l, vp_l, sp_l, ksm_l, vsm_l, ssm_l = [], [], [], [], [], []
    for l in range(DEPTH):
        lambda_init = 0.8 - 0.6 * math.exp(-0.3 * l)
        wl = (norm_w[l], w_ada[l], b_ada[l], w_in[l], w_out[l], lb_all[l], hg_norm_w[l],
              q_norm_w[l], k_norm_w[l], lam_q1[l], lam_k1[l], lam_q2[l], lam_k2[l],
              sub_norm_w[l], rel_table, lambda_init)
        S0p = jnp.zeros((yp.shape[0], H_A, DK_A, DV_A), jnp.float32)
        yp, kp, vp, sp = mixer_layer(yp, c_prompt, S0p, None, None, *wl)
        past_k = cache_k[l][page_table].reshape(dec_b, -1, H_B, DV_B)
        past_v = cache_v[l][page_table].reshape(dec_b, -1, H_B, DV_B)
        ys, ksm, vsm, ssm = mixer_layer(ys, c_sample, state_hgrn[l], past_k, past_v, *wl)
        kp_l.append(kp); vp_l.append(vp); sp_l.append(sp)
        ksm_l.append(ksm); vsm_l.append(vsm); ssm_l.append(ssm)
    k_prompt = jnp.stack(kp_l)
    v_prompt = jnp.stack(vp_l)
    state_prompt = jnp.stack(sp_l)
    k_sample = jnp.stack(ksm_l)
    v_sample = jnp.stack(vsm_l)
    state_sample = jnp.stack(ssm_l)
    return (yp, ys, k_prompt, v_prompt, state_prompt, k_sample, v_sample, state_sample)
```

```python
import functools
import math

import jax
import jax.numpy as jnp
from jax import lax
from jax.experimental import pallas as pl
from jax.experimental.pallas import tpu as pltpu

F32 = jnp.float32
BF16 = jnp.bfloat16

N_HEADS = 4
HEAD_W = 128
HALF = 64
GROUP_W = N_HEADS * HEAD_W
N_BUCKETS = 32
MAX_EXACT = N_BUCKETS // 2
MAX_DIST = 1024
CHUNK = 64
SUB = 16
EPS = 1e-6
LAMBDA_INIT = 0.8 - 0.6 * math.exp(-0.3 * 0)
NEG = -1e30

LANES = 128
SUBLANES = 8
VMEM_LIMIT_BYTES = 56 * 1024 * 1024


def _bucket_lower_bounds():
    los = list(range(MAX_EXACT))
    span = N_BUCKETS - MAX_EXACT
    ratio = MAX_DIST // MAX_EXACT
    for m in range(MAX_EXACT, N_BUCKETS):
        d = MAX_EXACT
        while d ** span < (MAX_EXACT ** span) * (ratio ** (m - MAX_EXACT)):
            d += 1
        los.append(d)
    return los


BUCKET_LO = _bucket_lower_bounds()
SATURATED_DIST = BUCKET_LO[-1]


def _cparams(sem):
    return pltpu.CompilerParams(dimension_semantics=sem, vmem_limit_bytes=VMEM_LIMIT_BYTES)


def _split2(x):
    hi = x.astype(BF16)
    lo = (x - hi.astype(F32)).astype(BF16)
    return hi, lo


def _split3(x):
    hi = x.astype(BF16)
    r = x - hi.astype(F32)
    mid = r.astype(BF16)
    lo = (r - mid.astype(F32)).astype(BF16)
    return hi, mid, lo


def _dot(a, b):
    return jnp.dot(a, b, preferred_element_type=F32)


def _dot_nt(a, b):
    return lax.dot_general(a, b, (((1,), (1,)), ((), ())), preferred_element_type=F32)


def _dot_tn(a, b):
    return lax.dot_general(a, b, (((0,), (0,)), ((), ())), preferred_element_type=F32)


def _div_pow2(x, n):
    assert n > 0 and n & (n - 1) == 0
    return lax.shift_right_logical(x, n.bit_length() - 1)


def _sigmoid_pair(x):
    t = jnp.exp(-jnp.abs(x))
    r = 1.0 / (1.0 + t)
    tr = t * r
    pos = x >= 0
    return jnp.where(pos, r, tr), jnp.where(pos, tr, r)


def _mod_kernel(c_ref, w_ref, b_ref, o_ref):
    c = c_ref[...]
    sig, _ = _sigmoid_pair(c)
    s = c * sig
    w = w_ref[...]
    s1, s2 = _split2(s)
    w1, w2 = _split2(w)
    o_ref[...] = _dot(s1, w1) + (_dot(s1, w2) + _dot(s2, w1)) + b_ref[...]


def _adaln_mod(c_all, w_ada, b_ada):
    n, d = c_all.shape
    d3 = w_ada.shape[1]
    bn = d
    return pl.pallas_call(
        _mod_kernel,
        out_shape=jax.ShapeDtypeStruct((n, d3), F32),
        grid=(d3 // bn,),
        in_specs=[pl.BlockSpec((n, d), lambda j: (0, 0)),
                  pl.BlockSpec((d, bn), lambda j: (0, j)),
                  pl.BlockSpec((1, bn), lambda j: (0, j))],
        out_specs=pl.BlockSpec((n, bn), lambda j: (0, j)),
        compiler_params=_cparams(("arbitrary",)),
        name="adaln_mod",
    )(c_all, w_ada, b_ada.reshape(1, d3))


def _small_kernel(lbp_ref, q1_ref, k1_ref, q2_ref, k2_ref, lb_ref, lam_ref):
    p = lbp_ref[...]
    m = jnp.max(p, axis=0, keepdims=True)
    e = jnp.exp(p - m)
    lb_ref[...] = e[0:1, :] / jnp.sum(e, axis=0, keepdims=True)
    s1 = jnp.sum(q1_ref[...] * k1_ref[...], axis=-1, keepdims=True)
    s2 = jnp.sum(q2_ref[...] * k2_ref[...], axis=-1, keepdims=True)
    lam_ref[...] = jnp.exp(s1) - jnp.exp(s2) + LAMBDA_INIT


def _small_params(lb_param, lq1, lk1, lq2, lk2):
    return pl.pallas_call(
        _small_kernel,
        out_shape=(jax.ShapeDtypeStruct((1, lb_param.shape[1]), F32),
                   jax.ShapeDtypeStruct((1, 1), F32)),
        name="small_params",
    )(lb_param, lq1, lk1, lq2, lk2)


def _bias_select(d, tab_ref, h):
    val = jnp.full(d.shape, tab_ref[N_BUCKETS - 1, h], F32)
    for m in range(N_BUCKETS - 2, -1, -1):
        val = jnp.where(d < BUCKET_LO[m + 1], tab_ref[m, h], val)
    return val


def _bias_prompt_kernel(tab_ref, o_ref, *, tile):
    h = pl.program_id(0)
    dl = pl.program_id(1)
    kk = lax.broadcasted_iota(jnp.int32, (tile, tile), 0)
    qq = lax.broadcasted_iota(jnp.int32, (tile, tile), 1)
    d = dl * tile + qq - kk
    val = _bias_select(jnp.maximum(d, 0), tab_ref, h)
    o_ref[0, 0] = jnp.where(d >= 0, val, NEG)


def _n_bias_tiles(tile):
    n = 1
    while (n - 1) * tile - (tile - 1) < SATURATED_DIST:
        n += 1
    return n


def _bias_prompt(rel_table, tile):
    nd = _n_bias_tiles(tile)
    return pl.pallas_call(
        functools.partial(_bias_prompt_kernel, tile=tile),
        out_shape=jax.ShapeDtypeStruct((N_HEADS, nd, tile, tile), F32),
        grid=(N_HEADS, nd),
        in_specs=[pl.BlockSpec(memory_space=pltpu.SMEM)],
        out_specs=pl.BlockSpec((1, 1, tile, tile), lambda h, d: (h, d, 0, 0)),
        compiler_params=_cparams(("arbitrary", "arbitrary")),
        name="bias_prompt",
    )(rel_table)


def _bias_decode_kernel(tab_ref, o_ref, on_ref, *, rows, past):
    h = pl.program_id(0)
    i = lax.broadcasted_iota(jnp.int32, (rows, past), 0)
    kpos = lax.broadcasted_iota(jnp.int32, (rows, past), 1)
    val = _bias_select(past + i - kpos, tab_ref, h)
    o_ref[0, 0] = val
    o_ref[0, 1] = val
    i2 = lax.broadcasted_iota(jnp.int32, (rows, LANES), 0)
    j2 = lax.broadcasted_iota(jnp.int32, (rows, LANES), 1)
    d2 = i2 - j2
    vn = jnp.where(d2 >= 0, _bias_select(jnp.maximum(d2, 0), tab_ref, h), NEG)
    on_ref[0, 0] = vn
    on_ref[0, 1] = vn


def _bias_decode(rel_table, rows, past):
    return pl.pallas_call(
        functools.partial(_bias_decode_kernel, rows=rows, past=past),
        out_shape=(jax.ShapeDtypeStruct((N_HEADS, 2, rows, past), F32),
                   jax.ShapeDtypeStruct((N_HEADS, 2, rows, LANES), F32)),
        grid=(N_HEADS,),
        in_specs=[pl.BlockSpec(memory_space=pltpu.SMEM)],
        out_specs=(pl.BlockSpec((1, 2, rows, past), lambda h: (h, 0, 0, 0)),
                   pl.BlockSpec((1, 2, rows, LANES), lambda h: (h, 0, 0, 0))),
        compiler_params=_cparams(("arbitrary",)),
        name="bias_decode",
    )(rel_table)


def _group_mean_sq(x, gmat):
    sq = x * x
    outs = []
    w = gmat.shape[0]
    for c0 in range(0, x.shape[1], w):
        hi, lo = _split2(sq[:, c0:c0 + w])
        outs.append(_dot(hi, gmat) + _dot(lo, gmat))
    return jnp.concatenate(outs, axis=1)


def _hgrn_chunk(bc, qc, kc, vc, st, chunk, sub):
    n_sub = chunk // sub
    o = _dot_nt((qc * jnp.exp(bc)).astype(BF16), st.astype(BF16))
    row = lax.broadcasted_iota(jnp.int32, (chunk, HEAD_W), 0)
    blocks = []
    for i in range(n_sub):
        lo, hi = i * sub, (i + 1) * sub
        mid = lo + sub // 2 - 1
        anchor = bc[mid:mid + 1, :]
        qi = qc[lo:hi, :] * jnp.exp(bc[lo:hi, :] - anchor)
        e = anchor - bc
        if i < n_sub - 1:
            e = jnp.where(row < hi, e, 0.0)
        ki = kc * jnp.exp(e)
        blocks.append(_dot_nt(qi.astype(BF16), ki.astype(BF16)))
    att = blocks[0] if n_sub == 1 else jnp.concatenate(blocks, axis=0)
    rr = lax.broadcasted_iota(jnp.int32, (chunk, chunk), 0)
    cc = lax.broadcasted_iota(jnp.int32, (chunk, chunk), 1)
    att = jnp.where(cc <= rr, att, 0.0)
    o = o + _dot(att.astype(BF16), vc.astype(BF16))
    b_last = bc[chunk - 1:chunk, :]
    ke = kc * jnp.exp(b_last - bc)
    st_new = st * jnp.exp(b_last) + _dot_tn(vc.astype(BF16), ke.astype(BF16))
    return o, st_new


def _mix_in_kernel(*refs, nseq, rows, chunk, sub, per_seq_state, emit_attn_kv):
    it = iter(refs)
    x_ref = next(it); mod_ref = next(it); nw_ref = next(it); win_ref = next(it); lb_ref = next(it)
    hgw_ref = next(it); qnw_ref = next(it); knw_ref = next(it)
    s0_ref = next(it) if per_seq_state else None
    ya_ref = next(it); q_ref = next(it); kf_ref = next(it); vf_ref = next(it); gb_ref = next(it)
    kb_ref = next(it) if emit_attn_kv else None
    vt_ref = next(it) if emit_attn_kv else None
    sfin_ref = next(it)
    st_scr = next(it); b_scr = next(it); q_scr = next(it); k_scr = next(it); v_scr = next(it); o_scr = next(it)

    tm = nseq * rows
    d = x_ref.shape[-1]
    n_chunks = tm // chunk

    x3 = x_ref[...]
    ms = jnp.mean(x3 * x3, axis=-1, keepdims=True)
    shift = mod_ref[:, :, 0:d]
    scale = mod_ref[:, :, d:2 * d]
    h3 = x3 * lax.rsqrt(ms + EPS) * nw_ref[...] * (1.0 + scale) + shift
    h = h3.reshape(tm, d).astype(BF16)

    def proj(idx):
        return _dot(h, win_ref[:, idx * GROUP_W:(idx + 1) * GROUP_W])

    gw = 2 * LANES
    gi = _div_pow2(lax.broadcasted_iota(jnp.int32, (gw, gw), 0), HALF)
    gj = _div_pow2(lax.broadcasted_iota(jnp.int32, (gw, gw), 1), HALF)
    gmat = jnp.where(gi == gj, 1.0 / HALF, 0.0).astype(BF16)

    qb = proj(5)
    qn = qb * lax.rsqrt(_group_mean_sq(qb, gmat) + EPS) * qnw_ref[...] * (HALF ** -0.5)
    q_ref[...] = qn.reshape(nseq, rows, GROUP_W).astype(q_ref.dtype)
    kb = proj(6)
    kn = kb * lax.rsqrt(_group_mean_sq(kb, gmat) + EPS) * knw_ref[...]
    kf_ref[...] = kn.reshape(nseq, rows, GROUP_W)
    vb = proj(7)
    vf_ref[...] = vb.reshape(nseq, rows, GROUP_W)
    if emit_attn_kv:
        kb_ref[...] = kn.reshape(nseq, rows, GROUP_W).astype(BF16)
        for hh in range(N_HEADS):
            vt_ref[0, hh, 0] = vb[:, hh * HEAD_W:(hh + 1) * HEAD_W].T.astype(BF16)
    zb = proj(8)
    sgz, _ = _sigmoid_pair(zb)
    gb_ref[...] = (zb * sgz).reshape(nseq, rows, GROUP_W)

    lb = lb_ref[...]
    fa = proj(1)
    sig, nsig = _sigmoid_pair(fa)
    logf = jnp.log(lb + (1.0 - lb) * sig)
    k_scr[...] = (1.0 - lb) * nsig
    q_scr[...] = proj(0) * (HEAD_W ** -0.5)
    v_scr[...] = proj(2)
    ri = lax.broadcasted_iota(jnp.int32, (tm, tm), 0)
    ci = lax.broadcasted_iota(jnp.int32, (tm, tm), 1)
    same_chunk = _div_pow2(ri, chunk) == _div_pow2(ci, chunk)
    tri = jnp.where(ci <= ri, jnp.where(same_chunk, 1.0, 0.0), 0.0).astype(BF16)
    l1, l2, l3 = _split3(logf)
    b_scr[...] = _dot(tri, l1) + _dot(tri, l2) + _dot(tri, l3)

    if not per_seq_state:
        @pl.when(pl.program_id(1) == 0)
        def _():
            st_scr[...] = jnp.zeros_like(st_scr)

    def chunk_body(c, carry):
        r0 = pl.multiple_of(c * chunk, chunk)
        for hh in range(N_HEADS):
            hs = slice(hh * HEAD_W, (hh + 1) * HEAD_W)
            if per_seq_state:
                st = s0_ref[c, hh].T
            else:
                st = st_scr[hh]
            o, st_new = _hgrn_chunk(b_scr[pl.ds(r0, chunk), hs], q_scr[pl.ds(r0, chunk), hs],
                                    k_scr[pl.ds(r0, chunk), hs], v_scr[pl.ds(r0, chunk), hs],
                                    st, chunk, sub)
            o_scr[pl.ds(r0, chunk), hs] = o
            if per_seq_state:
                sfin_ref[c, hh] = st_new.T
            else:
                st_scr[hh] = st_new
        return carry

    lax.fori_loop(0, n_chunks, chunk_body, 0)

    if not per_seq_state:
        @pl.when(pl.program_id(1) == pl.num_programs(1) - 1)
        def _():
            for hh in range(N_HEADS):
                sfin_ref[0, hh] = st_scr[hh].T

    sg, _ = _sigmoid_pair(proj(3))
    za = proj(4)
    sz, _ = _sigmoid_pair(za)
    outs = []
    for hh in range(N_HEADS):
        hs = slice(hh * HEAD_W, (hh + 1) * HEAD_W)
        oh = o_scr[:, hs] * sg[:, hs]
        mo = jnp.mean(oh * oh, axis=-1, keepdims=True)
        outs.append(oh * lax.rsqrt(mo + EPS))
    ya = jnp.concatenate(outs, axis=1) * hgw_ref[...] * (za * sz)
    ya_ref[...] = ya.reshape(nseq, rows, GROUP_W).astype(ya_ref.dtype)


def _mix_in(x, mod, norm_w, w_in, lb, hgw, qnw, knw, *, nseq, rows, chunk, sub, s0, attn_tile, q_dtype):
    n_s, r_total, d = x.shape
    per_seq_state = s0 is not None
    emit = attn_tile is not None
    tm = nseq * rows
    if per_seq_state:
        assert rows == r_total and rows == chunk and n_s % nseq == 0
        grid = (n_s // nseq,)
        tok = lambda i: (i, 0, 0)
        seq = lambda i: (i, 0, 0)
        const2 = lambda i: (0, 0)
        st_map = lambda i: (i, 0, 0, 0)
        sem = ("arbitrary",)
    else:
        assert nseq == 1 and r_total % rows == 0 and rows % chunk == 0
        grid = (n_s, r_total // rows)
        tok = lambda b, t: (b, t, 0)
        seq = lambda b, t: (b, 0, 0)
        const2 = lambda b, t: (0, 0)
        st_map = lambda b, t: (b, 0, 0, 0)
        sem = ("parallel", "arbitrary")
    if emit:
        assert rows == attn_tile
    d_in = w_in.shape[1]
    tok_spec = pl.BlockSpec((nseq, rows, GROUP_W), tok)
    in_specs = [pl.BlockSpec((nseq, rows, d), tok),
                pl.BlockSpec((nseq, 1, 3 * d), seq),
                pl.BlockSpec((1, d), const2),
                pl.BlockSpec((d, d_in), const2),
                pl.BlockSpec((1, GROUP_W), const2),
                pl.BlockSpec((1, GROUP_W), const2),
                pl.BlockSpec((1, GROUP_W), const2),
                pl.BlockSpec((1, GROUP_W), const2)]
    args = [x, mod, norm_w, w_in, lb, hgw, qnw, knw]
    if per_seq_state:
        in_specs.append(pl.BlockSpec((nseq, N_HEADS, HEAD_W, HEAD_W), st_map))
        args.append(s0)
    out_shape = [jax.ShapeDtypeStruct((n_s, r_total, GROUP_W), q_dtype),
                 jax.ShapeDtypeStruct((n_s, r_total, GROUP_W), q_dtype),
                 jax.ShapeDtypeStruct((n_s, r_total, GROUP_W), F32),
                 jax.ShapeDtypeStruct((n_s, r_total, GROUP_W), F32),
                 jax.ShapeDtypeStruct((n_s, r_total, GROUP_W), F32)]
    out_specs = [tok_spec] * 5
    if emit:
        nt = r_total // attn_tile
        out_shape += [jax.ShapeDtypeStruct((n_s, r_total, GROUP_W), BF16),
                      jax.ShapeDtypeStruct((n_s, N_HEADS, nt, HEAD_W, attn_tile), BF16)]
        out_specs += [tok_spec,
                      pl.BlockSpec((1, N_HEADS, 1, HEAD_W, attn_tile), lambda b, t: (b, 0, t, 0, 0))]
    out_shape.append(jax.ShapeDtypeStruct((n_s, N_HEADS, HEAD_W, HEAD_W), F32))
    out_specs.append(pl.BlockSpec((nseq, N_HEADS, HEAD_W, HEAD_W), st_map))
    scratch = [pltpu.VMEM((N_HEADS, HEAD_W, HEAD_W), F32)] + [pltpu.VMEM((tm, GROUP_W), F32)] * 5
    kern = functools.partial(_mix_in_kernel, nseq=nseq, rows=rows, chunk=chunk, sub=sub,
                             per_seq_state=per_seq_state, emit_attn_kv=emit)
    return pl.pallas_call(
        kern, out_shape=tuple(out_shape), grid=grid, in_specs=in_specs, out_specs=tuple(out_specs),
        scratch_shapes=scratch, compiler_params=_cparams(sem),
        name="mix_in_sample" if per_seq_state else "mix_in_prompt",
    )(*args)


def _sub_norm_gate(o, snw, gate):
    ms = jnp.mean(o * o, axis=-1, keepdims=True)
    return o * lax.rsqrt(ms + EPS) * snw * (1.0 - LAMBDA_INIT) * gate


def _attn_kernel(lam_ref, q_ref, k_ref, vt_ref, bias_ref, g_ref, snw_ref, o_ref,
                 qs_scr, m_scr, l_scr, acc_scr, *, tile, n_bias):
    qi = pl.program_id(2)
    q = q_ref[0].astype(F32)
    lane = lax.broadcasted_iota(jnp.int32, q.shape, 1)
    qs_scr[0:tile, :] = jnp.where(lane < HALF, q, 0.0).astype(BF16)
    qs_scr[tile:2 * tile, :] = jnp.where(lane >= HALF, q, 0.0).astype(BF16)
    m_scr[...] = jnp.full(m_scr.shape, NEG, F32)
    l_scr[...] = jnp.zeros_like(l_scr)
    acc_scr[...] = jnp.zeros_like(acc_scr)

    def body(ki, carry):
        r0 = pl.multiple_of(ki * tile, tile)
        kt = k_ref[0, pl.ds(r0, tile), :]
        s = _dot_nt(kt, qs_scr[...])
        bt = bias_ref[0, jnp.minimum(qi - ki, n_bias - 1)]
        s = s + jnp.concatenate([bt, bt], axis=1)
        m_old = m_scr[...]
        m_new = jnp.maximum(m_old, jnp.max(s, axis=0, keepdims=True))
        alpha = jnp.exp(m_old - m_new)
        p = jnp.exp(s - m_new)
        l_scr[...] = alpha * l_scr[...] + jnp.sum(p, axis=0, keepdims=True)
        acc_scr[...] = acc_scr[...] * alpha + _dot(vt_ref[0, 0, ki], p.astype(BF16))
        m_scr[...] = m_new
        return carry

    lax.fori_loop(0, qi + 1, body, 0)

    lam = lam_ref[0, 0]
    inv = 1.0 / l_scr[...]
    acc = acc_scr[...]
    o_t = acc[:, 0:tile] * inv[:, 0:tile] - lam * (acc[:, tile:2 * tile] * inv[:, tile:2 * tile])
    o_ref[0] = _sub_norm_gate(o_t.T, snw_ref[...], g_ref[0]).astype(o_ref.dtype)


def _attn_prompt(lam, q, kb, vt, bias, gate, snw, tile):
    b, l, _ = q.shape
    nq = l // tile
    nd = bias.shape[1]
    return pl.pallas_call(
        functools.partial(_attn_kernel, tile=tile, n_bias=nd),
        out_shape=jax.ShapeDtypeStruct((b, l, GROUP_W), BF16),
        grid=(b, N_HEADS, nq),
        in_specs=[pl.BlockSpec(memory_space=pltpu.SMEM),
                  pl.BlockSpec((1, tile, HEAD_W), lambda bb, h, i: (bb, i, h)),
                  pl.BlockSpec((1, l, HEAD_W), lambda bb, h, i: (bb, 0, h)),
                  pl.BlockSpec((1, 1, nq, HEAD_W, tile), lambda bb, h, i: (bb, h, 0, 0, 0)),
                  pl.BlockSpec((1, nd, tile, tile), lambda bb, h, i: (h, 0, 0, 0)),
                  pl.BlockSpec((1, tile, HEAD_W), lambda bb, h, i: (bb, i, h)),
                  pl.BlockSpec((1, HEAD_W), lambda bb, h, i: (0, 0))],
        out_specs=pl.BlockSpec((1, tile, HEAD_W), lambda bb, h, i: (bb, i, h)),
        scratch_shapes=[pltpu.VMEM((2 * tile, HEAD_W), BF16),
                        pltpu.VMEM((1, 2 * tile), F32),
                        pltpu.VMEM((1, 2 * tile), F32),
                        pltpu.VMEM((HEAD_W, 2 * tile), F32)],
        compiler_params=_cparams(("parallel", "parallel", "arbitrary")),
        name="attn_prompt",
    )(lam, q, kb, vt, bias, gate, snw)


def _decode_kernel(pt_ref, lam_ref, q_ref, kn_ref, vn_ref, g_ref, snw_ref, bias_ref, biasn_ref, *rest,
                   n_pages, rows, page):
    k_refs = rest[:n_pages]
    v_refs = rest[n_pages:2 * n_pages]
    o_ref = rest[2 * n_pages]
    qbd_scr, m_scr, l_scr, acc_scr, kpg_scr, vpg_scr = rest[2 * n_pages + 1:]
    g = pl.program_id(1)
    n_rows = 2 * N_HEADS * rows

    @pl.when(g == 0)
    def _():
        q = q_ref[0].astype(F32)
        qt = jnp.concatenate([q] * (2 * N_HEADS), axis=0)
        rgrp = _div_pow2(lax.broadcasted_iota(jnp.int32, qt.shape, 0), rows)
        cgrp = _div_pow2(lax.broadcasted_iota(jnp.int32, qt.shape, 1), HALF)
        qbd_scr[...] = jnp.where(rgrp == cgrp, qt, 0.0).astype(BF16)
        m_scr[...] = jnp.full(m_scr.shape, NEG, F32)
        l_scr[...] = jnp.zeros_like(l_scr)
        acc_scr[...] = jnp.zeros_like(acc_scr)
        kpg_scr[...] = jnp.zeros_like(kpg_scr)
        vpg_scr[...] = jnp.zeros_like(vpg_scr)
        kpg_scr[0:rows, :] = kn_ref[0]
        vpg_scr[0:rows, :] = vn_ref[0]

    def update(s, values):
        m_old = m_scr[...]
        m_new = jnp.maximum(m_old, jnp.max(s, axis=-1, keepdims=True))
        alpha = jnp.exp(m_old - m_new)
        p = jnp.exp(s - m_new)
        l_scr[...] = alpha * l_scr[...] + jnp.sum(p, axis=-1, keepdims=True)
        pb = p.astype(BF16)
        pv = _dot(pb[:, 0:page], values[0])
        for j in range(1, len(values)):
            pv = pv + _dot(pb[:, j * page:(j + 1) * page], values[j])
        acc_scr[...] = acc_scr[...] * alpha + pv
        m_scr[...] = m_new

    qbd = qbd_scr[...]
    scores = [_dot_nt(qbd, k_refs[j][0].astype(BF16)) for j in range(n_pages)]
    s = scores[0] if n_pages == 1 else jnp.concatenate(scores, axis=1)
    update(s + bias_ref[...], [v_refs[j][0].astype(BF16) for j in range(n_pages)])

    @pl.when(g == pl.num_programs(1) - 1)
    def _():
        s_new = _dot_nt(qbd, kpg_scr[...].astype(BF16)) + biasn_ref[...]
        update(s_new, [vpg_scr[...].astype(BF16)])
        lam = lam_ref[0, 0]
        inv = 1.0 / l_scr[...]
        acc = acc_scr[...]
        gate = g_ref[0]
        for hh in range(N_HEADS):
            hs = slice(hh * HEAD_W, (hh + 1) * HEAD_W)
            r0 = (2 * hh) * rows
            o0 = acc[r0:r0 + rows, hs] * inv[r0:r0 + rows, :]
            o1 = acc[r0 + rows:r0 + 2 * rows, hs] * inv[r0 + rows:r0 + 2 * rows, :]
            o_ref[0, :, hs] = _sub_norm_gate(o0 - lam * o1, snw_ref[...], gate[:, hs]).astype(o_ref.dtype)


def _attn_decode(page_table, lam, q, k_new, v_new, gate, snw, bias_dec, bias_new, cache_k, cache_v, n_pages):
    nb, rows, _ = q.shape
    pages_per_seq = page_table.shape[1]
    page = cache_k.shape[1]
    assert pages_per_seq % n_pages == 0 and page == LANES
    ng = pages_per_seq // n_pages
    n_rows = 2 * N_HEADS * rows
    seq = lambda b, g, pt: (b, 0, 0)
    in_specs = [pl.BlockSpec(memory_space=pltpu.SMEM),
                pl.BlockSpec((1, rows, GROUP_W), seq),
                pl.BlockSpec((1, rows, GROUP_W), seq),
                pl.BlockSpec((1, rows, GROUP_W), seq),
                pl.BlockSpec((1, rows, GROUP_W), seq),
                pl.BlockSpec((1, HEAD_W), lambda b, g, pt: (0, 0)),
                pl.BlockSpec((n_rows, n_pages * page), lambda b, g, pt: (0, g)),
                pl.BlockSpec((n_rows, LANES), lambda b, g, pt: (0, 0))]
    for cache in (cache_k, cache_v):
        for j in range(n_pages):
            in_specs.append(pl.BlockSpec((1, page, GROUP_W),
                                         lambda b, g, pt, j=j: (pt[b, g * n_pages + j], 0, 0)))
    grid_spec = pltpu.PrefetchScalarGridSpec(
        num_scalar_prefetch=1, grid=(nb, ng), in_specs=in_specs,
        out_specs=pl.BlockSpec((1, rows, GROUP_W), seq),
        scratch_shapes=[pltpu.VMEM((n_rows, GROUP_W), BF16),
                        pltpu.VMEM((n_rows, 1), F32),
                        pltpu.VMEM((n_rows, 1), F32),
                        pltpu.VMEM((n_rows, GROUP_W), F32),
                        pltpu.VMEM((page, GROUP_W), F32),
                        pltpu.VMEM((page, GROUP_W), F32)])
    return pl.pallas_call(
        functools.partial(_decode_kernel, n_pages=n_pages, rows=rows, page=page),
        out_shape=jax.ShapeDtypeStruct((nb, rows, GROUP_W), F32),
        grid_spec=grid_spec,
        compiler_params=_cparams(("parallel", "arbitrary")),
        name="attn_decode",
    )(page_table, lam, q, k_new, v_new, gate, snw, bias_dec, bias_new,
      *([cache_k] * n_pages), *([cache_v] * n_pages))


def _mix_out_kernel(ya_ref, yb_ref, x_ref, mod_ref, w_ref, o_ref, *, nseq, rows):
    d = x_ref.shape[-1]
    tm = nseq * rows
    ya = ya_ref[...].reshape(tm, GROUP_W).astype(BF16)
    yb = yb_ref[...].reshape(tm, GROUP_W).astype(BF16)
    out = _dot(ya, w_ref[0:GROUP_W, :]) + _dot(yb, w_ref[GROUP_W:2 * GROUP_W, :])
    gate = mod_ref[:, :, 2 * d:3 * d]
    o_ref[...] = x_ref[...] + gate * out.reshape(nseq, rows, d)


def _mix_out(ya, yb, x, mod, w_out, *, nseq, rows):
    n_s, r_total, d = x.shape
    grid = (n_s // nseq, r_total // rows)
    tok = lambda s, t: (s, t, 0)
    return pl.pallas_call(
        functools.partial(_mix_out_kernel, nseq=nseq, rows=rows),
        out_shape=jax.ShapeDtypeStruct(x.shape, F32),
        grid=grid,
        in_specs=[pl.BlockSpec((nseq, rows, GROUP_W), tok),
                  pl.BlockSpec((nseq, rows, GROUP_W), tok),
                  pl.BlockSpec((nseq, rows, d), tok),
                  pl.BlockSpec((nseq, 1, 3 * d), lambda s, t: (s, 0, 0)),
                  pl.BlockSpec((2 * GROUP_W, d), lambda s, t: (0, 0))],
        out_specs=pl.BlockSpec((nseq, rows, d), tok),
        compiler_params=_cparams(("parallel", "arbitrary")),
        name="mix_out",
    )(ya, yb, x, mod, w_out)


def _largest_tile(n, cap):
    t = cap
    while n % t:
        t //= 2
    return t


def kernel(x_prompt, x_sample, c_prompt, c_sample, cache_k, cache_v, state_hgrn, page_table, norm_w, w_ada,
           b_ada, w_in, w_out, lb_param, hg_norm_w, q_norm_w, k_norm_w, lam_q1, lam_k1, lam_q2, lam_k2,
           sub_norm_w, rel_table):
    depth = w_in.shape[0]
    assert depth == 1
    b, l, d = x_prompt.shape
    db, dl, _ = x_sample.shape
    assert d == 2 * GROUP_W and w_in.shape[2] == 9 * GROUP_W
    page = cache_k.shape[2]
    past = page_table.shape[1] * page
    assert dl % SUBLANES == 0 and cache_k.shape[3] * cache_k.shape[4] == GROUP_W

    w_in_b = w_in[0].astype(BF16)
    w_out_b = w_out[0].astype(BF16)
    nw = norm_w.reshape(1, d)
    hgw = jnp.tile(hg_norm_w.reshape(1, HEAD_W), (1, N_HEADS))
    qnw = jnp.tile(q_norm_w.reshape(1, HALF), (1, GROUP_W // HALF))
    knw = jnp.tile(k_norm_w.reshape(1, HALF), (1, GROUP_W // HALF))
    snw = sub_norm_w.reshape(1, HEAD_W)

    mod = _adaln_mod(jnp.concatenate([c_prompt, c_sample], axis=0), w_ada[0], b_ada[0])
    mod_p = mod[:b].reshape(b, 1, 3 * d)
    mod_s = mod[b:].reshape(db, 1, 3 * d)
    lb, lam = _small_params(lb_param, lam_q1, lam_k1, lam_q2, lam_k2)

    tile = _largest_tile(l, 256)
    ya, q, kf, vf, gb, kb, vt, st_p = _mix_in(
        x_prompt, mod_p, nw, w_in_b, lb, hgw, qnw, knw, nseq=1, rows=tile,
        chunk=CHUNK if l % CHUNK == 0 else l, sub=SUB, s0=None, attn_tile=tile, q_dtype=BF16)
    bias_p = _bias_prompt(rel_table, tile)
    yb = _attn_prompt(lam, q, kb, vt, bias_p, gb, snw, tile)
    y_prompt = _mix_out(ya, yb, x_prompt, mod_p, w_out_b, nseq=1, rows=_largest_tile(l, 512))

    nseq = _largest_tile(db, 8)
    ya_s, q_s, kf_s, vf_s, gb_s, st_s = _mix_in(
        x_sample, mod_s, nw, w_in_b, lb, hgw, qnw, knw, nseq=nseq, rows=dl,
        chunk=dl, sub=dl, s0=state_hgrn[0], attn_tile=None, q_dtype=F32)
    bias_d, bias_n = _bias_decode(rel_table, dl, past)
    n_rows = 2 * N_HEADS * dl
    yb_s = _attn_decode(page_table, lam, q_s, kf_s, vf_s, gb_s, snw,
                        bias_d.reshape(n_rows, past), bias_n.reshape(n_rows, LANES),
                        cache_k[0].reshape(-1, page, GROUP_W), cache_v[0].reshape(-1, page, GROUP_W),
                        n_pages=_largest_tile(page_table.shape[1], 8))
    y_sample = _mix_out(ya_s, yb_s, x_sample, mod_s, w_out_b, nseq=nseq, rows=dl)

    hb = N_HEADS
    return (y_prompt, y_sample,
            kf.reshape(1, b, l, hb, HEAD_W), vf.reshape(1, b, l, hb, HEAD_W), st_p[None],
            kf_s.reshape(1, db, dl, hb, HEAD_W), vf_s.reshape(1, db, dl, hb, HEAD_W), st_s[None])
```

```python
import functools
import math

import jax
import jax.numpy as jnp
from jax import lax
from jax.experimental import pallas as pl
from jax.experimental.pallas import tpu as pltpu

F32 = jnp.float32
BF16 = jnp.bfloat16

N_HEADS = 4
HEAD_W = 128
HALF = 64
GROUP_W = N_HEADS * HEAD_W
N_BUCKETS = 32
MAX_EXACT = N_BUCKETS // 2
MAX_DIST = 1024
CHUNK = 64
SUB = 16
EPS = 1e-6
LAMBDA_INIT = 0.8 - 0.6 * math.exp(-0.3 * 0)
NEG = -1e30
ATTN_TQ = 512
ATTN_TK = 256

LANES = 128
SUBLANES = 8
VMEM_LIMIT_BYTES = 56 * 1024 * 1024


def _bucket_lower_bounds():
    los = list(range(MAX_EXACT))
    span = N_BUCKETS - MAX_EXACT
    ratio = MAX_DIST // MAX_EXACT
    for m in range(MAX_EXACT, N_BUCKETS):
        d = MAX_EXACT
        while d ** span < (MAX_EXACT ** span) * (ratio ** (m - MAX_EXACT)):
            d += 1
        los.append(d)
    return los


BUCKET_LO = _bucket_lower_bounds()
SATURATED_DIST = BUCKET_LO[-1]


def _cparams(sem):
    return pltpu.CompilerParams(dimension_semantics=sem, vmem_limit_bytes=VMEM_LIMIT_BYTES)


def _split2(x):
    hi = x.astype(BF16)
    lo = (x - hi.astype(F32)).astype(BF16)
    return hi, lo


def _split3(x):
    hi = x.astype(BF16)
    r = x - hi.astype(F32)
    mid = r.astype(BF16)
    lo = (r - mid.astype(F32)).astype(BF16)
    return hi, mid, lo


def _dot(a, b):
    return jnp.dot(a, b, preferred_element_type=F32)


def _dot_nt(a, b):
    return lax.dot_general(a, b, (((1,), (1,)), ((), ())), preferred_element_type=F32)


def _dot_tn(a, b):
    return lax.dot_general(a, b, (((0,), (0,)), ((), ())), preferred_element_type=F32)


def _div_pow2(x, n):
    assert n > 0 and n & (n - 1) == 0
    return lax.shift_right_logical(x, n.bit_length() - 1)


def _sigmoid_pair(x):
    t = jnp.exp(-jnp.abs(x))
    r = 1.0 / (1.0 + t)
    tr = t * r
    pos = x >= 0
    return jnp.where(pos, r, tr), jnp.where(pos, tr, r)


def _mod_kernel(c_ref, w_ref, b_ref, o_ref):
    c = c_ref[...]
    sig, _ = _sigmoid_pair(c)
    s = c * sig
    w = w_ref[...]
    s1, s2 = _split2(s)
    w1, w2 = _split2(w)
    o_ref[...] = _dot(s1, w1) + (_dot(s1, w2) + _dot(s2, w1)) + b_ref[...]


def _adaln_mod(c_all, w_ada, b_ada):
    n, d = c_all.shape
    d3 = w_ada.shape[1]
    bn = d
    return pl.pallas_call(
        _mod_kernel,
        out_shape=jax.ShapeDtypeStruct((n, d3), F32),
        grid=(d3 // bn,),
        in_specs=[pl.BlockSpec((n, d), lambda j: (0, 0)),
                  pl.BlockSpec((d, bn), lambda j: (0, j)),
                  pl.BlockSpec((1, bn), lambda j: (0, j))],
        out_specs=pl.BlockSpec((n, bn), lambda j: (0, j)),
        compiler_params=_cparams(("arbitrary",)),
        name="adaln_mod",
    )(c_all, w_ada, b_ada.reshape(1, d3))


def _small_kernel(lbp_ref, q1_ref, k1_ref, q2_ref, k2_ref, lb_ref, lam_ref):
    p = lbp_ref[...]
    m = jnp.max(p, axis=0, keepdims=True)
    e = jnp.exp(p - m)
    lb_ref[...] = e[0:1, :] / jnp.sum(e, axis=0, keepdims=True)
    s1 = jnp.sum(q1_ref[...] * k1_ref[...], axis=-1, keepdims=True)
    s2 = jnp.sum(q2_ref[...] * k2_ref[...], axis=-1, keepdims=True)
    lam_ref[...] = jnp.exp(s1) - jnp.exp(s2) + LAMBDA_INIT


def _small_params(lb_param, lq1, lk1, lq2, lk2):
    return pl.pallas_call(
        _small_kernel,
        out_shape=(jax.ShapeDtypeStruct((1, lb_param.shape[1]), F32),
                   jax.ShapeDtypeStruct((1, 1), F32)),
        name="small_params",
    )(lb_param, lq1, lk1, lq2, lk2)


def _bias_select(d, tab_ref, h):
    val = jnp.full(d.shape, tab_ref[N_BUCKETS - 1, h], F32)
    for m in range(N_BUCKETS - 2, -1, -1):
        val = jnp.where(d < BUCKET_LO[m + 1], tab_ref[m, h], val)
    return val


def _bias_prompt_kernel(tab_ref, o_ref, *, tk, tq, scale):
    h = pl.program_id(0)
    dl = pl.program_id(1) - (tq // tk - 1)
    kk = lax.broadcasted_iota(jnp.int32, (tk, tq), 0)
    qq = lax.broadcasted_iota(jnp.int32, (tk, tq), 1)
    d = dl * tk + qq - kk
    val = _bias_select(jnp.maximum(d, 0), tab_ref, h) * scale
    o_ref[0, 0] = jnp.where(d >= 0, val, NEG)


def _n_bias_tiles(tk, tq):
    n_const = 0
    while n_const * tk - (tk - 1) < SATURATED_DIST:
        n_const += 1
    return n_const + 1 + (tq // tk - 1)


def _bias_prompt(rel_table, tk, tq, scale):
    nd = _n_bias_tiles(tk, tq)
    return pl.pallas_call(
        functools.partial(_bias_prompt_kernel, tk=tk, tq=tq, scale=scale),
        out_shape=jax.ShapeDtypeStruct((N_HEADS, nd, tk, tq), F32),
        grid=(N_HEADS, nd),
        in_specs=[pl.BlockSpec(memory_space=pltpu.SMEM)],
        out_specs=pl.BlockSpec((1, 1, tk, tq), lambda h, d: (h, d, 0, 0)),
        compiler_params=_cparams(("arbitrary", "arbitrary")),
        name="bias_prompt",
    )(rel_table)


def _bias_decode_kernel(tab_ref, o_ref, on_ref, *, rows, past):
    h = pl.program_id(0)
    i = lax.broadcasted_iota(jnp.int32, (rows, past), 0)
    kpos = lax.broadcasted_iota(jnp.int32, (rows, past), 1)
    val = _bias_select(past + i - kpos, tab_ref, h)
    o_ref[0, 0] = val
    o_ref[0, 1] = val
    i2 = lax.broadcasted_iota(jnp.int32, (rows, LANES), 0)
    j2 = lax.broadcasted_iota(jnp.int32, (rows, LANES), 1)
    d2 = i2 - j2
    vn = jnp.where(d2 >= 0, _bias_select(jnp.maximum(d2, 0), tab_ref, h), NEG)
    on_ref[0, 0] = vn
    on_ref[0, 1] = vn


def _bias_decode(rel_table, rows, past):
    return pl.pallas_call(
        functools.partial(_bias_decode_kernel, rows=rows, past=past),
        out_shape=(jax.ShapeDtypeStruct((N_HEADS, 2, rows, past), F32),
                   jax.ShapeDtypeStruct((N_HEADS, 2, rows, LANES), F32)),
        grid=(N_HEADS,),
        in_specs=[pl.BlockSpec(memory_space=pltpu.SMEM)],
        out_specs=(pl.BlockSpec((1, 2, rows, past), lambda h: (h, 0, 0, 0)),
                   pl.BlockSpec((1, 2, rows, LANES), lambda h: (h, 0, 0, 0))),
        compiler_params=_cparams(("arbitrary",)),
        name="bias_decode",
    )(rel_table)


def _group_mean_sq(x, gmat):
    sq = x * x
    outs = []
    w = gmat.shape[0]
    for c0 in range(0, x.shape[1], w):
        hi, lo = _split2(sq[:, c0:c0 + w])
        outs.append(_dot(hi, gmat) + _dot(lo, gmat))
    return jnp.concatenate(outs, axis=1)


def _hgrn_chunk(bc, qc, kc, vc, st, chunk, sub):
    n_sub = chunk // sub
    o = _dot_nt((qc * jnp.exp(bc)).astype(BF16), st.astype(BF16))
    row = lax.broadcasted_iota(jnp.int32, (chunk, HEAD_W), 0)
    blocks = []
    for i in range(n_sub):
        lo, hi = i * sub, (i + 1) * sub
        mid = lo + sub // 2 - 1
        anchor = bc[mid:mid + 1, :]
        qi = qc[lo:hi, :] * jnp.exp(bc[lo:hi, :] - anchor)
        e = anchor - bc
        if i < n_sub - 1:
            e = jnp.where(row < hi, e, 0.0)
        ki = kc * jnp.exp(e)
        blocks.append(_dot_nt(qi.astype(BF16), ki.astype(BF16)))
    att = blocks[0] if n_sub == 1 else jnp.concatenate(blocks, axis=0)
    rr = lax.broadcasted_iota(jnp.int32, (chunk, chunk), 0)
    cc = lax.broadcasted_iota(jnp.int32, (chunk, chunk), 1)
    att = jnp.where(cc <= rr, att, 0.0)
    o = o + _dot(att.astype(BF16), vc.astype(BF16))
    b_last = bc[chunk - 1:chunk, :]
    ke = kc * jnp.exp(b_last - bc)
    st_new = st * jnp.exp(b_last) + _dot_tn(vc.astype(BF16), ke.astype(BF16))
    return o, st_new


def _mix_in_kernel(*refs, nseq, rows, chunk, sub, per_seq_state, emit_attn_kv, q_scale):
    it = iter(refs)
    x_ref = next(it); mod_ref = next(it); nw_ref = next(it); win_ref = next(it); lb_ref = next(it)
    hgw_ref = next(it); qnw_ref = next(it); knw_ref = next(it)
    s0_ref = next(it) if per_seq_state else None
    ya_ref = next(it); q_ref = next(it); kf_ref = next(it); vf_ref = next(it); gb_ref = next(it)
    kb_ref = next(it) if emit_attn_kv else None
    vt_ref = next(it) if emit_attn_kv else None
    sfin_ref = next(it)
    st_scr = next(it); b_scr = next(it); q_scr = next(it); k_scr = next(it); v_scr = next(it); o_scr = next(it)

    tm = nseq * rows
    d = x_ref.shape[-1]
    n_chunks = tm // chunk

    x3 = x_ref[...]
    ms = jnp.mean(x3 * x3, axis=-1, keepdims=True)
    shift = mod_ref[:, :, 0:d]
    scale = mod_ref[:, :, d:2 * d]
    h3 = x3 * lax.rsqrt(ms + EPS) * nw_ref[...] * (1.0 + scale) + shift
    h = h3.reshape(tm, d).astype(BF16)

    def proj(idx):
        return _dot(h, win_ref[:, idx * GROUP_W:(idx + 1) * GROUP_W])

    gw = 2 * LANES
    gi = _div_pow2(lax.broadcasted_iota(jnp.int32, (gw, gw), 0), HALF)
    gj = _div_pow2(lax.broadcasted_iota(jnp.int32, (gw, gw), 1), HALF)
    gmat = jnp.where(gi == gj, 1.0 / HALF, 0.0).astype(BF16)

    qb = proj(5)
    qn = qb * lax.rsqrt(_group_mean_sq(qb, gmat) + EPS) * qnw_ref[...] * q_scale
    q_ref[...] = qn.reshape(nseq, rows, GROUP_W).astype(q_ref.dtype)
    kb = proj(6)
    kn = kb * lax.rsqrt(_group_mean_sq(kb, gmat) + EPS) * knw_ref[...]
    vb = proj(7)
    for hh in range(N_HEADS):
        hs = slice(hh * HEAD_W, (hh + 1) * HEAD_W)
        kf_ref[:, pl.ds(hh, rows, stride=N_HEADS), :] = kn[:, hs].reshape(nseq, rows, HEAD_W)
        vf_ref[:, pl.ds(hh, rows, stride=N_HEADS), :] = vb[:, hs].reshape(nseq, rows, HEAD_W)
    if emit_attn_kv:
        kb_ref[...] = kn.reshape(nseq, rows, GROUP_W).astype(BF16)
        for hh in range(N_HEADS):
            vt_ref[0, hh, 0] = vb[:, hh * HEAD_W:(hh + 1) * HEAD_W].T.astype(BF16)
    zb = proj(8)
    sgz, _ = _sigmoid_pair(zb)
    gb_ref[...] = (zb * sgz).reshape(nseq, rows, GROUP_W)

    lb = lb_ref[...]
    fa = proj(1)
    sig, nsig = _sigmoid_pair(fa)
    logf = jnp.log(lb + (1.0 - lb) * sig)
    k_scr[...] = (1.0 - lb) * nsig
    q_scr[...] = proj(0) * (HEAD_W ** -0.5)
    v_scr[...] = proj(2)
    ri = lax.broadcasted_iota(jnp.int32, (tm, tm), 0)
    ci = lax.broadcasted_iota(jnp.int32, (tm, tm), 1)
    same_chunk = _div_pow2(ri, chunk) == _div_pow2(ci, chunk)
    tri = jnp.where(ci <= ri, jnp.where(same_chunk, 1.0, 0.0), 0.0).astype(BF16)
    l1, l2, l3 = _split3(logf)
    b_scr[...] = _dot(tri, l1) + _dot(tri, l2) + _dot(tri, l3)

    if not per_seq_state:
        @pl.when(pl.program_id(1) == 0)
        def _():
            st_scr[...] = jnp.zeros_like(st_scr)

    def chunk_body(c, carry):
        r0 = pl.multiple_of(c * chunk, chunk)
        for hh in range(N_HEADS):
            hs = slice(hh * HEAD_W, (hh + 1) * HEAD_W)
            if per_seq_state:
                st = s0_ref[c, hh].T
            else:
                st = st_scr[hh]
            o, st_new = _hgrn_chunk(b_scr[pl.ds(r0, chunk), hs], q_scr[pl.ds(r0, chunk), hs],
                                    k_scr[pl.ds(r0, chunk), hs], v_scr[pl.ds(r0, chunk), hs],
                                    st, chunk, sub)
            o_scr[pl.ds(r0, chunk), hs] = o
            if per_seq_state:
                sfin_ref[c, hh] = st_new.T
            else:
                st_scr[hh] = st_new
        return carry

    lax.fori_loop(0, n_chunks, chunk_body, 0)

    if not per_seq_state:
        @pl.when(pl.program_id(1) == pl.num_programs(1) - 1)
        def _():
            for hh in range(N_HEADS):
                sfin_ref[0, hh] = st_scr[hh].T

    sg, _ = _sigmoid_pair(proj(3))
    za = proj(4)
    sz, _ = _sigmoid_pair(za)
    outs = []
    for hh in range(N_HEADS):
        hs = slice(hh * HEAD_W, (hh + 1) * HEAD_W)
        oh = o_scr[:, hs] * sg[:, hs]
        mo = jnp.mean(oh * oh, axis=-1, keepdims=True)
        outs.append(oh * lax.rsqrt(mo + EPS))
    ya = jnp.concatenate(outs, axis=1) * hgw_ref[...] * (za * sz)
    ya_ref[...] = ya.reshape(nseq, rows, GROUP_W).astype(ya_ref.dtype)


def _mix_in(x, mod, norm_w, w_in, lb, hgw, qnw, knw, *, nseq, rows, chunk, sub, s0, attn_tile, q_dtype, q_scale):
    n_s, r_total, d = x.shape
    per_seq_state = s0 is not None
    emit = attn_tile is not None
    tm = nseq * rows
    if per_seq_state:
        assert rows == r_total and rows == chunk and n_s % nseq == 0
        grid = (n_s // nseq,)
        tok = lambda i: (i, 0, 0)
        seq = lambda i: (i, 0, 0)
        const2 = lambda i: (0, 0)
        st_map = lambda i: (i, 0, 0, 0)
        sem = ("arbitrary",)
    else:
        assert nseq == 1 and r_total % rows == 0 and rows % chunk == 0
        grid = (n_s, r_total // rows)
        tok = lambda b, t: (b, t, 0)
        seq = lambda b, t: (b, 0, 0)
        const2 = lambda b, t: (0, 0)
        st_map = lambda b, t: (b, 0, 0, 0)
        sem = ("parallel", "arbitrary")
    if emit:
        assert rows == attn_tile
    d_in = w_in.shape[1]
    tok_spec = pl.BlockSpec((nseq, rows, GROUP_W), tok)
    tokhead_spec = pl.BlockSpec((nseq, rows * N_HEADS, HEAD_W), tok)
    in_specs = [pl.BlockSpec((nseq, rows, d), tok),
                pl.BlockSpec((nseq, 1, 3 * d), seq),
                pl.BlockSpec((1, d), const2),
                pl.BlockSpec((d, d_in), const2),
                pl.BlockSpec((1, GROUP_W), const2),
                pl.BlockSpec((1, GROUP_W), const2),
                pl.BlockSpec((1, GROUP_W), const2),
                pl.BlockSpec((1, GROUP_W), const2)]
    args = [x, mod, norm_w, w_in, lb, hgw, qnw, knw]
    if per_seq_state:
        in_specs.append(pl.BlockSpec((nseq, N_HEADS, HEAD_W, HEAD_W), st_map))
        args.append(s0)
    out_shape = [jax.ShapeDtypeStruct((n_s, r_total, GROUP_W), q_dtype),
                 jax.ShapeDtypeStruct((n_s, r_total, GROUP_W), q_dtype),
                 jax.ShapeDtypeStruct((n_s, r_total * N_HEADS, HEAD_W), F32),
                 jax.ShapeDtypeStruct((n_s, r_total * N_HEADS, HEAD_W), F32),
                 jax.ShapeDtypeStruct((n_s, r_total, GROUP_W), F32)]
    out_specs = [tok_spec, tok_spec, tokhead_spec, tokhead_spec, tok_spec]
    if emit:
        nt = r_total // attn_tile
        out_shape += [jax.ShapeDtypeStruct((n_s, r_total, GROUP_W), BF16),
                      jax.ShapeDtypeStruct((n_s, N_HEADS, nt, HEAD_W, attn_tile), BF16)]
        out_specs += [tok_spec,
                      pl.BlockSpec((1, N_HEADS, 1, HEAD_W, attn_tile), lambda b, t: (b, 0, t, 0, 0))]
    out_shape.append(jax.ShapeDtypeStruct((n_s, N_HEADS, HEAD_W, HEAD_W), F32))
    out_specs.append(pl.BlockSpec((nseq, N_HEADS, HEAD_W, HEAD_W), st_map))
    scratch = [pltpu.VMEM((N_HEADS, HEAD_W, HEAD_W), F32)] + [pltpu.VMEM((tm, GROUP_W), F32)] * 5
    kern = functools.partial(_mix_in_kernel, nseq=nseq, rows=rows, chunk=chunk, sub=sub,
                             per_seq_state=per_seq_state, emit_attn_kv=emit, q_scale=q_scale)
    return pl.pallas_call(
        kern, out_shape=tuple(out_shape), grid=grid, in_specs=in_specs, out_specs=tuple(out_specs),
        scratch_shapes=scratch, compiler_params=_cparams(sem),
        name="mix_in_sample" if per_seq_state else "mix_in_prompt",
    )(*args)


def _sub_norm_gate(o, snw, gate):
    ms = jnp.mean(o * o, axis=-1, keepdims=True)
    return o * lax.rsqrt(ms + EPS) * snw * (1.0 - LAMBDA_INIT) * gate


def _attn_kernel(lam_ref, q_ref, k_ref, vt_ref, bias_ref, g_ref, snw_ref, o_ref,
                 qs_scr, sa_scr, sb_scr, m_scr, l_scr, acc_scr, *, tq, tk, n_bias):
    qi = pl.program_id(2)
    ratio = tq // tk
    n_tiles = ratio * (qi + 1)
    q = q_ref[0].astype(F32)
    lane = lax.broadcasted_iota(jnp.int32, q.shape, 1)
    qs_scr[0:tq, :] = jnp.where(lane < HALF, q, 0.0).astype(BF16)
    qs_scr[tq:2 * tq, :] = jnp.where(lane >= HALF, q, 0.0).astype(BF16)
    m_scr[...] = jnp.full(m_scr.shape, NEG, F32)
    l_scr[...] = jnp.zeros_like(l_scr)
    acc_scr[...] = jnp.zeros_like(acc_scr)

    def scores(t, s_ref):
        r0 = pl.multiple_of(t * tk, tk)
        s_ref[...] = _dot_nt(k_ref[0, pl.ds(r0, tk), :], qs_scr[...])

    def softmax_pv(t, s_ref):
        bt = bias_ref[0, jnp.clip(ratio * qi - t + (ratio - 1), 0, n_bias - 1)]
        vt = vt_ref[0, 0, t]
        for c in range(2):
            cols = slice(c * tq, (c + 1) * tq)
            s = s_ref[:, cols] + bt
            m_old = m_scr[:, cols]
            m_new = jnp.maximum(m_old, jnp.max(s, axis=0, keepdims=True))
            alpha = jnp.exp2(m_old - m_new)
            p = jnp.exp2(s - m_new)
            l_scr[:, cols] = alpha * l_scr[:, cols] + jnp.sum(p, axis=0, keepdims=True)
            acc_scr[:, cols] = acc_scr[:, cols] * alpha + _dot(vt, p.astype(BF16))
            m_scr[:, cols] = m_new

    scores(0, sa_scr)

    def body(j, carry):
        t0 = 2 * j
        scores(t0 + 1, sb_scr)
        softmax_pv(t0, sa_scr)
        scores(jnp.minimum(t0 + 2, n_tiles - 1), sa_scr)
        softmax_pv(t0 + 1, sb_scr)
        return carry

    lax.fori_loop(0, n_tiles // 2, body, 0)

    lam = lam_ref[0, 0]
    inv = 1.0 / l_scr[...]
    acc = acc_scr[...]
    o_t = acc[:, 0:tq] * inv[:, 0:tq] - lam * (acc[:, tq:2 * tq] * inv[:, tq:2 * tq])
    o_ref[0] = _sub_norm_gate(o_t.T, snw_ref[...], g_ref[0]).astype(o_ref.dtype)


def _attn_prompt(lam, q, kb, vt, bias, gate, snw, tq, tk):
    b, l, _ = q.shape
    assert tq % (2 * tk) == 0 and l % tq == 0
    nq = l // tq
    nt = l // tk
    nd = bias.shape[1]
    return pl.pallas_call(
        functools.partial(_attn_kernel, tq=tq, tk=tk, n_bias=nd),
        out_shape=jax.ShapeDtypeStruct((b, l, GROUP_W), BF16),
        grid=(b, N_HEADS, nq),
        in_specs=[pl.BlockSpec(memory_space=pltpu.SMEM),
                  pl.BlockSpec((1, tq, HEAD_W), lambda bb, h, i: (bb, i, h)),
                  pl.BlockSpec((1, l, HEAD_W), lambda bb, h, i: (bb, 0, h)),
                  pl.BlockSpec((1, 1, nt, HEAD_W, tk), lambda bb, h, i: (bb, h, 0, 0, 0)),
                  pl.BlockSpec((1, nd, tk, tq), lambda bb, h, i: (h, 0, 0, 0)),
                  pl.BlockSpec((1, tq, HEAD_W), lambda bb, h, i: (bb, i, h)),
                  pl.BlockSpec((1, HEAD_W), lambda bb, h, i: (0, 0))],
        out_specs=pl.BlockSpec((1, tq, HEAD_W), lambda bb, h, i: (bb, i, h)),
        scratch_shapes=[pltpu.VMEM((2 * tq, HEAD_W), BF16),
                        pltpu.VMEM((tk, 2 * tq), F32),
                        pltpu.VMEM((tk, 2 * tq), F32),
                        pltpu.VMEM((1, 2 * tq), F32),
                        pltpu.VMEM((1, 2 * tq), F32),
                        pltpu.VMEM((HEAD_W, 2 * tq), F32)],
        compiler_params=_cparams(("parallel", "parallel", "arbitrary")),
        name="attn_prompt",
    )(lam, q, kb, vt, bias, gate, snw)


def _decode_kernel(pt_ref, lam_ref, q_ref, kn_ref, vn_ref, g_ref, snw_ref, bias_ref, biasn_ref, *rest,
                   n_pages, rows, page):
    k_refs = rest[:n_pages]
    v_refs = rest[n_pages:2 * n_pages]
    o_ref = rest[2 * n_pages]
    qst_scr, m_scr, l_scr, acc_scr, kpg_scr, vpg_scr = rest[2 * n_pages + 1:]
    g = pl.program_id(1)
    hr = 2 * rows

    def head_page(ref, hh):
        return ref[0, pl.ds(hh, page, stride=N_HEADS), :].astype(BF16)

    @pl.when(g == 0)
    def _():
        q = q_ref[0].astype(F32)
        lane = lax.broadcasted_iota(jnp.int32, (rows, HEAD_W), 1)
        kpg_scr[...] = jnp.zeros_like(kpg_scr)
        vpg_scr[...] = jnp.zeros_like(vpg_scr)
        for hh in range(N_HEADS):
            qh = q[:, hh * HEAD_W:(hh + 1) * HEAD_W]
            qst_scr[hh] = jnp.concatenate([jnp.where(lane < HALF, qh, 0.0), jnp.where(lane >= HALF, qh, 0.0)],
                                          axis=0).astype(BF16)
            kpg_scr[hh, 0:rows, :] = kn_ref[0, pl.ds(hh, rows, stride=N_HEADS), :]
            vpg_scr[hh, 0:rows, :] = vn_ref[0, pl.ds(hh, rows, stride=N_HEADS), :]
        m_scr[...] = jnp.full(m_scr.shape, NEG, F32)
        l_scr[...] = jnp.zeros_like(l_scr)
        acc_scr[...] = jnp.zeros_like(acc_scr)

    def update(s, n_blocks, value_fn):
        m_old = m_scr[...]
        m_new = jnp.maximum(m_old, jnp.max(s, axis=-1, keepdims=True))
        alpha = jnp.exp(m_old - m_new)
        p = jnp.exp(s - m_new)
        l_scr[...] = alpha * l_scr[...] + jnp.sum(p, axis=-1, keepdims=True)
        pb = p.astype(BF16)
        pv_rows = []
        for hh in range(N_HEADS):
            ph = pb[hh * hr:(hh + 1) * hr, :]
            pv = _dot(ph[:, 0:page], value_fn(hh, 0))
            for j in range(1, n_blocks):
                pv = pv + _dot(ph[:, j * page:(j + 1) * page], value_fn(hh, j))
            pv_rows.append(pv)
        acc_scr[...] = acc_scr[...] * alpha + jnp.concatenate(pv_rows, axis=0)
        m_scr[...] = m_new

    def scores(n_blocks, key_fn):
        s_rows = []
        for hh in range(N_HEADS):
            qh = qst_scr[hh]
            blocks = [_dot_nt(qh, key_fn(hh, j)) for j in range(n_blocks)]
            s_rows.append(blocks[0] if n_blocks == 1 else jnp.concatenate(blocks, axis=1))
        return jnp.concatenate(s_rows, axis=0)

    s = scores(n_pages, lambda hh, j: head_page(k_refs[j], hh)) + bias_ref[...]
    update(s, n_pages, lambda hh, j: head_page(v_refs[j], hh))

    @pl.when(g == pl.num_programs(1) - 1)
    def _():
        s_new = scores(1, lambda hh, j: kpg_scr[hh].astype(BF16)) + biasn_ref[...]
        update(s_new, 1, lambda hh, j: vpg_scr[hh].astype(BF16))
        lam = lam_ref[0, 0]
        inv = 1.0 / l_scr[...]
        acc = acc_scr[...] * inv
        gate = g_ref[0]
        for hh in range(N_HEADS):
            hs = slice(hh * HEAD_W, (hh + 1) * HEAD_W)
            r0 = hh * hr
            o = acc[r0:r0 + rows, :] - lam * acc[r0 + rows:r0 + 2 * rows, :]
            o_ref[0, :, hs] = _sub_norm_gate(o, snw_ref[...], gate[:, hs]).astype(o_ref.dtype)


def _attn_decode(page_table, lam, q, k_new, v_new, gate, snw, bias_dec, bias_new, cache_k, cache_v, n_pages):
    nb, rows, _ = q.shape
    pages_per_seq = page_table.shape[1]
    page = cache_k.shape[1] // N_HEADS
    assert pages_per_seq % n_pages == 0 and page == LANES
    ng = pages_per_seq // n_pages
    n_rows = 2 * N_HEADS * rows
    seq = lambda b, g, pt: (b, 0, 0)
    in_specs = [pl.BlockSpec(memory_space=pltpu.SMEM),
                pl.BlockSpec((1, rows, GROUP_W), seq),
                pl.BlockSpec((1, rows * N_HEADS, HEAD_W), seq),
                pl.BlockSpec((1, rows * N_HEADS, HEAD_W), seq),
                pl.BlockSpec((1, rows, GROUP_W), seq),
                pl.BlockSpec((1, HEAD_W), lambda b, g, pt: (0, 0)),
                pl.BlockSpec((n_rows, n_pages * page), lambda b, g, pt: (0, g)),
                pl.BlockSpec((n_rows, LANES), lambda b, g, pt: (0, 0))]
    for cache in (cache_k, cache_v):
        for j in range(n_pages):
            in_specs.append(pl.BlockSpec((1, page * N_HEADS, HEAD_W),
                                         lambda b, g, pt, j=j: (pt[b, g * n_pages + j], 0, 0)))
    grid_spec = pltpu.PrefetchScalarGridSpec(
        num_scalar_prefetch=1, grid=(nb, ng), in_specs=in_specs,
        out_specs=pl.BlockSpec((1, rows, GROUP_W), seq),
        scratch_shapes=[pltpu.VMEM((N_HEADS, 2 * rows, HEAD_W), BF16),
                        pltpu.VMEM((n_rows, 1), F32),
                        pltpu.VMEM((n_rows, 1), F32),
                        pltpu.VMEM((n_rows, HEAD_W), F32),
                        pltpu.VMEM((N_HEADS, page, HEAD_W), F32),
                        pltpu.VMEM((N_HEADS, page, HEAD_W), F32)])
    return pl.pallas_call(
        functools.partial(_decode_kernel, n_pages=n_pages, rows=rows, page=page),
        out_shape=jax.ShapeDtypeStruct((nb, rows, GROUP_W), F32),
        grid_spec=grid_spec,
        compiler_params=_cparams(("parallel", "arbitrary")),
        name="attn_decode",
    )(page_table, lam, q, k_new, v_new, gate, snw, bias_dec, bias_new,
      *([cache_k] * n_pages), *([cache_v] * n_pages))


def _mix_out_kernel(ya_ref, yb_ref, x_ref, mod_ref, w_ref, o_ref, *, nseq, rows):
    d = x_ref.shape[-1]
    tm = nseq * rows
    ya = ya_ref[...].reshape(tm, GROUP_W).astype(BF16)
    yb = yb_ref[...].reshape(tm, GROUP_W).astype(BF16)
    out = _dot(ya, w_ref[0:GROUP_W, :]) + _dot(yb, w_ref[GROUP_W:2 * GROUP_W, :])
    gate = mod_ref[:, :, 2 * d:3 * d]
    o_ref[...] = x_ref[...] + gate * out.reshape(nseq, rows, d)


def _mix_out(ya, yb, x, mod, w_out, *, nseq, rows):
    n_s, r_total, d = x.shape
    grid = (n_s // nseq, r_total // rows)
    tok = lambda s, t: (s, t, 0)
    return pl.pallas_call(
        functools.partial(_mix_out_kernel, nseq=nseq, rows=rows),
        out_shape=jax.ShapeDtypeStruct(x.shape, F32),
        grid=grid,
        in_specs=[pl.BlockSpec((nseq, rows, GROUP_W), tok),
                  pl.BlockSpec((nseq, rows, GROUP_W), tok),
                  pl.BlockSpec((nseq, rows, d), tok),
                  pl.BlockSpec((nseq, 1, 3 * d), lambda s, t: (s, 0, 0)),
                  pl.BlockSpec((2 * GROUP_W, d), lambda s, t: (0, 0))],
        out_specs=pl.BlockSpec((nseq, rows, d), tok),
        compiler_params=_cparams(("parallel", "arbitrary")),
        name="mix_out",
    )(ya, yb, x, mod, w_out)


def _largest_tile(n, cap):
    t = cap
    while n % t:
        t //= 2
    return t


def kernel(x_prompt, x_sample, c_prompt, c_sample, cache_k, cache_v, state_hgrn, page_table, norm_w, w_ada,
           b_ada, w_in, w_out, lb_param, hg_norm_w, q_norm_w, k_norm_w, lam_q1, lam_k1, lam_q2, lam_k2,
           sub_norm_w, rel_table):
    depth = w_in.shape[0]
    assert depth == 1
    b, l, d = x_prompt.shape
    db, dl, _ = x_sample.shape
    assert d == 2 * GROUP_W and w_in.shape[2] == 9 * GROUP_W
    page = cache_k.shape[2]
    past = page_table.shape[1] * page
    assert dl % SUBLANES == 0 and cache_k.shape[3] * cache_k.shape[4] == GROUP_W

    w_in_b = w_in[0].astype(BF16)
    w_out_b = w_out[0].astype(BF16)
    nw = norm_w.reshape(1, d)
    hgw = jnp.tile(hg_norm_w.reshape(1, HEAD_W), (1, N_HEADS))
    qnw = jnp.tile(q_norm_w.reshape(1, HALF), (1, GROUP_W // HALF))
    knw = jnp.tile(k_norm_w.reshape(1, HALF), (1, GROUP_W // HALF))
    snw = sub_norm_w.reshape(1, HEAD_W)

    mod = _adaln_mod(jnp.concatenate([c_prompt, c_sample], axis=0), w_ada[0], b_ada[0])
    mod_p = mod[:b].reshape(b, 1, 3 * d)
    mod_s = mod[b:].reshape(db, 1, 3 * d)
    lb, lam = _small_params(lb_param, lam_q1, lam_k1, lam_q2, lam_k2)

    log2e = math.log2(math.e)
    ya, q, kf, vf, gb, kb, vt, st_p = _mix_in(
        x_prompt, mod_p, nw, w_in_b, lb, hgw, qnw, knw, nseq=1, rows=ATTN_TK,
        chunk=CHUNK if l % CHUNK == 0 else l, sub=SUB, s0=None, attn_tile=ATTN_TK, q_dtype=BF16,
        q_scale=HALF ** -0.5 * log2e)
    bias_p = _bias_prompt(rel_table, ATTN_TK, ATTN_TQ, log2e)
    yb = _attn_prompt(lam, q, kb, vt, bias_p, gb, snw, ATTN_TQ, ATTN_TK)
    y_prompt = _mix_out(ya, yb, x_prompt, mod_p, w_out_b, nseq=1, rows=_largest_tile(l, 512))

    nseq = _largest_tile(db, 8)
    ya_s, q_s, kf_s, vf_s, gb_s, st_s = _mix_in(
        x_sample, mod_s, nw, w_in_b, lb, hgw, qnw, knw, nseq=nseq, rows=dl,
        chunk=dl, sub=dl, s0=state_hgrn[0], attn_tile=None, q_dtype=F32, q_scale=HALF ** -0.5)
    bias_d, bias_n = _bias_decode(rel_table, dl, past)
    n_rows = 2 * N_HEADS * dl
    yb_s = _attn_decode(page_table, lam, q_s, kf_s, vf_s, gb_s, snw,
                        bias_d.reshape(n_rows, past), bias_n.reshape(n_rows, LANES),
                        cache_k.reshape(-1, page * N_HEADS, HEAD_W), cache_v.reshape(-1, page * N_HEADS, HEAD_W),
                        n_pages=_largest_tile(page_table.shape[1], 8))
    y_sample = _mix_out(ya_s, yb_s, x_sample, mod_s, w_out_b, nseq=nseq, rows=dl)

    hb = N_HEADS
    return (y_prompt, y_sample,
            kf.reshape(1, b, l, hb, HEAD_W), vf.reshape(1, b, l, hb, HEAD_W), st_p[None],
            kf_s.reshape(1, db, dl, hb, HEAD_W), vf_s.reshape(1, db, dl, hb, HEAD_W), st_s[None])
```

```python
import functools
import math

import jax
import jax.numpy as jnp
from jax import lax
from jax.experimental import pallas as pl
from jax.experimental.pallas import tpu as pltpu

F32 = jnp.float32
BF16 = jnp.bfloat16

N_HEADS = 4
HEAD_W = 128
HALF = 64
GROUP_W = N_HEADS * HEAD_W
N_BUCKETS = 32
MAX_EXACT = N_BUCKETS // 2
MAX_DIST = 1024
CHUNK = 64
SUB = 16
EPS = 1e-6
LAMBDA_INIT = 0.8 - 0.6 * math.exp(-0.3 * 0)
NEG = -1e30
ATTN_TQ = 512
ATTN_TK = 256
VT_ROWS = HEAD_W + 16

LANES = 128
SUBLANES = 8
VMEM_LIMIT_BYTES = 56 * 1024 * 1024


def _bucket_lower_bounds():
    los = list(range(MAX_EXACT))
    span = N_BUCKETS - MAX_EXACT
    ratio = MAX_DIST // MAX_EXACT
    for m in range(MAX_EXACT, N_BUCKETS):
        d = MAX_EXACT
        while d ** span < (MAX_EXACT ** span) * (ratio ** (m - MAX_EXACT)):
            d += 1
        los.append(d)
    return los


BUCKET_LO = _bucket_lower_bounds()
SATURATED_DIST = BUCKET_LO[-1]


def _cparams(sem):
    return pltpu.CompilerParams(dimension_semantics=sem, vmem_limit_bytes=VMEM_LIMIT_BYTES)


def _split2(x):
    hi = x.astype(BF16)
    lo = (x - hi.astype(F32)).astype(BF16)
    return hi, lo


def _split3(x):
    hi = x.astype(BF16)
    r = x - hi.astype(F32)
    mid = r.astype(BF16)
    lo = (r - mid.astype(F32)).astype(BF16)
    return hi, mid, lo


def _dot(a, b):
    return jnp.dot(a, b, preferred_element_type=F32)


def _dot_nt(a, b):
    return lax.dot_general(a, b, (((1,), (1,)), ((), ())), preferred_element_type=F32)


def _dot_tn(a, b):
    return lax.dot_general(a, b, (((0,), (0,)), ((), ())), preferred_element_type=F32)


def _div_pow2(x, n):
    assert n > 0 and n & (n - 1) == 0
    return lax.shift_right_logical(x, n.bit_length() - 1)


def _sigmoid_pair(x):
    t = jnp.exp(-jnp.abs(x))
    r = 1.0 / (1.0 + t)
    tr = t * r
    pos = x >= 0
    return jnp.where(pos, r, tr), jnp.where(pos, tr, r)


def _mod_kernel(c_ref, w_ref, b_ref, o_ref):
    c = c_ref[...]
    sig, _ = _sigmoid_pair(c)
    s = c * sig
    w = w_ref[...]
    s1, s2 = _split2(s)
    w1, w2 = _split2(w)
    o_ref[...] = _dot(s1, w1) + (_dot(s1, w2) + _dot(s2, w1)) + b_ref[...]


def _adaln_mod(c_all, w_ada, b_ada):
    n, d = c_all.shape
    d3 = w_ada.shape[1]
    bn = d
    return pl.pallas_call(
        _mod_kernel,
        out_shape=jax.ShapeDtypeStruct((n, d3), F32),
        grid=(d3 // bn,),
        in_specs=[pl.BlockSpec((n, d), lambda j: (0, 0)),
                  pl.BlockSpec((d, bn), lambda j: (0, j)),
                  pl.BlockSpec((1, bn), lambda j: (0, j))],
        out_specs=pl.BlockSpec((n, bn), lambda j: (0, j)),
        compiler_params=_cparams(("arbitrary",)),
        name="adaln_mod",
    )(c_all, w_ada, b_ada.reshape(1, d3))


def _small_kernel(lbp_ref, q1_ref, k1_ref, q2_ref, k2_ref, lb_ref, lam_ref):
    p = lbp_ref[...]
    m = jnp.max(p, axis=0, keepdims=True)
    e = jnp.exp(p - m)
    lb_ref[...] = e[0:1, :] / jnp.sum(e, axis=0, keepdims=True)
    s1 = jnp.sum(q1_ref[...] * k1_ref[...], axis=-1, keepdims=True)
    s2 = jnp.sum(q2_ref[...] * k2_ref[...], axis=-1, keepdims=True)
    lam_ref[...] = jnp.exp(s1) - jnp.exp(s2) + LAMBDA_INIT


def _small_params(lb_param, lq1, lk1, lq2, lk2):
    return pl.pallas_call(
        _small_kernel,
        out_shape=(jax.ShapeDtypeStruct((1, lb_param.shape[1]), F32),
                   jax.ShapeDtypeStruct((1, 1), F32)),
        name="small_params",
    )(lb_param, lq1, lk1, lq2, lk2)


def _bias_select(d, tab_ref, h):
    val = jnp.full(d.shape, tab_ref[N_BUCKETS - 1, h], F32)
    for m in range(N_BUCKETS - 2, -1, -1):
        val = jnp.where(d < BUCKET_LO[m + 1], tab_ref[m, h], val)
    return val


def _bias_prompt_kernel(tab_ref, o_ref, *, tk, tq, scale):
    h = pl.program_id(0)
    dl = pl.program_id(1) - (tq // tk - 1)
    kk = lax.broadcasted_iota(jnp.int32, (tk, tq), 0)
    qq = lax.broadcasted_iota(jnp.int32, (tk, tq), 1)
    d = dl * tk + qq - kk
    val = _bias_select(jnp.maximum(d, 0), tab_ref, h) * scale
    o_ref[0, 0] = jnp.where(d >= 0, val, NEG)


def _n_bias_tiles(tk, tq):
    n_const = 0
    while n_const * tk - (tk - 1) < SATURATED_DIST:
        n_const += 1
    return n_const + 1 + (tq // tk - 1)


def _bias_prompt(rel_table, tk, tq, scale):
    nd = _n_bias_tiles(tk, tq)
    return pl.pallas_call(
        functools.partial(_bias_prompt_kernel, tk=tk, tq=tq, scale=scale),
        out_shape=jax.ShapeDtypeStruct((N_HEADS, nd, tk, tq), F32),
        grid=(N_HEADS, nd),
        in_specs=[pl.BlockSpec(memory_space=pltpu.SMEM)],
        out_specs=pl.BlockSpec((1, 1, tk, tq), lambda h, d: (h, d, 0, 0)),
        compiler_params=_cparams(("arbitrary", "arbitrary")),
        name="bias_prompt",
    )(rel_table)


def _bias_decode_kernel(tab_ref, o_ref, on_ref, *, rows, past):
    h = pl.program_id(0)
    i = lax.broadcasted_iota(jnp.int32, (rows, past), 0)
    kpos = lax.broadcasted_iota(jnp.int32, (rows, past), 1)
    val = _bias_select(past + i - kpos, tab_ref, h)
    o_ref[0, 0] = val
    o_ref[0, 1] = val
    i2 = lax.broadcasted_iota(jnp.int32, (rows, LANES), 0)
    j2 = lax.broadcasted_iota(jnp.int32, (rows, LANES), 1)
    d2 = i2 - j2
    vn = jnp.where(d2 >= 0, _bias_select(jnp.maximum(d2, 0), tab_ref, h), NEG)
    on_ref[0, 0] = vn
    on_ref[0, 1] = vn


def _bias_decode(rel_table, rows, past):
    return pl.pallas_call(
        functools.partial(_bias_decode_kernel, rows=rows, past=past),
        out_shape=(jax.ShapeDtypeStruct((N_HEADS, 2, rows, past), F32),
                   jax.ShapeDtypeStruct((N_HEADS, 2, rows, LANES), F32)),
        grid=(N_HEADS,),
        in_specs=[pl.BlockSpec(memory_space=pltpu.SMEM)],
        out_specs=(pl.BlockSpec((1, 2, rows, past), lambda h: (h, 0, 0, 0)),
                   pl.BlockSpec((1, 2, rows, LANES), lambda h: (h, 0, 0, 0))),
        compiler_params=_cparams(("arbitrary",)),
        name="bias_decode",
    )(rel_table)


def _group_mean_sq(x, gmat):
    sq = x * x
    outs = []
    w = gmat.shape[0]
    for c0 in range(0, x.shape[1], w):
        hi, lo = _split2(sq[:, c0:c0 + w])
        outs.append(_dot(hi, gmat) + _dot(lo, gmat))
    return jnp.concatenate(outs, axis=1)


def _hgrn_intra_mask(chunk, sub):
    n_sub = chunk // sub
    rr = lax.broadcasted_iota(jnp.int32, (chunk, n_sub * chunk), 0)
    cc = lax.broadcasted_iota(jnp.int32, (chunk, n_sub * chunk), 1)
    own_block = _div_pow2(cc, chunk) == _div_pow2(rr, sub)
    causal = jnp.bitwise_and(cc, chunk - 1) <= rr
    return jnp.where(own_block, jnp.where(causal, 1.0, 0.0), 0.0)


def _hgrn_chunk(bc, qc, kc, vc, st, mask, chunk, sub):
    n_sub = chunk // sub
    o = _dot_nt((qc * jnp.exp(bc)).astype(BF16), st.astype(BF16))
    row = lax.broadcasted_iota(jnp.int32, (chunk, HEAD_W), 0)
    q_parts, k_parts = [], []
    for i in range(n_sub):
        lo, hi = i * sub, (i + 1) * sub
        mid = lo + sub // 2 - 1
        anchor = bc[mid:mid + 1, :]
        q_parts.append(qc[lo:hi, :] * jnp.exp(bc[lo:hi, :] - anchor))
        e = anchor - bc
        if i < n_sub - 1:
            e = jnp.where(row < hi, e, 0.0)
        k_parts.append((kc * jnp.exp(e)).astype(BF16))
    qa = q_parts[0] if n_sub == 1 else jnp.concatenate(q_parts, axis=0)
    ka = k_parts[0] if n_sub == 1 else jnp.concatenate(k_parts, axis=0)
    att = jnp.where(mask > 0.5, _dot_nt(qa.astype(BF16), ka), 0.0)
    vb = vc.astype(BF16)
    vv = vb if n_sub == 1 else jnp.concatenate([vb] * n_sub, axis=0)
    o = o + _dot(att.astype(BF16), vv)
    b_last = bc[chunk - 1:chunk, :]
    ke = kc * jnp.exp(b_last - bc)
    st_new = st * jnp.exp(b_last) + _dot_tn(vc.astype(BF16), ke.astype(BF16))
    return o, st_new


def _mix_in_kernel(*refs, nseq, rows, chunk, sub, per_seq_state, emit_attn_kv, q_scale):
    it = iter(refs)
    x_ref = next(it); mod_ref = next(it); nw_ref = next(it); win_ref = next(it); lb_ref = next(it)
    hgw_ref = next(it); qnw_ref = next(it); knw_ref = next(it)
    s0_ref = next(it) if per_seq_state else None
    ya_ref = next(it); q_ref = next(it); kf_ref = next(it); vf_ref = next(it); gb_ref = next(it)
    kb_ref = next(it) if emit_attn_kv else None
    vt_ref = next(it) if emit_attn_kv else None
    sfin_ref = next(it)
    st_scr = next(it); b_scr = next(it); q_scr = next(it); k_scr = next(it); v_scr = next(it); o_scr = next(it)

    tm = nseq * rows
    d = x_ref.shape[-1]
    n_chunks = tm // chunk

    x3 = x_ref[...]
    ms = jnp.mean(x3 * x3, axis=-1, keepdims=True)
    shift = mod_ref[:, :, 0:d]
    scale = mod_ref[:, :, d:2 * d]
    h3 = x3 * lax.rsqrt(ms + EPS) * nw_ref[...] * (1.0 + scale) + shift
    h = h3.reshape(tm, d).astype(BF16)

    def proj(idx):
        return _dot(h, win_ref[:, idx * GROUP_W:(idx + 1) * GROUP_W])

    gw = 2 * LANES
    gi = _div_pow2(lax.broadcasted_iota(jnp.int32, (gw, gw), 0), HALF)
    gj = _div_pow2(lax.broadcasted_iota(jnp.int32, (gw, gw), 1), HALF)
    gmat = jnp.where(gi == gj, 1.0 / HALF, 0.0).astype(BF16)

    qb = proj(5)
    qn = qb * lax.rsqrt(_group_mean_sq(qb, gmat) + EPS) * qnw_ref[...] * q_scale
    q_ref[...] = qn.reshape(nseq, rows, GROUP_W).astype(q_ref.dtype)
    kb = proj(6)
    kn = kb * lax.rsqrt(_group_mean_sq(kb, gmat) + EPS) * knw_ref[...]
    vb = proj(7)
    for hh in range(N_HEADS):
        hs = slice(hh * HEAD_W, (hh + 1) * HEAD_W)
        kf_ref[:, pl.ds(hh, rows, stride=N_HEADS), :] = kn[:, hs].reshape(nseq, rows, HEAD_W)
        vf_ref[:, pl.ds(hh, rows, stride=N_HEADS), :] = vb[:, hs].reshape(nseq, rows, HEAD_W)
    if emit_attn_kv:
        kb_ref[...] = kn.reshape(nseq, rows, GROUP_W).astype(BF16)
        for hh in range(N_HEADS):
            vt_ref[0, hh, 0, 0:HEAD_W, :] = vb[:, hh * HEAD_W:(hh + 1) * HEAD_W].T.astype(BF16)
            vt_ref[0, hh, 0, HEAD_W:VT_ROWS, :] = jnp.ones((VT_ROWS - HEAD_W, tm), BF16)
    zb = proj(8)
    sgz, _ = _sigmoid_pair(zb)
    gb_ref[...] = (zb * sgz).reshape(nseq, rows, GROUP_W)

    lb = lb_ref[...]
    fa = proj(1)
    sig, nsig = _sigmoid_pair(fa)
    logf = jnp.log(lb + (1.0 - lb) * sig)
    k_scr[...] = (1.0 - lb) * nsig
    q_scr[...] = proj(0) * (HEAD_W ** -0.5)
    v_scr[...] = proj(2)
    ri = lax.broadcasted_iota(jnp.int32, (tm, tm), 0)
    ci = lax.broadcasted_iota(jnp.int32, (tm, tm), 1)
    same_chunk = _div_pow2(ri, chunk) == _div_pow2(ci, chunk)
    tri = jnp.where(ci <= ri, jnp.where(same_chunk, 1.0, 0.0), 0.0).astype(BF16)
    l1, l2, l3 = _split3(logf)
    b_scr[...] = _dot(tri, l1) + _dot(tri, l2) + _dot(tri, l3)

    if not per_seq_state:
        @pl.when(pl.program_id(1) == 0)
        def _():
            st_scr[...] = jnp.zeros_like(st_scr)

    intra_mask = _hgrn_intra_mask(chunk, sub)
    for hh in range(N_HEADS):
        hs = slice(hh * HEAD_W, (hh + 1) * HEAD_W)
        st = None if per_seq_state else st_scr[hh]
        for c in range(n_chunks):
            rs = slice(c * chunk, (c + 1) * chunk)
            if per_seq_state:
                st = s0_ref[c, hh].T
            o, st = _hgrn_chunk(b_scr[rs, hs], q_scr[rs, hs], k_scr[rs, hs], v_scr[rs, hs], st, intra_mask,
                                chunk, sub)
            o_scr[rs, hs] = o
            if per_seq_state:
                sfin_ref[c, hh] = st.T
        if not per_seq_state:
            st_scr[hh] = st

    if not per_seq_state:
        @pl.when(pl.program_id(1) == pl.num_programs(1) - 1)
        def _():
            for hh in range(N_HEADS):
                sfin_ref[0, hh] = st_scr[hh].T

    sg, _ = _sigmoid_pair(proj(3))
    za = proj(4)
    sz, _ = _sigmoid_pair(za)
    outs = []
    for hh in range(N_HEADS):
        hs = slice(hh * HEAD_W, (hh + 1) * HEAD_W)
        oh = o_scr[:, hs] * sg[:, hs]
        mo = jnp.mean(oh * oh, axis=-1, keepdims=True)
        outs.append(oh * lax.rsqrt(mo + EPS))
    ya = jnp.concatenate(outs, axis=1) * hgw_ref[...] * (za * sz)
    ya_ref[...] = ya.reshape(nseq, rows, GROUP_W).astype(ya_ref.dtype)


def _mix_in(x, mod, norm_w, w_in, lb, hgw, qnw, knw, *, nseq, rows, chunk, sub, s0, attn_tile, q_dtype, q_scale):
    n_s, r_total, d = x.shape
    per_seq_state = s0 is not None
    emit = attn_tile is not None
    tm = nseq * rows
    if per_seq_state:
        assert rows == r_total and rows == chunk and n_s % nseq == 0
        grid = (n_s // nseq,)
        tok = lambda i: (i, 0, 0)
        seq = lambda i: (i, 0, 0)
        const2 = lambda i: (0, 0)
        st_map = lambda i: (i, 0, 0, 0)
        sem = ("arbitrary",)
    else:
        assert nseq == 1 and r_total % rows == 0 and rows % chunk == 0
        grid = (n_s, r_total // rows)
        tok = lambda b, t: (b, t, 0)
        seq = lambda b, t: (b, 0, 0)
        const2 = lambda b, t: (0, 0)
        st_map = lambda b, t: (b, 0, 0, 0)
        sem = ("parallel", "arbitrary")
    if emit:
        assert rows == attn_tile
    d_in = w_in.shape[1]
    tok_spec = pl.BlockSpec((nseq, rows, GROUP_W), tok)
    tokhead_spec = pl.BlockSpec((nseq, rows * N_HEADS, HEAD_W), tok)
    in_specs = [pl.BlockSpec((nseq, rows, d), tok),
                pl.BlockSpec((nseq, 1, 3 * d), seq),
                pl.BlockSpec((1, d), const2),
                pl.BlockSpec((d, d_in), const2),
                pl.BlockSpec((1, GROUP_W), const2),
                pl.BlockSpec((1, GROUP_W), const2),
                pl.BlockSpec((1, GROUP_W), const2),
                pl.BlockSpec((1, GROUP_W), const2)]
    args = [x, mod, norm_w, w_in, lb, hgw, qnw, knw]
    if per_seq_state:
        in_specs.append(pl.BlockSpec((nseq, N_HEADS, HEAD_W, HEAD_W), st_map))
        args.append(s0)
    out_shape = [jax.ShapeDtypeStruct((n_s, r_total, GROUP_W), q_dtype),
                 jax.ShapeDtypeStruct((n_s, r_total, GROUP_W), q_dtype),
                 jax.ShapeDtypeStruct((n_s, r_total * N_HEADS, HEAD_W), F32),
                 jax.ShapeDtypeStruct((n_s, r_total * N_HEADS, HEAD_W), F32),
                 jax.ShapeDtypeStruct((n_s, r_total, GROUP_W), F32)]
    out_specs = [tok_spec, tok_spec, tokhead_spec, tokhead_spec, tok_spec]
    if emit:
        nt = r_total // attn_tile
        out_shape += [jax.ShapeDtypeStruct((n_s, r_total, GROUP_W), BF16),
                      jax.ShapeDtypeStruct((n_s, N_HEADS, nt, VT_ROWS, attn_tile), BF16)]
        out_specs += [tok_spec,
                      pl.BlockSpec((1, N_HEADS, 1, VT_ROWS, attn_tile), lambda b, t: (b, 0, t, 0, 0))]
    out_shape.append(jax.ShapeDtypeStruct((n_s, N_HEADS, HEAD_W, HEAD_W), F32))
    out_specs.append(pl.BlockSpec((nseq, N_HEADS, HEAD_W, HEAD_W), st_map))
    scratch = [pltpu.VMEM((N_HEADS, HEAD_W, HEAD_W), F32)] + [pltpu.VMEM((tm, GROUP_W), F32)] * 5
    kern = functools.partial(_mix_in_kernel, nseq=nseq, rows=rows, chunk=chunk, sub=sub,
                             per_seq_state=per_seq_state, emit_attn_kv=emit, q_scale=q_scale)
    return pl.pallas_call(
        kern, out_shape=tuple(out_shape), grid=grid, in_specs=in_specs, out_specs=tuple(out_specs),
        scratch_shapes=scratch, compiler_params=_cparams(sem),
        name="mix_in_sample" if per_seq_state else "mix_in_prompt",
    )(*args)


def _sub_norm_gate(o, snw, gate):
    ms = jnp.mean(o * o, axis=-1, keepdims=True)
    return o * lax.rsqrt(ms + EPS) * snw * (1.0 - LAMBDA_INIT) * gate


def _attn_kernel(lam_ref, q_ref, k_ref, vt_ref, bias_ref, g_ref, snw_ref, o_ref,
                 qs_scr, sa_scr, sb_scr, m_scr, acc_scr, *, tq, tk, n_bias):
    qi = pl.program_id(2)
    ratio = tq // tk
    n_tiles = ratio * (qi + 1)
    q = q_ref[0].astype(F32)
    lane = lax.broadcasted_iota(jnp.int32, q.shape, 1)
    qs_scr[0:tq, :] = jnp.where(lane < HALF, q, 0.0).astype(BF16)
    qs_scr[tq:2 * tq, :] = jnp.where(lane >= HALF, q, 0.0).astype(BF16)
    m_scr[...] = jnp.full(m_scr.shape, NEG, F32)
    acc_scr[...] = jnp.zeros_like(acc_scr)

    def scores(t, s_ref):
        r0 = pl.multiple_of(t * tk, tk)
        kt = k_ref[0, pl.ds(r0, tk), :]
        bt = bias_ref[0, jnp.clip(ratio * qi - t + (ratio - 1), 0, n_bias - 1)]
        for c in range(2):
            s_ref[:, c * tq:(c + 1) * tq] = _dot_nt(kt, qs_scr[c * tq:(c + 1) * tq, :]) + bt

    def softmax_pv(t, s_ref):
        vt = vt_ref[0, 0, t]
        for c in range(2):
            cols = slice(c * tq, (c + 1) * tq)
            m_old = m_scr[:, cols]
            m_new = jnp.maximum(m_old, jnp.max(s_ref[:, cols], axis=0, keepdims=True))
            alpha = jnp.exp2(m_old - m_new)
            p = jnp.exp2(s_ref[:, cols] - m_new)
            acc_scr[:, cols] = acc_scr[:, cols] * alpha + _dot(vt, p.astype(BF16))
            m_scr[:, cols] = m_new

    scores(0, sa_scr)

    def body(j, carry):
        t0 = 2 * j
        scores(t0 + 1, sb_scr)
        softmax_pv(t0, sa_scr)
        scores(jnp.minimum(t0 + 2, n_tiles - 1), sa_scr)
        softmax_pv(t0 + 1, sb_scr)
        return carry

    lax.fori_loop(0, n_tiles // 2, body, 0)

    lam = lam_ref[0, 0]
    inv = 1.0 / acc_scr[HEAD_W:HEAD_W + 1, :]
    acc = acc_scr[0:HEAD_W, :]
    o_t = acc[:, 0:tq] * inv[:, 0:tq] - lam * (acc[:, tq:2 * tq] * inv[:, tq:2 * tq])
    o_ref[0] = _sub_norm_gate(o_t.T, snw_ref[...], g_ref[0]).astype(o_ref.dtype)


def _attn_prompt(lam, q, kb, vt, bias, gate, snw, tq, tk):
    b, l, _ = q.shape
    assert tq % (2 * tk) == 0 and l % tq == 0
    nq = l // tq
    nt = l // tk
    nd = bias.shape[1]
    return pl.pallas_call(
        functools.partial(_attn_kernel, tq=tq, tk=tk, n_bias=nd),
        out_shape=jax.ShapeDtypeStruct((b, l, GROUP_W), BF16),
        grid=(b, N_HEADS, nq),
        in_specs=[pl.BlockSpec(memory_space=pltpu.SMEM),
                  pl.BlockSpec((1, tq, HEAD_W), lambda bb, h, i: (bb, i, h)),
                  pl.BlockSpec((1, l, HEAD_W), lambda bb, h, i: (bb, 0, h)),
                  pl.BlockSpec((1, 1, nt, VT_ROWS, tk), lambda bb, h, i: (bb, h, 0, 0, 0)),
                  pl.BlockSpec((1, nd, tk, tq), lambda bb, h, i: (h, 0, 0, 0)),
                  pl.BlockSpec((1, tq, HEAD_W), lambda bb, h, i: (bb, i, h)),
                  pl.BlockSpec((1, HEAD_W), lambda bb, h, i: (0, 0))],
        out_specs=pl.BlockSpec((1, tq, HEAD_W), lambda bb, h, i: (bb, i, h)),
        scratch_shapes=[pltpu.VMEM((2 * tq, HEAD_W), BF16),
                        pltpu.VMEM((tk, 2 * tq), F32),
                        pltpu.VMEM((tk, 2 * tq), F32),
                        pltpu.VMEM((1, 2 * tq), F32),
                        pltpu.VMEM((VT_ROWS, 2 * tq), F32)],
        compiler_params=_cparams(("parallel", "parallel", "arbitrary")),
        name="attn_prompt",
    )(lam, q, kb, vt, bias, gate, snw)


def _decode_kernel(pt_ref, lam_ref, q_ref, kn_ref, vn_ref, g_ref, snw_ref, bias_ref, biasn_ref, *rest,
                   n_pages, rows, page):
    k_refs = rest[:n_pages]
    v_refs = rest[n_pages:2 * n_pages]
    o_ref = rest[2 * n_pages]
    qst_scr, m_scr, l_scr, acc_scr, kpg_scr, vpg_scr = rest[2 * n_pages + 1:]
    g = pl.program_id(1)
    hr = 2 * rows

    def head_page(ref, hh):
        return ref[0, pl.ds(hh, page, stride=N_HEADS), :].astype(BF16)

    @pl.when(g == 0)
    def _():
        q = q_ref[0].astype(F32)
        lane = lax.broadcasted_iota(jnp.int32, (rows, HEAD_W), 1)
        kpg_scr[...] = jnp.zeros_like(kpg_scr)
        vpg_scr[...] = jnp.zeros_like(vpg_scr)
        for hh in range(N_HEADS):
            qh = q[:, hh * HEAD_W:(hh + 1) * HEAD_W]
            qst_scr[hh] = jnp.concatenate([jnp.where(lane < HALF, qh, 0.0), jnp.where(lane >= HALF, qh, 0.0)],
                                          axis=0).astype(BF16)
            kpg_scr[hh, 0:rows, :] = kn_ref[0, pl.ds(hh, rows, stride=N_HEADS), :]
            vpg_scr[hh, 0:rows, :] = vn_ref[0, pl.ds(hh, rows, stride=N_HEADS), :]
        m_scr[...] = jnp.full(m_scr.shape, NEG, F32)
        l_scr[...] = jnp.zeros_like(l_scr)
        acc_scr[...] = jnp.zeros_like(acc_scr)

    def update(s, n_blocks, value_fn):
        m_old = m_scr[...]
        m_new = jnp.maximum(m_old, jnp.max(s, axis=-1, keepdims=True))
        alpha = jnp.exp(m_old - m_new)
        p = jnp.exp(s - m_new)
        l_scr[...] = alpha * l_scr[...] + jnp.sum(p, axis=-1, keepdims=True)
        pb = p.astype(BF16)
        pv_rows = []
        for hh in range(N_HEADS):
            ph = pb[hh * hr:(hh + 1) * hr, :]
            pv = _dot(ph[:, 0:page], value_fn(hh, 0))
            for j in range(1, n_blocks):
                pv = pv + _dot(ph[:, j * page:(j + 1) * page], value_fn(hh, j))
            pv_rows.append(pv)
        acc_scr[...] = acc_scr[...] * alpha + jnp.concatenate(pv_rows, axis=0)
        m_scr[...] = m_new

    def scores(n_blocks, key_fn):
        s_rows = []
        for hh in range(N_HEADS):
            qh = qst_scr[hh]
            blocks = [_dot_nt(qh, key_fn(hh, j)) for j in range(n_blocks)]
            s_rows.append(blocks[0] if n_blocks == 1 else jnp.concatenate(blocks, axis=1))
        return jnp.concatenate(s_rows, axis=0)

    s = scores(n_pages, lambda hh, j: head_page(k_refs[j], hh)) + bias_ref[...]
    update(s, n_pages, lambda hh, j: head_page(v_refs[j], hh))

    @pl.when(g == pl.num_programs(1) - 1)
    def _():
        s_new = scores(1, lambda hh, j: kpg_scr[hh].astype(BF16)) + biasn_ref[...]
        update(s_new, 1, lambda hh, j: vpg_scr[hh].astype(BF16))
        lam = lam_ref[0, 0]
        inv = 1.0 / l_scr[...]
        acc = acc_scr[...] * inv
        gate = g_ref[0]
        for hh in range(N_HEADS):
            hs = slice(hh * HEAD_W, (hh + 1) * HEAD_W)
            r0 = hh * hr
            o = acc[r0:r0 + rows, :] - lam * acc[r0 + rows:r0 + 2 * rows, :]
            o_ref[0, :, hs] = _sub_norm_gate(o, snw_ref[...], gate[:, hs]).astype(o_ref.dtype)


def _attn_decode(page_table, lam, q, k_new, v_new, gate, snw, bias_dec, bias_new, cache_k, cache_v, n_pages):
    nb, rows, _ = q.shape
    pages_per_seq = page_table.shape[1]
    page = cache_k.shape[1] // N_HEADS
    assert pages_per_seq % n_pages == 0 and page == LANES
    ng = pages_per_seq // n_pages
    n_rows = 2 * N_HEADS * rows
    seq = lambda b, g, pt: (b, 0, 0)
    in_specs = [pl.BlockSpec(memory_space=pltpu.SMEM),
                pl.BlockSpec((1, rows, GROUP_W), seq),
                pl.BlockSpec((1, rows * N_HEADS, HEAD_W), seq),
                pl.BlockSpec((1, rows * N_HEADS, HEAD_W), seq),
                pl.BlockSpec((1, rows, GROUP_W), seq),
                pl.BlockSpec((1, HEAD_W), lambda b, g, pt: (0, 0)),
                pl.BlockSpec((n_rows, n_pages * page), lambda b, g, pt: (0, g)),
                pl.BlockSpec((n_rows, LANES), lambda b, g, pt: (0, 0))]
    for cache in (cache_k, cache_v):
        for j in range(n_pages):
            in_specs.append(pl.BlockSpec((1, page * N_HEADS, HEAD_W),
                                         lambda b, g, pt, j=j: (pt[b, g * n_pages + j], 0, 0)))
    grid_spec = pltpu.PrefetchScalarGridSpec(
        num_scalar_prefetch=1, grid=(nb, ng), in_specs=in_specs,
        out_specs=pl.BlockSpec((1, rows, GROUP_W), seq),
        scratch_shapes=[pltpu.VMEM((N_HEADS, 2 * rows, HEAD_W), BF16),
                        pltpu.VMEM((n_rows, 1), F32),
                        pltpu.VMEM((n_rows, 1), F32),
                        pltpu.VMEM((n_rows, HEAD_W), F32),
                        pltpu.VMEM((N_HEADS, page, HEAD_W), F32),
                        pltpu.VMEM((N_HEADS, page, HEAD_W), F32)])
    return pl.pallas_call(
        functools.partial(_decode_kernel, n_pages=n_pages, rows=rows, page=page),
        out_shape=jax.ShapeDtypeStruct((nb, rows, GROUP_W), F32),
        grid_spec=grid_spec,
        compiler_params=_cparams(("parallel", "arbitrary")),
        name="attn_decode",
    )(page_table, lam, q, k_new, v_new, gate, snw, bias_dec, bias_new,
      *([cache_k] * n_pages), *([cache_v] * n_pages))


def _mix_out_kernel(ya_ref, yb_ref, x_ref, mod_ref, w_ref, o_ref, *, nseq, rows):
    d = x_ref.shape[-1]
    tm = nseq * rows
    ya = ya_ref[...].reshape(tm, GROUP_W).astype(BF16)
    yb = yb_ref[...].reshape(tm, GROUP_W).astype(BF16)
    out = _dot(ya, w_ref[0:GROUP_W, :]) + _dot(yb, w_ref[GROUP_W:2 * GROUP_W, :])
    gate = mod_ref[:, :, 2 * d:3 * d]
    o_ref[...] = x_ref[...] + gate * out.reshape(nseq, rows, d)


def _mix_out(ya, yb, x, mod, w_out, *, nseq, rows):
    n_s, r_total, d = x.shape
    grid = (n_s // nseq, r_total // rows)
    tok = lambda s, t: (s, t, 0)
    return pl.pallas_call(
        functools.partial(_mix_out_kernel, nseq=nseq, rows=rows),
        out_shape=jax.ShapeDtypeStruct(x.shape, F32),
        grid=grid,
        in_specs=[pl.BlockSpec((nseq, rows, GROUP_W), tok),
                  pl.BlockSpec((nseq, rows, GROUP_W), tok),
                  pl.BlockSpec((nseq, rows, d), tok),
                  pl.BlockSpec((nseq, 1, 3 * d), lambda s, t: (s, 0, 0)),
                  pl.BlockSpec((2 * GROUP_W, d), lambda s, t: (0, 0))],
        out_specs=pl.BlockSpec((nseq, rows, d), tok),
        compiler_params=_cparams(("parallel", "arbitrary")),
        name="mix_out",
    )(ya, yb, x, mod, w_out)


def _largest_tile(n, cap):
    t = cap
    while n % t:
        t //= 2
    return t


def kernel(x_prompt, x_sample, c_prompt, c_sample, cache_k, cache_v, state_hgrn, page_table, norm_w, w_ada,
           b_ada, w_in, w_out, lb_param, hg_norm_w, q_norm_w, k_norm_w, lam_q1, lam_k1, lam_q2, lam_k2,
           sub_norm_w, rel_table):
    depth = w_in.shape[0]
    assert depth == 1
    b, l, d = x_prompt.shape
    db, dl, _ = x_sample.shape
    assert d == 2 * GROUP_W and w_in.shape[2] == 9 * GROUP_W
    page = cache_k.shape[2]
    past = page_table.shape[1] * page
    assert dl % SUBLANES == 0 and cache_k.shape[3] * cache_k.shape[4] == GROUP_W

    w_in_b = w_in[0].astype(BF16)
    w_out_b = w_out[0].astype(BF16)
    nw = norm_w.reshape(1, d)
    hgw = jnp.tile(hg_norm_w.reshape(1, HEAD_W), (1, N_HEADS))
    qnw = jnp.tile(q_norm_w.reshape(1, HALF), (1, GROUP_W // HALF))
    knw = jnp.tile(k_norm_w.reshape(1, HALF), (1, GROUP_W // HALF))
    snw = sub_norm_w.reshape(1, HEAD_W)

    mod = _adaln_mod(jnp.concatenate([c_prompt, c_sample], axis=0), w_ada[0], b_ada[0])
    mod_p = mod[:b].reshape(b, 1, 3 * d)
    mod_s = mod[b:].reshape(db, 1, 3 * d)
    lb, lam = _small_params(lb_param, lam_q1, lam_k1, lam_q2, lam_k2)

    log2e = math.log2(math.e)
    ya, q, kf, vf, gb, kb, vt, st_p = _mix_in(
        x_prompt, mod_p, nw, w_in_b, lb, hgw, qnw, knw, nseq=1, rows=ATTN_TK,
        chunk=CHUNK if l % CHUNK == 0 else l, sub=SUB, s0=None, attn_tile=ATTN_TK, q_dtype=BF16,
        q_scale=HALF ** -0.5 * log2e)
    bias_p = _bias_prompt(rel_table, ATTN_TK, ATTN_TQ, log2e)
    yb = _attn_prompt(lam, q, kb, vt, bias_p, gb, snw, ATTN_TQ, ATTN_TK)
    y_prompt = _mix_out(ya, yb, x_prompt, mod_p, w_out_b, nseq=1, rows=_largest_tile(l, 512))

    nseq = _largest_tile(db, 8)
    ya_s, q_s, kf_s, vf_s, gb_s, st_s = _mix_in(
        x_sample, mod_s, nw, w_in_b, lb, hgw, qnw, knw, nseq=nseq, rows=dl,
        chunk=dl, sub=dl, s0=state_hgrn[0], attn_tile=None, q_dtype=F32, q_scale=HALF ** -0.5)
    bias_d, bias_n = _bias_decode(rel_table, dl, past)
    n_rows = 2 * N_HEADS * dl
    yb_s = _attn_decode(page_table, lam, q_s, kf_s, vf_s, gb_s, snw,
                        bias_d.reshape(n_rows, past), bias_n.reshape(n_rows, LANES),
                        cache_k.reshape(-1, page * N_HEADS, HEAD_W), cache_v.reshape(-1, page * N_HEADS, HEAD_W),
                        n_pages=_largest_tile(page_table.shape[1], 8))
    y_sample = _mix_out(ya_s, yb_s, x_sample, mod_s, w_out_b, nseq=nseq, rows=dl)

    hb = N_HEADS
    return (y_prompt, y_sample,
            kf.reshape(1, b, l, hb, HEAD_W), vf.reshape(1, b, l, hb, HEAD_W), st_p[None],
            kf_s.reshape(1, db, dl, hb, HEAD_W), vf_s.reshape(1, db, dl, hb, HEAD_W), st_s[None])
```

```python
import functools
import math

import jax
import jax.numpy as jnp
from jax import lax
from jax.experimental import pallas as pl
from jax.experimental.pallas import tpu as pltpu

F32 = jnp.float32
BF16 = jnp.bfloat16

N_HEADS = 4
HEAD_W = 128
HALF = 64
GROUP_W = N_HEADS * HEAD_W
N_BUCKETS = 32
MAX_EXACT = N_BUCKETS // 2
MAX_DIST = 1024
CHUNK = 64
SUB = 16
EPS = 1e-6
LAMBDA_INIT = 0.8 - 0.6 * math.exp(-0.3 * 0)
NEG = -1e30
ATTN_TQ = 512
ATTN_TK = 256
VT_ROWS = HEAD_W + 16
DECODE_GROUP_PAGES = 8

LANES = 128
SUBLANES = 8
VMEM_LIMIT_BYTES = 56 * 1024 * 1024


def _bucket_lower_bounds():
    los = list(range(MAX_EXACT))
    span = N_BUCKETS - MAX_EXACT
    ratio = MAX_DIST // MAX_EXACT
    for m in range(MAX_EXACT, N_BUCKETS):
        d = MAX_EXACT
        while d ** span < (MAX_EXACT ** span) * (ratio ** (m - MAX_EXACT)):
            d += 1
        los.append(d)
    return los


BUCKET_LO = _bucket_lower_bounds()
SATURATED_DIST = BUCKET_LO[-1]


def _cparams(sem):
    return pltpu.CompilerParams(dimension_semantics=sem, vmem_limit_bytes=VMEM_LIMIT_BYTES)


def _split2(x):
    hi = x.astype(BF16)
    lo = (x - hi.astype(F32)).astype(BF16)
    return hi, lo


def _split3(x):
    hi = x.astype(BF16)
    r = x - hi.astype(F32)
    mid = r.astype(BF16)
    lo = (r - mid.astype(F32)).astype(BF16)
    return hi, mid, lo


def _dot(a, b):
    return jnp.dot(a, b, preferred_element_type=F32)


def _dot_nt(a, b):
    return lax.dot_general(a, b, (((1,), (1,)), ((), ())), preferred_element_type=F32)


def _dot_tn(a, b):
    return lax.dot_general(a, b, (((0,), (0,)), ((), ())), preferred_element_type=F32)


def _div_pow2(x, n):
    assert n > 0 and n & (n - 1) == 0
    return lax.shift_right_logical(x, n.bit_length() - 1)


def _sigmoid_pair(x):
    t = jnp.exp(-jnp.abs(x))
    r = 1.0 / (1.0 + t)
    tr = t * r
    pos = x >= 0
    return jnp.where(pos, r, tr), jnp.where(pos, tr, r)


def _mod_kernel(c_ref, w_ref, b_ref, o_ref):
    c = c_ref[...]
    sig, _ = _sigmoid_pair(c)
    s = c * sig
    w = w_ref[...]
    s1, s2 = _split2(s)
    w1, w2 = _split2(w)
    o_ref[...] = _dot(s1, w1) + (_dot(s1, w2) + _dot(s2, w1)) + b_ref[...]


def _adaln_mod(c_all, w_ada, b_ada):
    n, d = c_all.shape
    d3 = w_ada.shape[1]
    bn = d
    return pl.pallas_call(
        _mod_kernel,
        out_shape=jax.ShapeDtypeStruct((n, d3), F32),
        grid=(d3 // bn,),
        in_specs=[pl.BlockSpec((n, d), lambda j: (0, 0)),
                  pl.BlockSpec((d, bn), lambda j: (0, j)),
                  pl.BlockSpec((1, bn), lambda j: (0, j))],
        out_specs=pl.BlockSpec((n, bn), lambda j: (0, j)),
        compiler_params=_cparams(("arbitrary",)),
        name="adaln_mod",
    )(c_all, w_ada, b_ada.reshape(1, d3))


def _small_kernel(lbp_ref, q1_ref, k1_ref, q2_ref, k2_ref, lb_ref, lam_ref):
    p = lbp_ref[...]
    m = jnp.max(p, axis=0, keepdims=True)
    e = jnp.exp(p - m)
    lb_ref[...] = e[0:1, :] / jnp.sum(e, axis=0, keepdims=True)
    s1 = jnp.sum(q1_ref[...] * k1_ref[...], axis=-1, keepdims=True)
    s2 = jnp.sum(q2_ref[...] * k2_ref[...], axis=-1, keepdims=True)
    lam_ref[...] = jnp.exp(s1) - jnp.exp(s2) + LAMBDA_INIT


def _small_params(lb_param, lq1, lk1, lq2, lk2):
    return pl.pallas_call(
        _small_kernel,
        out_shape=(jax.ShapeDtypeStruct((1, lb_param.shape[1]), F32),
                   jax.ShapeDtypeStruct((1, 1), F32)),
        name="small_params",
    )(lb_param, lq1, lk1, lq2, lk2)


def _bias_select(d, tab_ref, h):
    val = jnp.full(d.shape, tab_ref[N_BUCKETS - 1, h], F32)
    for m in range(N_BUCKETS - 2, -1, -1):
        val = jnp.where(d < BUCKET_LO[m + 1], tab_ref[m, h], val)
    return val


def _bias_prompt_kernel(tab_ref, o_ref, *, tk, tq, scale):
    h = pl.program_id(0)
    dl = pl.program_id(1) - (tq // tk - 1)
    kk = lax.broadcasted_iota(jnp.int32, (tk, tq), 0)
    qq = lax.broadcasted_iota(jnp.int32, (tk, tq), 1)
    d = dl * tk + qq - kk
    val = _bias_select(jnp.maximum(d, 0), tab_ref, h) * scale
    o_ref[0, 0] = jnp.where(d >= 0, val, NEG)


def _n_bias_tiles(tk, tq):
    n_const = 0
    while n_const * tk - (tk - 1) < SATURATED_DIST:
        n_const += 1
    return n_const + 1 + (tq // tk - 1)


def _bias_prompt(rel_table, tk, tq, scale):
    nd = _n_bias_tiles(tk, tq)
    return pl.pallas_call(
        functools.partial(_bias_prompt_kernel, tk=tk, tq=tq, scale=scale),
        out_shape=jax.ShapeDtypeStruct((N_HEADS, nd, tk, tq), F32),
        grid=(N_HEADS, nd),
        in_specs=[pl.BlockSpec(memory_space=pltpu.SMEM)],
        out_specs=pl.BlockSpec((1, 1, tk, tq), lambda h, d: (h, d, 0, 0)),
        compiler_params=_cparams(("arbitrary", "arbitrary")),
        name="bias_prompt",
    )(rel_table)


def _bias_decode_kernel(tab_ref, o_ref, on_ref, *, rows, past, width):
    h = pl.program_id(0)
    g = pl.program_id(1)
    i = lax.broadcasted_iota(jnp.int32, (rows, width), 0)
    kpos = g * width + lax.broadcasted_iota(jnp.int32, (rows, width), 1)
    val = _bias_select(past + i - kpos, tab_ref, h)
    o_ref[0, 0, 0] = val
    o_ref[0, 0, 1] = val
    i2 = lax.broadcasted_iota(jnp.int32, (rows, LANES), 0)
    j2 = lax.broadcasted_iota(jnp.int32, (rows, LANES), 1)
    d2 = i2 - j2
    vn = jnp.where(d2 >= 0, _bias_select(jnp.maximum(d2, 0), tab_ref, h), NEG)
    on_ref[0, 0] = vn
    on_ref[0, 1] = vn


def _bias_decode(rel_table, rows, past, width):
    ng = past // width
    return pl.pallas_call(
        functools.partial(_bias_decode_kernel, rows=rows, past=past, width=width),
        out_shape=(jax.ShapeDtypeStruct((ng, N_HEADS, 2, rows, width), F32),
                   jax.ShapeDtypeStruct((N_HEADS, 2, rows, LANES), F32)),
        grid=(N_HEADS, ng),
        in_specs=[pl.BlockSpec(memory_space=pltpu.SMEM)],
        out_specs=(pl.BlockSpec((1, 1, 2, rows, width), lambda h, g: (g, h, 0, 0, 0)),
                   pl.BlockSpec((1, 2, rows, LANES), lambda h, g: (h, 0, 0, 0))),
        compiler_params=_cparams(("arbitrary", "arbitrary")),
        name="bias_decode",
    )(rel_table)


def _group_mean_sq(x, gmat):
    sq = x * x
    outs = []
    w = gmat.shape[0]
    for c0 in range(0, x.shape[1], w):
        hi, lo = _split2(sq[:, c0:c0 + w])
        outs.append(_dot(hi, gmat) + _dot(lo, gmat))
    return jnp.concatenate(outs, axis=1)


def _hgrn_intra_mask(chunk, sub):
    n_sub = chunk // sub
    rr = lax.broadcasted_iota(jnp.int32, (chunk, n_sub * chunk), 0)
    cc = lax.broadcasted_iota(jnp.int32, (chunk, n_sub * chunk), 1)
    own_block = _div_pow2(cc, chunk) == _div_pow2(rr, sub)
    causal = jnp.bitwise_and(cc, chunk - 1) <= rr
    return jnp.where(own_block, jnp.where(causal, 1.0, 0.0), 0.0)


def _hgrn_chunk(bc, qc, kc, vc, st, mask, chunk, sub):
    n_sub = chunk // sub
    o = _dot_nt((qc * jnp.exp(bc)).astype(BF16), st.astype(BF16))
    row = lax.broadcasted_iota(jnp.int32, (chunk, HEAD_W), 0)
    q_parts, k_parts = [], []
    for i in range(n_sub):
        lo, hi = i * sub, (i + 1) * sub
        mid = lo + sub // 2 - 1
        anchor = bc[mid:mid + 1, :]
        q_parts.append(qc[lo:hi, :] * jnp.exp(bc[lo:hi, :] - anchor))
        e = anchor - bc
        if i < n_sub - 1:
            e = jnp.where(row < hi, e, 0.0)
        k_parts.append((kc * jnp.exp(e)).astype(BF16))
    qa = q_parts[0] if n_sub == 1 else jnp.concatenate(q_parts, axis=0)
    ka = k_parts[0] if n_sub == 1 else jnp.concatenate(k_parts, axis=0)
    att = jnp.where(mask > 0.5, _dot_nt(qa.astype(BF16), ka), 0.0)
    vb = vc.astype(BF16)
    vv = vb if n_sub == 1 else jnp.concatenate([vb] * n_sub, axis=0)
    o = o + _dot(att.astype(BF16), vv)
    b_last = bc[chunk - 1:chunk, :]
    ke = kc * jnp.exp(b_last - bc)
    st_new = st * jnp.exp(b_last) + _dot_tn(vc.astype(BF16), ke.astype(BF16))
    return o, st_new


def _mix_in_kernel(*refs, nseq, rows, chunk, sub, per_seq_state, emit_attn_kv, q_scale):
    it = iter(refs)
    x_ref = next(it); mod_ref = next(it); nw_ref = next(it); win_ref = next(it); lb_ref = next(it)
    hgw_ref = next(it); qnw_ref = next(it); knw_ref = next(it)
    s0_ref = next(it) if per_seq_state else None
    ya_ref = next(it); q_ref = next(it); kf_ref = next(it); vf_ref = next(it); gb_ref = next(it)
    kb_ref = next(it) if emit_attn_kv else None
    vt_ref = next(it) if emit_attn_kv else None
    sfin_ref = next(it)
    st_scr = next(it); b_scr = next(it); q_scr = next(it); k_scr = next(it); v_scr = next(it); o_scr = next(it)

    tm = nseq * rows
    d = x_ref.shape[-1]
    n_chunks = tm // chunk

    x3 = x_ref[...]
    ms = jnp.mean(x3 * x3, axis=-1, keepdims=True)
    shift = mod_ref[:, :, 0:d]
    scale = mod_ref[:, :, d:2 * d]
    h3 = x3 * lax.rsqrt(ms + EPS) * nw_ref[...] * (1.0 + scale) + shift
    h = h3.reshape(tm, d).astype(BF16)

    def proj(idx):
        return _dot(h, win_ref[:, idx * GROUP_W:(idx + 1) * GROUP_W])

    def group_b():
        gw = 2 * LANES
        gi = _div_pow2(lax.broadcasted_iota(jnp.int32, (gw, gw), 0), HALF)
        gj = _div_pow2(lax.broadcasted_iota(jnp.int32, (gw, gw), 1), HALF)
        gmat = jnp.where(gi == gj, 1.0 / HALF, 0.0).astype(BF16)

        qb = proj(5)
        qn = qb * lax.rsqrt(_group_mean_sq(qb, gmat) + EPS) * qnw_ref[...] * q_scale
        q_ref[...] = qn.reshape(nseq, rows, GROUP_W).astype(q_ref.dtype)
        kb = proj(6)
        kn = kb * lax.rsqrt(_group_mean_sq(kb, gmat) + EPS) * knw_ref[...]
        vb = proj(7)
        for hh in range(N_HEADS):
            hs = slice(hh * HEAD_W, (hh + 1) * HEAD_W)
            kf_ref[:, pl.ds(hh, rows, stride=N_HEADS), :] = kn[:, hs].reshape(nseq, rows, HEAD_W)
            vf_ref[:, pl.ds(hh, rows, stride=N_HEADS), :] = vb[:, hs].reshape(nseq, rows, HEAD_W)
        if emit_attn_kv:
            kb_ref[...] = kn.reshape(nseq, rows, GROUP_W).astype(BF16)
            for hh in range(N_HEADS):
                vt_ref[0, hh, 0, 0:HEAD_W, :] = vb[:, hh * HEAD_W:(hh + 1) * HEAD_W].T.astype(BF16)
                vt_ref[0, hh, 0, HEAD_W:VT_ROWS, :] = jnp.ones((VT_ROWS - HEAD_W, tm), BF16)
        zb = proj(8)
        sgz, _ = _sigmoid_pair(zb)
        gb_ref[...] = (zb * sgz).reshape(nseq, rows, GROUP_W)

    lb = lb_ref[...]
    fa = proj(1)
    sig, nsig = _sigmoid_pair(fa)
    logf = jnp.log(lb + (1.0 - lb) * sig)
    k_scr[...] = (1.0 - lb) * nsig
    q_scr[...] = proj(0) * (HEAD_W ** -0.5)
    v_scr[...] = proj(2)
    ri = lax.broadcasted_iota(jnp.int32, (tm, tm), 0)
    ci = lax.broadcasted_iota(jnp.int32, (tm, tm), 1)
    same_chunk = _div_pow2(ri, chunk) == _div_pow2(ci, chunk)
    tri = jnp.where(ci <= ri, jnp.where(same_chunk, 1.0, 0.0), 0.0).astype(BF16)
    l1, l2, l3 = _split3(logf)
    b_scr[...] = _dot(tri, l1) + _dot(tri, l2) + _dot(tri, l3)

    if not per_seq_state:
        @pl.when(pl.program_id(1) == 0)
        def _():
            st_scr[...] = jnp.zeros_like(st_scr)

    intra_mask = _hgrn_intra_mask(chunk, sub)
    for hh in range(N_HEADS):
        hs = slice(hh * HEAD_W, (hh + 1) * HEAD_W)
        st = None if per_seq_state else st_scr[hh]
        for c in range(n_chunks):
            rs = slice(c * chunk, (c + 1) * chunk)
            if per_seq_state:
                st = s0_ref[c, hh].T
            o, st = _hgrn_chunk(b_scr[rs, hs], q_scr[rs, hs], k_scr[rs, hs], v_scr[rs, hs], st, intra_mask,
                                chunk, sub)
            o_scr[rs, hs] = o
            if per_seq_state:
                sfin_ref[c, hh] = st.T
        if not per_seq_state:
            st_scr[hh] = st

    group_b()

    if not per_seq_state:
        @pl.when(pl.program_id(1) == pl.num_programs(1) - 1)
        def _():
            for hh in range(N_HEADS):
                sfin_ref[0, hh] = st_scr[hh].T

    sg, _ = _sigmoid_pair(proj(3))
    za = proj(4)
    sz, _ = _sigmoid_pair(za)
    outs = []
    for hh in range(N_HEADS):
        hs = slice(hh * HEAD_W, (hh + 1) * HEAD_W)
        oh = o_scr[:, hs] * sg[:, hs]
        mo = jnp.mean(oh * oh, axis=-1, keepdims=True)
        outs.append(oh * lax.rsqrt(mo + EPS))
    ya = jnp.concatenate(outs, axis=1) * hgw_ref[...] * (za * sz)
    ya_ref[...] = ya.reshape(nseq, rows, GROUP_W).astype(ya_ref.dtype)


def _mix_in(x, mod, norm_w, w_in, lb, hgw, qnw, knw, *, nseq, rows, chunk, sub, s0, attn_tile, q_dtype, q_scale):
    n_s, r_total, d = x.shape
    per_seq_state = s0 is not None
    emit = attn_tile is not None
    tm = nseq * rows
    if per_seq_state:
        assert rows == r_total and rows == chunk and n_s % nseq == 0
        grid = (n_s // nseq,)
        tok = lambda i: (i, 0, 0)
        seq = lambda i: (i, 0, 0)
        const2 = lambda i: (0, 0)
        st_map = lambda i: (i, 0, 0, 0)
        sem = ("arbitrary",)
    else:
        assert nseq == 1 and r_total % rows == 0 and rows % chunk == 0
        grid = (n_s, r_total // rows)
        tok = lambda b, t: (b, t, 0)
        seq = lambda b, t: (b, 0, 0)
        const2 = lambda b, t: (0, 0)
        st_map = lambda b, t: (b, 0, 0, 0)
        sem = ("parallel", "arbitrary")
    if emit:
        assert rows == attn_tile
    d_in = w_in.shape[1]
    tok_spec = pl.BlockSpec((nseq, rows, GROUP_W), tok)
    tokhead_spec = pl.BlockSpec((nseq, rows * N_HEADS, HEAD_W), tok)
    in_specs = [pl.BlockSpec((nseq, rows, d), tok),
                pl.BlockSpec((nseq, 1, 3 * d), seq),
                pl.BlockSpec((1, d), const2),
                pl.BlockSpec((d, d_in), const2),
                pl.BlockSpec((1, GROUP_W), const2),
                pl.BlockSpec((1, GROUP_W), const2),
                pl.BlockSpec((1, GROUP_W), const2),
                pl.BlockSpec((1, GROUP_W), const2)]
    args = [x, mod, norm_w, w_in, lb, hgw, qnw, knw]
    if per_seq_state:
        in_specs.append(pl.BlockSpec((nseq, N_HEADS, HEAD_W, HEAD_W), st_map))
        args.append(s0)
    out_shape = [jax.ShapeDtypeStruct((n_s, r_total, GROUP_W), q_dtype),
                 jax.ShapeDtypeStruct((n_s, r_total, GROUP_W), q_dtype),
                 jax.ShapeDtypeStruct((n_s, r_total * N_HEADS, HEAD_W), F32),
                 jax.ShapeDtypeStruct((n_s, r_total * N_HEADS, HEAD_W), F32),
                 jax.ShapeDtypeStruct((n_s, r_total, GROUP_W), F32)]
    out_specs = [tok_spec, tok_spec, tokhead_spec, tokhead_spec, tok_spec]
    if emit:
        nt = r_total // attn_tile
        out_shape += [jax.ShapeDtypeStruct((n_s, r_total, GROUP_W), BF16),
                      jax.ShapeDtypeStruct((n_s, N_HEADS, nt, VT_ROWS, attn_tile), BF16)]
        out_specs += [tok_spec,
                      pl.BlockSpec((1, N_HEADS, 1, VT_ROWS, attn_tile), lambda b, t: (b, 0, t, 0, 0))]
    out_shape.append(jax.ShapeDtypeStruct((n_s, N_HEADS, HEAD_W, HEAD_W), F32))
    out_specs.append(pl.BlockSpec((nseq, N_HEADS, HEAD_W, HEAD_W), st_map))
    scratch = [pltpu.VMEM((N_HEADS, HEAD_W, HEAD_W), F32)] + [pltpu.VMEM((tm, GROUP_W), F32)] * 5
    kern = functools.partial(_mix_in_kernel, nseq=nseq, rows=rows, chunk=chunk, sub=sub,
                             per_seq_state=per_seq_state, emit_attn_kv=emit, q_scale=q_scale)
    return pl.pallas_call(
        kern, out_shape=tuple(out_shape), grid=grid, in_specs=in_specs, out_specs=tuple(out_specs),
        scratch_shapes=scratch, compiler_params=_cparams(sem),
        name="mix_in_sample" if per_seq_state else "mix_in_prompt",
    )(*args)


def _sub_norm_gate(o, snw, gate):
    ms = jnp.mean(o * o, axis=-1, keepdims=True)
    return o * lax.rsqrt(ms + EPS) * snw * (1.0 - LAMBDA_INIT) * gate


def _attn_kernel(lam_ref, q_ref, k_ref, vt_ref, bias_ref, g_ref, snw_ref, o_ref,
                 qs_scr, sa_scr, sb_scr, m_scr, acc_scr, *, tq, tk, n_bias):
    qi = pl.program_id(2)
    ratio = tq // tk
    n_tiles = ratio * (qi + 1)
    q = q_ref[0].astype(F32)
    lane = lax.broadcasted_iota(jnp.int32, q.shape, 1)
    qs_scr[0:tq, :] = jnp.where(lane < HALF, q, 0.0).astype(BF16)
    qs_scr[tq:2 * tq, :] = jnp.where(lane >= HALF, q, 0.0).astype(BF16)
    m_scr[...] = jnp.full(m_scr.shape, NEG, F32)
    acc_scr[...] = jnp.zeros_like(acc_scr)

    def scores(t, s_ref):
        r0 = pl.multiple_of(t * tk, tk)
        kt = k_ref[0, pl.ds(r0, tk), :]
        bt = bias_ref[0, jnp.clip(ratio * qi - t + (ratio - 1), 0, n_bias - 1)]
        for c in range(2):
            s_ref[:, c * tq:(c + 1) * tq] = _dot_nt(kt, qs_scr[c * tq:(c + 1) * tq, :]) + bt

    def softmax_pv(t, s_ref):
        vt = vt_ref[0, 0, t]
        for c in range(2):
            cols = slice(c * tq, (c + 1) * tq)
            m_old = m_scr[:, cols]
            m_new = jnp.maximum(m_old, jnp.max(s_ref[:, cols], axis=0, keepdims=True))
            alpha = jnp.exp2(m_old - m_new)
            p = jnp.exp2(s_ref[:, cols] - m_new)
            acc_scr[:, cols] = acc_scr[:, cols] * alpha + _dot(vt, p.astype(BF16))
            m_scr[:, cols] = m_new

    scores(0, sa_scr)

    def body(j, carry):
        t0 = 2 * j
        scores(t0 + 1, sb_scr)
        softmax_pv(t0, sa_scr)
        scores(jnp.minimum(t0 + 2, n_tiles - 1), sa_scr)
        softmax_pv(t0 + 1, sb_scr)
        return carry

    lax.fori_loop(0, n_tiles // 2, body, 0)

    lam = lam_ref[0, 0]
    inv = 1.0 / acc_scr[HEAD_W:HEAD_W + 1, :]
    acc = acc_scr[0:HEAD_W, :]
    o_t = acc[:, 0:tq] * inv[:, 0:tq] - lam * (acc[:, tq:2 * tq] * inv[:, tq:2 * tq])
    o_ref[0] = _sub_norm_gate(o_t.T, snw_ref[...], g_ref[0]).astype(o_ref.dtype)


def _attn_prompt(lam, q, kb, vt, bias, gate, snw, tq, tk):
    b, l, _ = q.shape
    assert tq % (2 * tk) == 0 and l % tq == 0
    nq = l // tq
    nt = l // tk
    nd = bias.shape[1]
    return pl.pallas_call(
        functools.partial(_attn_kernel, tq=tq, tk=tk, n_bias=nd),
        out_shape=jax.ShapeDtypeStruct((b, l, GROUP_W), BF16),
        grid=(b, N_HEADS, nq),
        in_specs=[pl.BlockSpec(memory_space=pltpu.SMEM),
                  pl.BlockSpec((1, tq, HEAD_W), lambda bb, h, i: (bb, i, h)),
                  pl.BlockSpec((1, l, HEAD_W), lambda bb, h, i: (bb, 0, h)),
                  pl.BlockSpec((1, 1, nt, VT_ROWS, tk), lambda bb, h, i: (bb, h, 0, 0, 0)),
                  pl.BlockSpec((1, nd, tk, tq), lambda bb, h, i: (h, 0, 0, 0)),
                  pl.BlockSpec((1, tq, HEAD_W), lambda bb, h, i: (bb, i, h)),
                  pl.BlockSpec((1, HEAD_W), lambda bb, h, i: (0, 0))],
        out_specs=pl.BlockSpec((1, tq, HEAD_W), lambda bb, h, i: (bb, i, h)),
        scratch_shapes=[pltpu.VMEM((2 * tq, HEAD_W), BF16),
                        pltpu.VMEM((tk, 2 * tq), F32),
                        pltpu.VMEM((tk, 2 * tq), F32),
                        pltpu.VMEM((1, 2 * tq), F32),
                        pltpu.VMEM((VT_ROWS, 2 * tq), F32)],
        compiler_params=_cparams(("parallel", "parallel", "arbitrary")),
        name="attn_prompt",
    )(lam, q, kb, vt, bias, gate, snw)


def _decode_kernel(pt_ref, lam_ref, q_ref, kn_ref, vn_ref, g_ref, snw_ref, bias_ref, biasn_ref, ck_hbm, cv_hbm,
                   o_ref, kbuf, vbuf, sem, qst_scr, m_scr, l_scr, acc_scr, kpg_scr, vpg_scr,
                   *, grp, rows, page, n_seq, groups_per_seq):
    hr = 2 * rows
    total_groups = n_seq * groups_per_seq

    def group_copies(gidx, slot0):
        b = _div_pow2(gidx, groups_per_seq)
        p0 = jnp.bitwise_and(gidx, groups_per_seq - 1) * grp
        copies = []
        for j in range(grp):
            pid = pt_ref[b, p0 + j]
            copies.append(pltpu.make_async_copy(ck_hbm.at[pid], kbuf.at[slot0 + j], sem.at[0, slot0 + j]))
            copies.append(pltpu.make_async_copy(cv_hbm.at[pid], vbuf.at[slot0 + j], sem.at[1, slot0 + j]))
        return copies

    def start_group(gidx, slot0):
        for cp in group_copies(gidx, slot0):
            cp.start()

    def wait_group(gidx, slot0):
        for cp in group_copies(gidx, slot0):
            cp.wait()

    def head_page(buf, slot, hh):
        return buf[slot, pl.ds(hh, page, stride=N_HEADS), :].astype(BF16)

    def init_sequence(b):
        q = q_ref[b].astype(F32)
        lane = lax.broadcasted_iota(jnp.int32, (rows, HEAD_W), 1)
        kpg_scr[...] = jnp.zeros_like(kpg_scr)
        vpg_scr[...] = jnp.zeros_like(vpg_scr)
        for hh in range(N_HEADS):
            qh = q[:, hh * HEAD_W:(hh + 1) * HEAD_W]
            qst_scr[hh] = jnp.concatenate([jnp.where(lane < HALF, qh, 0.0), jnp.where(lane >= HALF, qh, 0.0)],
                                          axis=0).astype(BF16)
            kpg_scr[hh, 0:rows, :] = kn_ref[b, pl.ds(hh, rows, stride=N_HEADS), :]
            vpg_scr[hh, 0:rows, :] = vn_ref[b, pl.ds(hh, rows, stride=N_HEADS), :]
        m_scr[...] = jnp.full(m_scr.shape, NEG, F32)
        l_scr[...] = jnp.zeros_like(l_scr)
        acc_scr[...] = jnp.zeros_like(acc_scr)

    def update(s, n_blocks, value_fn):
        m_old = m_scr[...]
        m_new = jnp.maximum(m_old, jnp.max(s, axis=-1, keepdims=True))
        alpha = jnp.exp(m_old - m_new)
        p = jnp.exp(s - m_new)
        l_scr[...] = alpha * l_scr[...] + jnp.sum(p, axis=-1, keepdims=True)
        pb = p.astype(BF16)
        pv_rows = []
        for hh in range(N_HEADS):
            ph = pb[hh * hr:(hh + 1) * hr, :]
            pv = _dot(ph[:, 0:page], value_fn(hh, 0))
            for j in range(1, n_blocks):
                pv = pv + _dot(ph[:, j * page:(j + 1) * page], value_fn(hh, j))
            pv_rows.append(pv)
        acc_scr[...] = acc_scr[...] * alpha + jnp.concatenate(pv_rows, axis=0)
        m_scr[...] = m_new

    def scores(n_blocks, key_fn):
        s_rows = []
        for hh in range(N_HEADS):
            qh = qst_scr[hh]
            blocks = [_dot_nt(qh, key_fn(hh, j)) for j in range(n_blocks)]
            s_rows.append(blocks[0] if n_blocks == 1 else jnp.concatenate(blocks, axis=1))
        return jnp.concatenate(s_rows, axis=0)

    def consume_group(gs, slot0):
        s = scores(grp, lambda hh, j: head_page(kbuf, slot0 + j, hh)) + bias_ref[gs]
        update(s, grp, lambda hh, j: head_page(vbuf, slot0 + j, hh))

    def finish_sequence(b):
        s_new = scores(1, lambda hh, j: kpg_scr[hh].astype(BF16)) + biasn_ref[...]
        update(s_new, 1, lambda hh, j: vpg_scr[hh].astype(BF16))
        lam = lam_ref[0, 0]
        inv = 1.0 / l_scr[...]
        acc = acc_scr[...] * inv
        gate = g_ref[b]
        for hh in range(N_HEADS):
            hs = slice(hh * HEAD_W, (hh + 1) * HEAD_W)
            r0 = hh * hr
            o = acc[r0:r0 + rows, :] - lam * acc[r0 + rows:r0 + 2 * rows, :]
            o_ref[b, :, hs] = _sub_norm_gate(o, snw_ref[...], gate[:, hs]).astype(o_ref.dtype)

    start_group(0, 0)
    start_group(1, grp)

    def seq_body(b, carry):
        init_sequence(b)

        def pair_body(it, c):
            g0 = b * groups_per_seq + 2 * it
            for half in range(2):
                gidx = g0 + half
                slot0 = half * grp
                wait_group(gidx, slot0)
                consume_group(2 * it + half, slot0)

                @pl.when(gidx + 2 < total_groups)
                def _():
                    start_group(gidx + 2, slot0)
            return c

        lax.fori_loop(0, groups_per_seq // 2, pair_body, 0)
        finish_sequence(b)
        return carry

    lax.fori_loop(0, n_seq, seq_body, 0)


def _attn_decode(page_table, lam, q, k_new, v_new, gate, snw, bias_dec, bias_new, cache_k, cache_v, n_pages):
    nb, rows, _ = q.shape
    pages_per_seq = page_table.shape[1]
    page = cache_k.shape[1] // N_HEADS
    grp = n_pages
    groups_per_seq = pages_per_seq // grp
    assert pages_per_seq % grp == 0 and page == LANES
    assert groups_per_seq % 2 == 0 and groups_per_seq & (groups_per_seq - 1) == 0
    assert bias_dec.shape == (groups_per_seq, 2 * N_HEADS * rows, grp * page)
    n_rows = 2 * N_HEADS * rows
    smem = pl.BlockSpec(memory_space=pltpu.SMEM)
    vmem = pl.BlockSpec(memory_space=pltpu.VMEM)
    hbm = pl.BlockSpec(memory_space=pl.ANY)
    return pl.pallas_call(
        functools.partial(_decode_kernel, grp=grp, rows=rows, page=page, n_seq=nb, groups_per_seq=groups_per_seq),
        out_shape=jax.ShapeDtypeStruct((nb, rows, GROUP_W), F32),
        in_specs=[smem, smem, vmem, vmem, vmem, vmem, vmem, vmem, vmem, hbm, hbm],
        out_specs=vmem,
        scratch_shapes=[pltpu.VMEM((2 * grp, page * N_HEADS, HEAD_W), F32),
                        pltpu.VMEM((2 * grp, page * N_HEADS, HEAD_W), F32),
                        pltpu.SemaphoreType.DMA((2, 2 * grp)),
                        pltpu.VMEM((N_HEADS, 2 * rows, HEAD_W), BF16),
                        pltpu.VMEM((n_rows, 1), F32),
                        pltpu.VMEM((n_rows, 1), F32),
                        pltpu.VMEM((n_rows, HEAD_W), F32),
                        pltpu.VMEM((N_HEADS, page, HEAD_W), F32),
                        pltpu.VMEM((N_HEADS, page, HEAD_W), F32)],
        compiler_params=pltpu.CompilerParams(vmem_limit_bytes=VMEM_LIMIT_BYTES),
        name="attn_decode",
    )(page_table, lam, q, k_new, v_new, gate, snw, bias_dec, bias_new, cache_k, cache_v)


def _mix_out_kernel(ya_ref, yb_ref, x_ref, mod_ref, w_ref, o_ref, *, nseq, rows):
    d = x_ref.shape[-1]
    tm = nseq * rows
    ya = ya_ref[...].reshape(tm, GROUP_W).astype(BF16)
    yb = yb_ref[...].reshape(tm, GROUP_W).astype(BF16)
    out = _dot(ya, w_ref[0:GROUP_W, :]) + _dot(yb, w_ref[GROUP_W:2 * GROUP_W, :])
    gate = mod_ref[:, :, 2 * d:3 * d]
    o_ref[...] = x_ref[...] + gate * out.reshape(nseq, rows, d)


def _mix_out(ya, yb, x, mod, w_out, *, nseq, rows):
    n_s, r_total, d = x.shape
    grid = (n_s // nseq, r_total // rows)
    tok = lambda s, t: (s, t, 0)
    return pl.pallas_call(
        functools.partial(_mix_out_kernel, nseq=nseq, rows=rows),
        out_shape=jax.ShapeDtypeStruct(x.shape, F32),
        grid=grid,
        in_specs=[pl.BlockSpec((nseq, rows, GROUP_W), tok),
                  pl.BlockSpec((nseq, rows, GROUP_W), tok),
                  pl.BlockSpec((nseq, rows, d), tok),
                  pl.BlockSpec((nseq, 1, 3 * d), lambda s, t: (s, 0, 0)),
                  pl.BlockSpec((2 * GROUP_W, d), lambda s, t: (0, 0))],
        out_specs=pl.BlockSpec((nseq, rows, d), tok),
        compiler_params=_cparams(("parallel", "arbitrary")),
        name="mix_out",
    )(ya, yb, x, mod, w_out)


def _largest_tile(n, cap):
    t = cap
    while n % t:
        t //= 2
    return t


def kernel(x_prompt, x_sample, c_prompt, c_sample, cache_k, cache_v, state_hgrn, page_table, norm_w, w_ada,
           b_ada, w_in, w_out, lb_param, hg_norm_w, q_norm_w, k_norm_w, lam_q1, lam_k1, lam_q2, lam_k2,
           sub_norm_w, rel_table):
    depth = w_in.shape[0]
    assert depth == 1
    b, l, d = x_prompt.shape
    db, dl, _ = x_sample.shape
    assert d == 2 * GROUP_W and w_in.shape[2] == 9 * GROUP_W
    page = cache_k.shape[2]
    past = page_table.shape[1] * page
    assert dl % SUBLANES == 0 and cache_k.shape[3] * cache_k.shape[4] == GROUP_W

    w_in_b = w_in[0].astype(BF16)
    w_out_b = w_out[0].astype(BF16)
    nw = norm_w.reshape(1, d)
    hgw = jnp.tile(hg_norm_w.reshape(1, HEAD_W), (1, N_HEADS))
    qnw = jnp.tile(q_norm_w.reshape(1, HALF), (1, GROUP_W // HALF))
    knw = jnp.tile(k_norm_w.reshape(1, HALF), (1, GROUP_W // HALF))
    snw = sub_norm_w.reshape(1, HEAD_W)

    mod = _adaln_mod(jnp.concatenate([c_prompt, c_sample], axis=0), w_ada[0], b_ada[0])
    mod_p = mod[:b].reshape(b, 1, 3 * d)
    mod_s = mod[b:].reshape(db, 1, 3 * d)
    lb, lam = _small_params(lb_param, lam_q1, lam_k1, lam_q2, lam_k2)

    log2e = math.log2(math.e)
    ya, q, kf, vf, gb, kb, vt, st_p = _mix_in(
        x_prompt, mod_p, nw, w_in_b, lb, hgw, qnw, knw, nseq=1, rows=ATTN_TK,
        chunk=CHUNK if l % CHUNK == 0 else l, sub=SUB, s0=None, attn_tile=ATTN_TK, q_dtype=BF16,
        q_scale=HALF ** -0.5 * log2e)
    bias_p = _bias_prompt(rel_table, ATTN_TK, ATTN_TQ, log2e)
    yb = _attn_prompt(lam, q, kb, vt, bias_p, gb, snw, ATTN_TQ, ATTN_TK)
    y_prompt = _mix_out(ya, yb, x_prompt, mod_p, w_out_b, nseq=1, rows=_largest_tile(l, 512))

    nseq = _largest_tile(db, 8)
    ya_s, q_s, kf_s, vf_s, gb_s, st_s = _mix_in(
        x_sample, mod_s, nw, w_in_b, lb, hgw, qnw, knw, nseq=nseq, rows=dl,
        chunk=dl, sub=dl, s0=state_hgrn[0], attn_tile=None, q_dtype=F32, q_scale=HALF ** -0.5)
    grp = _largest_tile(page_table.shape[1] // 2, DECODE_GROUP_PAGES)
    bias_d, bias_n = _bias_decode(rel_table, dl, past, grp * page)
    n_rows = 2 * N_HEADS * dl
    yb_s = _attn_decode(page_table, lam, q_s, kf_s, vf_s, gb_s, snw,
                        bias_d.reshape(-1, n_rows, grp * page), bias_n.reshape(n_rows, LANES),
                        cache_k.reshape(-1, page * N_HEADS, HEAD_W), cache_v.reshape(-1, page * N_HEADS, HEAD_W),
                        n_pages=grp)
    y_sample = _mix_out(ya_s, yb_s, x_sample, mod_s, w_out_b, nseq=nseq, rows=dl)

    hb = N_HEADS
    return (y_prompt, y_sample,
            kf.reshape(1, b, l, hb, HEAD_W), vf.reshape(1, b, l, hb, HEAD_W), st_p[None],
            kf_s.reshape(1, db, dl, hb, HEAD_W), vf_s.reshape(1, db, dl, hb, HEAD_W), st_s[None])
```

```python
import functools
import math

import jax
import jax.numpy as jnp
from jax import lax
from jax.experimental import pallas as pl
from jax.experimental.pallas import tpu as pltpu

F32 = jnp.float32
BF16 = jnp.bfloat16

N_HEADS = 4
HEAD_W = 128
HALF = 64
GROUP_W = N_HEADS * HEAD_W
N_BUCKETS = 32
MAX_EXACT = N_BUCKETS // 2
MAX_DIST = 1024
CHUNK = 64
SUB = 16
EPS = 1e-6
LAMBDA_INIT = 0.8 - 0.6 * math.exp(-0.3 * 0)
NEG = -1e30
ATTN_TQ = 512
ATTN_TK = 256
VT_ROWS = HEAD_W + 16
DECODE_GROUP_PAGES = 8
DECODE_RING_GROUPS = 4

LANES = 128
SUBLANES = 8
VMEM_LIMIT_BYTES = 56 * 1024 * 1024


def _bucket_lower_bounds():
    los = list(range(MAX_EXACT))
    span = N_BUCKETS - MAX_EXACT
    ratio = MAX_DIST // MAX_EXACT
    for m in range(MAX_EXACT, N_BUCKETS):
        d = MAX_EXACT
        while d ** span < (MAX_EXACT ** span) * (ratio ** (m - MAX_EXACT)):
            d += 1
        los.append(d)
    return los


BUCKET_LO = _bucket_lower_bounds()
SATURATED_DIST = BUCKET_LO[-1]


def _cparams(sem):
    return pltpu.CompilerParams(dimension_semantics=sem, vmem_limit_bytes=VMEM_LIMIT_BYTES)


def _split2(x):
    hi = x.astype(BF16)
    lo = (x - hi.astype(F32)).astype(BF16)
    return hi, lo


def _split3(x):
    hi = x.astype(BF16)
    r = x - hi.astype(F32)
    mid = r.astype(BF16)
    lo = (r - mid.astype(F32)).astype(BF16)
    return hi, mid, lo


def _dot(a, b):
    return jnp.dot(a, b, preferred_element_type=F32)


def _dot_nt(a, b):
    return lax.dot_general(a, b, (((1,), (1,)), ((), ())), preferred_element_type=F32)


def _dot_tn(a, b):
    return lax.dot_general(a, b, (((0,), (0,)), ((), ())), preferred_element_type=F32)


def _div_pow2(x, n):
    assert n > 0 and n & (n - 1) == 0
    return lax.shift_right_logical(x, n.bit_length() - 1)


def _sigmoid_pair(x):
    t = jnp.exp(-jnp.abs(x))
    r = 1.0 / (1.0 + t)
    tr = t * r
    pos = x >= 0
    return jnp.where(pos, r, tr), jnp.where(pos, tr, r)


def _mod_kernel(c_ref, w_ref, b_ref, o_ref):
    c = c_ref[...]
    sig, _ = _sigmoid_pair(c)
    s = c * sig
    w = w_ref[...]
    s1, s2 = _split2(s)
    w1, w2 = _split2(w)
    o_ref[...] = _dot(s1, w1) + (_dot(s1, w2) + _dot(s2, w1)) + b_ref[...]


def _adaln_mod(c_all, w_ada, b_ada):
    n, d = c_all.shape
    d3 = w_ada.shape[1]
    bn = d
    return pl.pallas_call(
        _mod_kernel,
        out_shape=jax.ShapeDtypeStruct((n, d3), F32),
        grid=(d3 // bn,),
        in_specs=[pl.BlockSpec((n, d), lambda j: (0, 0)),
                  pl.BlockSpec((d, bn), lambda j: (0, j)),
                  pl.BlockSpec((1, bn), lambda j: (0, j))],
        out_specs=pl.BlockSpec((n, bn), lambda j: (0, j)),
        compiler_params=_cparams(("arbitrary",)),
        name="adaln_mod",
    )(c_all, w_ada, b_ada.reshape(1, d3))


def _small_kernel(lbp_ref, q1_ref, k1_ref, q2_ref, k2_ref, lb_ref, lam_ref):
    p = lbp_ref[...]
    m = jnp.max(p, axis=0, keepdims=True)
    e = jnp.exp(p - m)
    lb_ref[...] = e[0:1, :] / jnp.sum(e, axis=0, keepdims=True)
    s1 = jnp.sum(q1_ref[...] * k1_ref[...], axis=-1, keepdims=True)
    s2 = jnp.sum(q2_ref[...] * k2_ref[...], axis=-1, keepdims=True)
    lam_ref[...] = jnp.exp(s1) - jnp.exp(s2) + LAMBDA_INIT


def _small_params(lb_param, lq1, lk1, lq2, lk2):
    return pl.pallas_call(
        _small_kernel,
        out_shape=(jax.ShapeDtypeStruct((1, lb_param.shape[1]), F32),
                   jax.ShapeDtypeStruct((1, 1), F32)),
        name="small_params",
    )(lb_param, lq1, lk1, lq2, lk2)


def _bias_select(d, tab_ref, h):
    val = jnp.full(d.shape, tab_ref[N_BUCKETS - 1, h], F32)
    for m in range(N_BUCKETS - 2, -1, -1):
        val = jnp.where(d < BUCKET_LO[m + 1], tab_ref[m, h], val)
    return val


def _bias_prompt_kernel(tab_ref, o_ref, *, tk, tq, scale):
    h = pl.program_id(0)
    dl = pl.program_id(1) - (tq // tk - 1)
    kk = lax.broadcasted_iota(jnp.int32, (tk, tq), 0)
    qq = lax.broadcasted_iota(jnp.int32, (tk, tq), 1)
    d = dl * tk + qq - kk
    val = _bias_select(jnp.maximum(d, 0), tab_ref, h) * scale
    o_ref[0, 0] = jnp.where(d >= 0, val, NEG)


def _n_bias_tiles(tk, tq):
    n_const = 0
    while n_const * tk - (tk - 1) < SATURATED_DIST:
        n_const += 1
    return n_const + 1 + (tq // tk - 1)


def _bias_prompt(rel_table, tk, tq, scale):
    nd = _n_bias_tiles(tk, tq)
    return pl.pallas_call(
        functools.partial(_bias_prompt_kernel, tk=tk, tq=tq, scale=scale),
        out_shape=jax.ShapeDtypeStruct((N_HEADS, nd, tk, tq), F32),
        grid=(N_HEADS, nd),
        in_specs=[pl.BlockSpec(memory_space=pltpu.SMEM)],
        out_specs=pl.BlockSpec((1, 1, tk, tq), lambda h, d: (h, d, 0, 0)),
        compiler_params=_cparams(("arbitrary", "arbitrary")),
        name="bias_prompt",
    )(rel_table)


def _bias_decode_kernel(tab_ref, o_ref, on_ref, *, rows, past, width):
    h = pl.program_id(0)
    g = pl.program_id(1)
    i = lax.broadcasted_iota(jnp.int32, (rows, width), 0)
    kpos = g * width + lax.broadcasted_iota(jnp.int32, (rows, width), 1)
    val = _bias_select(past + i - kpos, tab_ref, h)
    o_ref[0, 0, 0] = val
    o_ref[0, 0, 1] = val
    i2 = lax.broadcasted_iota(jnp.int32, (rows, LANES), 0)
    j2 = lax.broadcasted_iota(jnp.int32, (rows, LANES), 1)
    d2 = i2 - j2
    vn = jnp.where(d2 >= 0, _bias_select(jnp.maximum(d2, 0), tab_ref, h), NEG)
    on_ref[0, 0] = vn
    on_ref[0, 1] = vn


def _bias_decode(rel_table, rows, past, width):
    ng = past // width
    return pl.pallas_call(
        functools.partial(_bias_decode_kernel, rows=rows, past=past, width=width),
        out_shape=(jax.ShapeDtypeStruct((ng, N_HEADS, 2, rows, width), F32),
                   jax.ShapeDtypeStruct((N_HEADS, 2, rows, LANES), F32)),
        grid=(N_HEADS, ng),
        in_specs=[pl.BlockSpec(memory_space=pltpu.SMEM)],
        out_specs=(pl.BlockSpec((1, 1, 2, rows, width), lambda h, g: (g, h, 0, 0, 0)),
                   pl.BlockSpec((1, 2, rows, LANES), lambda h, g: (h, 0, 0, 0))),
        compiler_params=_cparams(("arbitrary", "arbitrary")),
        name="bias_decode",
    )(rel_table)


def _group_mean_sq(x, gmat):
    sq = x * x
    outs = []
    w = gmat.shape[0]
    for c0 in range(0, x.shape[1], w):
        hi, lo = _split2(sq[:, c0:c0 + w])
        outs.append(_dot(hi, gmat) + _dot(lo, gmat))
    return jnp.concatenate(outs, axis=1)


def _hgrn_intra_mask(chunk, sub):
    n_sub = chunk // sub
    rr = lax.broadcasted_iota(jnp.int32, (chunk, n_sub * chunk), 0)
    cc = lax.broadcasted_iota(jnp.int32, (chunk, n_sub * chunk), 1)
    own_block = _div_pow2(cc, chunk) == _div_pow2(rr, sub)
    causal = jnp.bitwise_and(cc, chunk - 1) <= rr
    return jnp.where(own_block, jnp.where(causal, 1.0, 0.0), 0.0)


def _hgrn_chunk(bc, qc, kc, vc, st, mask, chunk, sub):
    n_sub = chunk // sub
    o = _dot_nt((qc * jnp.exp(bc)).astype(BF16), st.astype(BF16))
    row = lax.broadcasted_iota(jnp.int32, (chunk, HEAD_W), 0)
    q_parts, k_parts = [], []
    for i in range(n_sub):
        lo, hi = i * sub, (i + 1) * sub
        mid = lo + sub // 2 - 1
        anchor = bc[mid:mid + 1, :]
        q_parts.append(qc[lo:hi, :] * jnp.exp(bc[lo:hi, :] - anchor))
        e = anchor - bc
        if i < n_sub - 1:
            e = jnp.where(row < hi, e, 0.0)
        k_parts.append((kc * jnp.exp(e)).astype(BF16))
    qa = q_parts[0] if n_sub == 1 else jnp.concatenate(q_parts, axis=0)
    ka = k_parts[0] if n_sub == 1 else jnp.concatenate(k_parts, axis=0)
    att = jnp.where(mask > 0.5, _dot_nt(qa.astype(BF16), ka), 0.0)
    vb = vc.astype(BF16)
    vv = vb if n_sub == 1 else jnp.concatenate([vb] * n_sub, axis=0)
    o = o + _dot(att.astype(BF16), vv)
    b_last = bc[chunk - 1:chunk, :]
    ke = kc * jnp.exp(b_last - bc)
    st_new = st * jnp.exp(b_last) + _dot_tn(vc.astype(BF16), ke.astype(BF16))
    return o, st_new


def _mix_in_kernel(*refs, nseq, rows, chunk, sub, per_seq_state, emit_attn_kv, q_scale):
    it = iter(refs)
    x_ref = next(it); mod_ref = next(it); nw_ref = next(it); win_ref = next(it); lb_ref = next(it)
    hgw_ref = next(it); qnw_ref = next(it); knw_ref = next(it)
    s0_ref = next(it) if per_seq_state else None
    ya_ref = next(it); q_ref = next(it); kf_ref = next(it); vf_ref = next(it); gb_ref = next(it)
    kb_ref = next(it) if emit_attn_kv else None
    vt_ref = next(it) if emit_attn_kv else None
    sfin_ref = next(it)
    st_scr = next(it); b_scr = next(it); q_scr = next(it); k_scr = next(it); v_scr = next(it); o_scr = next(it)

    tm = nseq * rows
    d = x_ref.shape[-1]
    n_chunks = tm // chunk

    x3 = x_ref[...]
    ms = jnp.mean(x3 * x3, axis=-1, keepdims=True)
    shift = mod_ref[:, :, 0:d]
    scale = mod_ref[:, :, d:2 * d]
    h3 = x3 * lax.rsqrt(ms + EPS) * nw_ref[...] * (1.0 + scale) + shift
    h = h3.reshape(tm, d).astype(BF16)

    def proj(idx):
        return _dot(h, win_ref[:, idx * GROUP_W:(idx + 1) * GROUP_W])

    def group_b():
        gw = 2 * LANES
        gi = _div_pow2(lax.broadcasted_iota(jnp.int32, (gw, gw), 0), HALF)
        gj = _div_pow2(lax.broadcasted_iota(jnp.int32, (gw, gw), 1), HALF)
        gmat = jnp.where(gi == gj, 1.0 / HALF, 0.0).astype(BF16)

        qb = proj(5)
        qn = qb * lax.rsqrt(_group_mean_sq(qb, gmat) + EPS) * qnw_ref[...] * q_scale
        q_ref[...] = qn.reshape(nseq, rows, GROUP_W).astype(q_ref.dtype)
        kb = proj(6)
        kn = kb * lax.rsqrt(_group_mean_sq(kb, gmat) + EPS) * knw_ref[...]
        vb = proj(7)
        for hh in range(N_HEADS):
            hs = slice(hh * HEAD_W, (hh + 1) * HEAD_W)
            kf_ref[:, pl.ds(hh, rows, stride=N_HEADS), :] = kn[:, hs].reshape(nseq, rows, HEAD_W)
            vf_ref[:, pl.ds(hh, rows, stride=N_HEADS), :] = vb[:, hs].reshape(nseq, rows, HEAD_W)
        if emit_attn_kv:
            kb_ref[...] = kn.reshape(nseq, rows, GROUP_W).astype(BF16)
            for hh in range(N_HEADS):
                vt_ref[0, hh, 0, 0:HEAD_W, :] = vb[:, hh * HEAD_W:(hh + 1) * HEAD_W].T.astype(BF16)
                vt_ref[0, hh, 0, HEAD_W:VT_ROWS, :] = jnp.ones((VT_ROWS - HEAD_W, tm), BF16)
        zb = proj(8)
        sgz, _ = _sigmoid_pair(zb)
        gb_ref[...] = (zb * sgz).reshape(nseq, rows, GROUP_W)

    lb = lb_ref[...]
    fa = proj(1)
    sig, nsig = _sigmoid_pair(fa)
    logf = jnp.log(lb + (1.0 - lb) * sig)
    k_scr[...] = (1.0 - lb) * nsig
    q_scr[...] = proj(0) * (HEAD_W ** -0.5)
    v_scr[...] = proj(2)
    ri = lax.broadcasted_iota(jnp.int32, (tm, tm), 0)
    ci = lax.broadcasted_iota(jnp.int32, (tm, tm), 1)
    same_chunk = _div_pow2(ri, chunk) == _div_pow2(ci, chunk)
    tri = jnp.where(ci <= ri, jnp.where(same_chunk, 1.0, 0.0), 0.0).astype(BF16)
    l1, l2, l3 = _split3(logf)
    b_scr[...] = _dot(tri, l1) + _dot(tri, l2) + _dot(tri, l3)

    if not per_seq_state:
        @pl.when(pl.program_id(1) == 0)
        def _():
            st_scr[...] = jnp.zeros_like(st_scr)

    intra_mask = _hgrn_intra_mask(chunk, sub)
    for hh in range(N_HEADS):
        hs = slice(hh * HEAD_W, (hh + 1) * HEAD_W)
        st = None if per_seq_state else st_scr[hh]
        for c in range(n_chunks):
            rs = slice(c * chunk, (c + 1) * chunk)
            if per_seq_state:
                st = s0_ref[c, hh].T
            o, st = _hgrn_chunk(b_scr[rs, hs], q_scr[rs, hs], k_scr[rs, hs], v_scr[rs, hs], st, intra_mask,
                                chunk, sub)
            o_scr[rs, hs] = o
            if per_seq_state:
                sfin_ref[c, hh] = st.T
        if not per_seq_state:
            st_scr[hh] = st

    group_b()

    if not per_seq_state:
        @pl.when(pl.program_id(1) == pl.num_programs(1) - 1)
        def _():
            for hh in range(N_HEADS):
                sfin_ref[0, hh] = st_scr[hh].T

    sg, _ = _sigmoid_pair(proj(3))
    za = proj(4)
    sz, _ = _sigmoid_pair(za)
    outs = []
    for hh in range(N_HEADS):
        hs = slice(hh * HEAD_W, (hh + 1) * HEAD_W)
        oh = o_scr[:, hs] * sg[:, hs]
        mo = jnp.mean(oh * oh, axis=-1, keepdims=True)
        outs.append(oh * lax.rsqrt(mo + EPS))
    ya = jnp.concatenate(outs, axis=1) * hgw_ref[...] * (za * sz)
    ya_ref[...] = ya.reshape(nseq, rows, GROUP_W).astype(ya_ref.dtype)


def _mix_in(x, mod, norm_w, w_in, lb, hgw, qnw, knw, *, nseq, rows, chunk, sub, s0, attn_tile, q_dtype, q_scale):
    n_s, r_total, d = x.shape
    per_seq_state = s0 is not None
    emit = attn_tile is not None
    tm = nseq * rows
    if per_seq_state:
        assert rows == r_total and rows == chunk and n_s % nseq == 0
        grid = (n_s // nseq,)
        tok = lambda i: (i, 0, 0)
        seq = lambda i: (i, 0, 0)
        const2 = lambda i: (0, 0)
        st_map = lambda i: (i, 0, 0, 0)
        sem = ("arbitrary",)
    else:
        assert nseq == 1 and r_total % rows == 0 and rows % chunk == 0
        grid = (n_s, r_total // rows)
        tok = lambda b, t: (b, t, 0)
        seq = lambda b, t: (b, 0, 0)
        const2 = lambda b, t: (0, 0)
        st_map = lambda b, t: (b, 0, 0, 0)
        sem = ("parallel", "arbitrary")
    if emit:
        assert rows == attn_tile
    d_in = w_in.shape[1]
    tok_spec = pl.BlockSpec((nseq, rows, GROUP_W), tok)
    tokhead_spec = pl.BlockSpec((nseq, rows * N_HEADS, HEAD_W), tok)
    in_specs = [pl.BlockSpec((nseq, rows, d), tok),
                pl.BlockSpec((nseq, 1, 3 * d), seq),
                pl.BlockSpec((1, d), const2),
                pl.BlockSpec((d, d_in), const2),
                pl.BlockSpec((1, GROUP_W), const2),
                pl.BlockSpec((1, GROUP_W), const2),
                pl.BlockSpec((1, GROUP_W), const2),
                pl.BlockSpec((1, GROUP_W), const2)]
    args = [x, mod, norm_w, w_in, lb, hgw, qnw, knw]
    if per_seq_state:
        in_specs.append(pl.BlockSpec((nseq, N_HEADS, HEAD_W, HEAD_W), st_map))
        args.append(s0)
    out_shape = [jax.ShapeDtypeStruct((n_s, r_total, GROUP_W), q_dtype),
                 jax.ShapeDtypeStruct((n_s, r_total, GROUP_W), q_dtype),
                 jax.ShapeDtypeStruct((n_s, r_total * N_HEADS, HEAD_W), F32),
                 jax.ShapeDtypeStruct((n_s, r_total * N_HEADS, HEAD_W), F32),
                 jax.ShapeDtypeStruct((n_s, r_total, GROUP_W), F32)]
    out_specs = [tok_spec, tok_spec, tokhead_spec, tokhead_spec, tok_spec]
    if emit:
        nt = r_total // attn_tile
        out_shape += [jax.ShapeDtypeStruct((n_s, r_total, GROUP_W), BF16),
                      jax.ShapeDtypeStruct((n_s, N_HEADS, nt, VT_ROWS, attn_tile), BF16)]
        out_specs += [tok_spec,
                      pl.BlockSpec((1, N_HEADS, 1, VT_ROWS, attn_tile), lambda b, t: (b, 0, t, 0, 0))]
    out_shape.append(jax.ShapeDtypeStruct((n_s, N_HEADS, HEAD_W, HEAD_W), F32))
    out_specs.append(pl.BlockSpec((nseq, N_HEADS, HEAD_W, HEAD_W), st_map))
    scratch = [pltpu.VMEM((N_HEADS, HEAD_W, HEAD_W), F32)] + [pltpu.VMEM((tm, GROUP_W), F32)] * 5
    kern = functools.partial(_mix_in_kernel, nseq=nseq, rows=rows, chunk=chunk, sub=sub,
                             per_seq_state=per_seq_state, emit_attn_kv=emit, q_scale=q_scale)
    return pl.pallas_call(
        kern, out_shape=tuple(out_shape), grid=grid, in_specs=in_specs, out_specs=tuple(out_specs),
        scratch_shapes=scratch, compiler_params=_cparams(sem),
        name="mix_in_sample" if per_seq_state else "mix_in_prompt",
    )(*args)


def _sub_norm_gate(o, snw, gate):
    ms = jnp.mean(o * o, axis=-1, keepdims=True)
    return o * lax.rsqrt(ms + EPS) * snw * (1.0 - LAMBDA_INIT) * gate


def _attn_kernel(lam_ref, q_ref, k_ref, vt_ref, bias_ref, g_ref, snw_ref, o_ref,
                 qs_scr, sa_scr, sb_scr, m_scr, acc_scr, *, tq, tk, n_bias):
    qi = pl.program_id(2)
    ratio = tq // tk
    n_tiles = ratio * (qi + 1)
    q = q_ref[0].astype(F32)
    lane = lax.broadcasted_iota(jnp.int32, q.shape, 1)
    qs_scr[0:tq, :] = jnp.where(lane < HALF, q, 0.0).astype(BF16)
    qs_scr[tq:2 * tq, :] = jnp.where(lane >= HALF, q, 0.0).astype(BF16)
    m_scr[...] = jnp.full(m_scr.shape, NEG, F32)
    acc_scr[...] = jnp.zeros_like(acc_scr)

    def scores(t, s_ref):
        r0 = pl.multiple_of(t * tk, tk)
        kt = k_ref[0, pl.ds(r0, tk), :]
        bt = bias_ref[0, jnp.clip(ratio * qi - t + (ratio - 1), 0, n_bias - 1)]
        for c in range(2):
            s_ref[:, c * tq:(c + 1) * tq] = _dot_nt(kt, qs_scr[c * tq:(c + 1) * tq, :]) + bt

    def softmax_pv(t, s_ref):
        vt = vt_ref[0, 0, t]
        for c in range(2):
            cols = slice(c * tq, (c + 1) * tq)
            m_old = m_scr[:, cols]
            m_new = jnp.maximum(m_old, jnp.max(s_ref[:, cols], axis=0, keepdims=True))
            alpha = jnp.exp2(m_old - m_new)
            p = jnp.exp2(s_ref[:, cols] - m_new)
            acc_scr[:, cols] = acc_scr[:, cols] * alpha + _dot(vt, p.astype(BF16))
            m_scr[:, cols] = m_new

    scores(0, sa_scr)

    def pair(j):
        t0 = 2 * j
        scores(t0 + 1, sb_scr)
        softmax_pv(t0, sa_scr)
        scores(jnp.minimum(t0 + 2, n_tiles - 1), sa_scr)
        softmax_pv(t0 + 1, sb_scr)

    def body(jj, carry):
        pair(2 * jj)
        pair(2 * jj + 1)
        return carry

    n_pairs = n_tiles // 2
    lax.fori_loop(0, n_pairs // 2, body, 0)

    @pl.when(n_pairs % 2 == 1)
    def _():
        pair(n_pairs - 1)

    lam = lam_ref[0, 0]
    inv = 1.0 / acc_scr[HEAD_W:HEAD_W + 1, :]
    acc = acc_scr[0:HEAD_W, :]
    o_t = acc[:, 0:tq] * inv[:, 0:tq] - lam * (acc[:, tq:2 * tq] * inv[:, tq:2 * tq])
    o_ref[0] = _sub_norm_gate(o_t.T, snw_ref[...], g_ref[0]).astype(o_ref.dtype)


def _attn_prompt(lam, q, kb, vt, bias, gate, snw, tq, tk):
    b, l, _ = q.shape
    assert tq % (2 * tk) == 0 and l % tq == 0
    nq = l // tq
    nt = l // tk
    nd = bias.shape[1]
    return pl.pallas_call(
        functools.partial(_attn_kernel, tq=tq, tk=tk, n_bias=nd),
        out_shape=jax.ShapeDtypeStruct((b, l, GROUP_W), BF16),
        grid=(b, N_HEADS, nq),
        in_specs=[pl.BlockSpec(memory_space=pltpu.SMEM),
                  pl.BlockSpec((1, tq, HEAD_W), lambda bb, h, i: (bb, i, h)),
                  pl.BlockSpec((1, l, HEAD_W), lambda bb, h, i: (bb, 0, h)),
                  pl.BlockSpec((1, 1, nt, VT_ROWS, tk), lambda bb, h, i: (bb, h, 0, 0, 0)),
                  pl.BlockSpec((1, nd, tk, tq), lambda bb, h, i: (h, 0, 0, 0)),
                  pl.BlockSpec((1, tq, HEAD_W), lambda bb, h, i: (bb, i, h)),
                  pl.BlockSpec((1, HEAD_W), lambda bb, h, i: (0, 0))],
        out_specs=pl.BlockSpec((1, tq, HEAD_W), lambda bb, h, i: (bb, i, h)),
        scratch_shapes=[pltpu.VMEM((2 * tq, HEAD_W), BF16),
                        pltpu.VMEM((tk, 2 * tq), F32),
                        pltpu.VMEM((tk, 2 * tq), F32),
                        pltpu.VMEM((1, 2 * tq), F32),
                        pltpu.VMEM((VT_ROWS, 2 * tq), F32)],
        compiler_params=_cparams(("parallel", "parallel", "arbitrary")),
        name="attn_prompt",
    )(lam, q, kb, vt, bias, gate, snw)


def _decode_kernel(pt_ref, lam_ref, q_ref, kn_ref, vn_ref, g_ref, snw_ref, bias_ref, biasn_ref, ck_hbm, cv_hbm,
                   o_ref, kbuf, vbuf, sem, qst_scr, m_scr, l_scr, acc_scr, kpg_scr, vpg_scr,
                   *, grp, ring, rows, page, n_seq, groups_per_seq):
    hr = 2 * rows
    total_groups = n_seq * groups_per_seq

    def group_copies(gidx, slot0):
        b = _div_pow2(gidx, groups_per_seq)
        p0 = jnp.bitwise_and(gidx, groups_per_seq - 1) * grp
        copies = []
        for j in range(grp):
            pid = pt_ref[b, p0 + j]
            copies.append(pltpu.make_async_copy(ck_hbm.at[pid], kbuf.at[slot0 + j], sem.at[0, slot0 + j]))
            copies.append(pltpu.make_async_copy(cv_hbm.at[pid], vbuf.at[slot0 + j], sem.at[1, slot0 + j]))
        return copies

    def start_group(gidx, slot0):
        for cp in group_copies(gidx, slot0):
            cp.start()

    def wait_group(gidx, slot0):
        for cp in group_copies(gidx, slot0):
            cp.wait()

    def head_page(buf, slot, hh):
        return buf[slot, pl.ds(hh, page, stride=N_HEADS), :].astype(BF16)

    def init_sequence(b):
        q = q_ref[b].astype(F32)
        lane = lax.broadcasted_iota(jnp.int32, (rows, HEAD_W), 1)
        kpg_scr[...] = jnp.zeros_like(kpg_scr)
        vpg_scr[...] = jnp.zeros_like(vpg_scr)
        for hh in range(N_HEADS):
            qh = q[:, hh * HEAD_W:(hh + 1) * HEAD_W]
            qst_scr[hh] = jnp.concatenate([jnp.where(lane < HALF, qh, 0.0), jnp.where(lane >= HALF, qh, 0.0)],
                                          axis=0).astype(BF16)
            kpg_scr[hh, 0:rows, :] = kn_ref[b, pl.ds(hh, rows, stride=N_HEADS), :]
            vpg_scr[hh, 0:rows, :] = vn_ref[b, pl.ds(hh, rows, stride=N_HEADS), :]
        m_scr[...] = jnp.full(m_scr.shape, NEG, F32)
        l_scr[...] = jnp.zeros_like(l_scr)
        acc_scr[...] = jnp.zeros_like(acc_scr)

    def update(s, n_blocks, value_fn):
        m_old = m_scr[...]
        m_new = jnp.maximum(m_old, jnp.max(s, axis=-1, keepdims=True))
        alpha = jnp.exp(m_old - m_new)
        p = jnp.exp(s - m_new)
        l_scr[...] = alpha * l_scr[...] + jnp.sum(p, axis=-1, keepdims=True)
        pb = p.astype(BF16)
        pv_rows = []
        for hh in range(N_HEADS):
            ph = pb[hh * hr:(hh + 1) * hr, :]
            pv = _dot(ph[:, 0:page], value_fn(hh, 0))
            for j in range(1, n_blocks):
                pv = pv + _dot(ph[:, j * page:(j + 1) * page], value_fn(hh, j))
            pv_rows.append(pv)
        acc_scr[...] = acc_scr[...] * alpha + jnp.concatenate(pv_rows, axis=0)
        m_scr[...] = m_new

    def scores(n_blocks, key_fn):
        s_rows = []
        for hh in range(N_HEADS):
            qh = qst_scr[hh]
            blocks = [_dot_nt(qh, key_fn(hh, j)) for j in range(n_blocks)]
            s_rows.append(blocks[0] if n_blocks == 1 else jnp.concatenate(blocks, axis=1))
        return jnp.concatenate(s_rows, axis=0)

    def consume_group(gs, slot0):
        s = scores(grp, lambda hh, j: head_page(kbuf, slot0 + j, hh)) + bias_ref[gs]
        update(s, grp, lambda hh, j: head_page(vbuf, slot0 + j, hh))

    def finish_sequence(b):
        s_new = scores(1, lambda hh, j: kpg_scr[hh].astype(BF16)) + biasn_ref[...]
        update(s_new, 1, lambda hh, j: vpg_scr[hh].astype(BF16))
        lam = lam_ref[0, 0]
        inv = 1.0 / l_scr[...]
        acc = acc_scr[...] * inv
        gate = g_ref[b]
        for hh in range(N_HEADS):
            hs = slice(hh * HEAD_W, (hh + 1) * HEAD_W)
            r0 = hh * hr
            o = acc[r0:r0 + rows, :] - lam * acc[r0 + rows:r0 + 2 * rows, :]
            o_ref[b, :, hs] = _sub_norm_gate(o, snw_ref[...], gate[:, hs]).astype(o_ref.dtype)

    for part in range(ring):
        start_group(part, part * grp)

    def seq_body(b, carry):
        init_sequence(b)

        def ring_body(it, c):
            g0 = b * groups_per_seq + ring * it
            for part in range(ring):
                gidx = g0 + part
                slot0 = part * grp
                wait_group(gidx, slot0)
                consume_group(ring * it + part, slot0)

                @pl.when(gidx + ring < total_groups)
                def _():
                    start_group(gidx + ring, slot0)
            return c

        lax.fori_loop(0, groups_per_seq // ring, ring_body, 0)
        finish_sequence(b)
        return carry

    lax.fori_loop(0, n_seq, seq_body, 0)


def _attn_decode(page_table, lam, q, k_new, v_new, gate, snw, bias_dec, bias_new, cache_k, cache_v, n_pages):
    nb, rows, _ = q.shape
    pages_per_seq = page_table.shape[1]
    page = cache_k.shape[1] // N_HEADS
    grp = n_pages
    groups_per_seq = pages_per_seq // grp
    assert pages_per_seq % grp == 0 and page == LANES
    ring = min(DECODE_RING_GROUPS, groups_per_seq)
    assert groups_per_seq % ring == 0 and groups_per_seq & (groups_per_seq - 1) == 0
    assert bias_dec.shape == (groups_per_seq, 2 * N_HEADS * rows, grp * page)
    n_rows = 2 * N_HEADS * rows
    smem = pl.BlockSpec(memory_space=pltpu.SMEM)
    vmem = pl.BlockSpec(memory_space=pltpu.VMEM)
    hbm = pl.BlockSpec(memory_space=pl.ANY)
    return pl.pallas_call(
        functools.partial(_decode_kernel, grp=grp, ring=ring, rows=rows, page=page, n_seq=nb,
                          groups_per_seq=groups_per_seq),
        out_shape=jax.ShapeDtypeStruct((nb, rows, GROUP_W), F32),
        in_specs=[smem, smem, vmem, vmem, vmem, vmem, vmem, vmem, vmem, hbm, hbm],
        out_specs=vmem,
        scratch_shapes=[pltpu.VMEM((ring * grp, page * N_HEADS, HEAD_W), F32),
                        pltpu.VMEM((ring * grp, page * N_HEADS, HEAD_W), F32),
                        pltpu.SemaphoreType.DMA((2, ring * grp)),
                        pltpu.VMEM((N_HEADS, 2 * rows, HEAD_W), BF16),
                        pltpu.VMEM((n_rows, 1), F32),
                        pltpu.VMEM((n_rows, 1), F32),
                        pltpu.VMEM((n_rows, HEAD_W), F32),
                        pltpu.VMEM((N_HEADS, page, HEAD_W), F32),
                        pltpu.VMEM((N_HEADS, page, HEAD_W), F32)],
        compiler_params=pltpu.CompilerParams(vmem_limit_bytes=VMEM_LIMIT_BYTES),
        name="attn_decode",
    )(page_table, lam, q, k_new, v_new, gate, snw, bias_dec, bias_new, cache_k, cache_v)


def _mix_out_kernel(ya_ref, yb_ref, x_ref, mod_ref, w_ref, o_ref, *, nseq, rows):
    d = x_ref.shape[-1]
    tm = nseq * rows
    ya = ya_ref[...].reshape(tm, GROUP_W).astype(BF16)
    yb = yb_ref[...].reshape(tm, GROUP_W).astype(BF16)
    out = _dot(ya, w_ref[0:GROUP_W, :]) + _dot(yb, w_ref[GROUP_W:2 * GROUP_W, :])
    gate = mod_ref[:, :, 2 * d:3 * d]
    o_ref[...] = x_ref[...] + gate * out.reshape(nseq, rows, d)


def _mix_out(ya, yb, x, mod, w_out, *, nseq, rows):
    n_s, r_total, d = x.shape
    grid = (n_s // nseq, r_total // rows)
    tok = lambda s, t: (s, t, 0)
    return pl.pallas_call(
        functools.partial(_mix_out_kernel, nseq=nseq, rows=rows),
        out_shape=jax.ShapeDtypeStruct(x.shape, F32),
        grid=grid,
        in_specs=[pl.BlockSpec((nseq, rows, GROUP_W), tok),
                  pl.BlockSpec((nseq, rows, GROUP_W), tok),
                  pl.BlockSpec((nseq, rows, d), tok),
                  pl.BlockSpec((nseq, 1, 3 * d), lambda s, t: (s, 0, 0)),
                  pl.BlockSpec((2 * GROUP_W, d), lambda s, t: (0, 0))],
        out_specs=pl.BlockSpec((nseq, rows, d), tok),
        compiler_params=_cparams(("parallel", "arbitrary")),
        name="mix_out",
    )(ya, yb, x, mod, w_out)


def _largest_tile(n, cap):
    t = cap
    while n % t:
        t //= 2
    return t


def kernel(x_prompt, x_sample, c_prompt, c_sample, cache_k, cache_v, state_hgrn, page_table, norm_w, w_ada,
           b_ada, w_in, w_out, lb_param, hg_norm_w, q_norm_w, k_norm_w, lam_q1, lam_k1, lam_q2, lam_k2,
           sub_norm_w, rel_table):
    depth = w_in.shape[0]
    assert depth == 1
    b, l, d = x_prompt.shape
    db, dl, _ = x_sample.shape
    assert d == 2 * GROUP_W and w_in.shape[2] == 9 * GROUP_W
    page = cache_k.shape[2]
    past = page_table.shape[1] * page
    assert dl % SUBLANES == 0 and cache_k.shape[3] * cache_k.shape[4] == GROUP_W

    w_in_b = w_in[0].astype(BF16)
    w_out_b = w_out[0].astype(BF16)
    nw = norm_w.reshape(1, d)
    hgw = jnp.tile(hg_norm_w.reshape(1, HEAD_W), (1, N_HEADS))
    qnw = jnp.tile(q_norm_w.reshape(1, HALF), (1, GROUP_W // HALF))
    knw = jnp.tile(k_norm_w.reshape(1, HALF), (1, GROUP_W // HALF))
    snw = sub_norm_w.reshape(1, HEAD_W)

    mod = _adaln_mod(jnp.concatenate([c_prompt, c_sample], axis=0), w_ada[0], b_ada[0])
    mod_p = mod[:b].reshape(b, 1, 3 * d)
    mod_s = mod[b:].reshape(db, 1, 3 * d)
    lb, lam = _small_params(lb_param, lam_q1, lam_k1, lam_q2, lam_k2)

    log2e = math.log2(math.e)
    ya, q, kf, vf, gb, kb, vt, st_p = _mix_in(
        x_prompt, mod_p, nw, w_in_b, lb, hgw, qnw, knw, nseq=1, rows=ATTN_TK,
        chunk=CHUNK if l % CHUNK == 0 else l, sub=SUB, s0=None, attn_tile=ATTN_TK, q_dtype=BF16,
        q_scale=HALF ** -0.5 * log2e)
    bias_p = _bias_prompt(rel_table, ATTN_TK, ATTN_TQ, log2e)
    yb = _attn_prompt(lam, q, kb, vt, bias_p, gb, snw, ATTN_TQ, ATTN_TK)
    y_prompt = _mix_out(ya, yb, x_prompt, mod_p, w_out_b, nseq=1, rows=_largest_tile(l, 512))

    nseq = _largest_tile(db, 8)
    ya_s, q_s, kf_s, vf_s, gb_s, st_s = _mix_in(
        x_sample, mod_s, nw, w_in_b, lb, hgw, qnw, knw, nseq=nseq, rows=dl,
        chunk=dl, sub=dl, s0=state_hgrn[0], attn_tile=None, q_dtype=F32, q_scale=HALF ** -0.5)
    grp = _largest_tile(page_table.shape[1] // 2, DECODE_GROUP_PAGES)
    bias_d, bias_n = _bias_decode(rel_table, dl, past, grp * page)
    n_rows = 2 * N_HEADS * dl
    yb_s = _attn_decode(page_table, lam, q_s, kf_s, vf_s, gb_s, snw,
                        bias_d.reshape(-1, n_rows, grp * page), bias_n.reshape(n_rows, LANES),
                        cache_k.reshape(-1, page * N_HEADS, HEAD_W), cache_v.reshape(-1, page * N_HEADS, HEAD_W),
                        n_pages=grp)
    y_sample = _mix_out(ya_s, yb_s, x_sample, mod_s, w_out_b, nseq=nseq, rows=dl)

    hb = N_HEADS
    return (y_prompt, y_sample,
            kf.reshape(1, b, l, hb, HEAD_W), vf.reshape(1, b, l, hb, HEAD_W), st_p[None],
            kf_s.reshape(1, db, dl, hb, HEAD_W), vf_s.reshape(1, db, dl, hb, HEAD_W), st_s[None])
```

```python
import functools
import math

import jax
import jax.numpy as jnp
from jax import lax
from jax.experimental import pallas as pl
from jax.experimental.pallas import tpu as pltpu

F32 = jnp.float32
BF16 = jnp.bfloat16

N_HEADS = 4
HEAD_W = 128
HALF = 64
GROUP_W = N_HEADS * HEAD_W
N_BUCKETS = 32
MAX_EXACT = N_BUCKETS // 2
MAX_DIST = 1024
CHUNK = 64
SUB = 16
EPS = 1e-6
LAMBDA_INIT = 0.8 - 0.6 * math.exp(-0.3 * 0)
NEG = -1e30
ATTN_TQ = 512
ATTN_TK = 256
VT_ROWS = HEAD_W + 16
DECODE_GROUP_PAGES = 8
DECODE_RING_GROUPS = 4

LANES = 128
SUBLANES = 8
VMEM_LIMIT_BYTES = 56 * 1024 * 1024


def _bucket_lower_bounds():
    los = list(range(MAX_EXACT))
    span = N_BUCKETS - MAX_EXACT
    ratio = MAX_DIST // MAX_EXACT
    for m in range(MAX_EXACT, N_BUCKETS):
        d = MAX_EXACT
        while d ** span < (MAX_EXACT ** span) * (ratio ** (m - MAX_EXACT)):
            d += 1
        los.append(d)
    return los


BUCKET_LO = _bucket_lower_bounds()
SATURATED_DIST = BUCKET_LO[-1]


def _cparams(sem):
    return pltpu.CompilerParams(dimension_semantics=sem, vmem_limit_bytes=VMEM_LIMIT_BYTES)


def _split2(x):
    hi = x.astype(BF16)
    lo = (x - hi.astype(F32)).astype(BF16)
    return hi, lo


def _split3(x):
    hi = x.astype(BF16)
    r = x - hi.astype(F32)
    mid = r.astype(BF16)
    lo = (r - mid.astype(F32)).astype(BF16)
    return hi, mid, lo


def _dot(a, b):
    return jnp.dot(a, b, preferred_element_type=F32)


def _dot_nt(a, b):
    return lax.dot_general(a, b, (((1,), (1,)), ((), ())), preferred_element_type=F32)


def _dot_tn(a, b):
    return lax.dot_general(a, b, (((0,), (0,)), ((), ())), preferred_element_type=F32)


def _div_pow2(x, n):
    assert n > 0 and n & (n - 1) == 0
    return lax.shift_right_logical(x, n.bit_length() - 1)


def _sigmoid_pair(x):
    t = jnp.exp(-jnp.abs(x))
    r = 1.0 / (1.0 + t)
    tr = t * r
    pos = x >= 0
    return jnp.where(pos, r, tr), jnp.where(pos, tr, r)


def _mod_kernel(c_ref, w_ref, b_ref, o_ref):
    c = c_ref[...]
    sig, _ = _sigmoid_pair(c)
    s = c * sig
    w = w_ref[...]
    s1, s2 = _split2(s)
    w1, w2 = _split2(w)
    o_ref[...] = _dot(s1, w1) + (_dot(s1, w2) + _dot(s2, w1)) + b_ref[...]


def _adaln_mod(c_all, w_ada, b_ada):
    n, d = c_all.shape
    d3 = w_ada.shape[1]
    bn = d
    return pl.pallas_call(
        _mod_kernel,
        out_shape=jax.ShapeDtypeStruct((n, d3), F32),
        grid=(d3 // bn,),
        in_specs=[pl.BlockSpec((n, d), lambda j: (0, 0)),
                  pl.BlockSpec((d, bn), lambda j: (0, j)),
                  pl.BlockSpec((1, bn), lambda j: (0, j))],
        out_specs=pl.BlockSpec((n, bn), lambda j: (0, j)),
        compiler_params=_cparams(("arbitrary",)),
        name="adaln_mod",
    )(c_all, w_ada, b_ada.reshape(1, d3))


def _small_kernel(lbp_ref, q1_ref, k1_ref, q2_ref, k2_ref, lb_ref, lam_ref):
    p = lbp_ref[...]
    m = jnp.max(p, axis=0, keepdims=True)
    e = jnp.exp(p - m)
    lb_ref[...] = e[0:1, :] / jnp.sum(e, axis=0, keepdims=True)
    s1 = jnp.sum(q1_ref[...] * k1_ref[...], axis=-1, keepdims=True)
    s2 = jnp.sum(q2_ref[...] * k2_ref[...], axis=-1, keepdims=True)
    lam_ref[...] = jnp.exp(s1) - jnp.exp(s2) + LAMBDA_INIT


def _small_params(lb_param, lq1, lk1, lq2, lk2):
    return pl.pallas_call(
        _small_kernel,
        out_shape=(jax.ShapeDtypeStruct((1, lb_param.shape[1]), F32),
                   jax.ShapeDtypeStruct((1, 1), F32)),
        name="small_params",
    )(lb_param, lq1, lk1, lq2, lk2)


def _bias_select(d, tab_ref, h):
    val = jnp.full(d.shape, tab_ref[N_BUCKETS - 1, h], F32)
    for m in range(N_BUCKETS - 2, -1, -1):
        val = jnp.where(d < BUCKET_LO[m + 1], tab_ref[m, h], val)
    return val


def _bias_prompt_kernel(tab_ref, o_ref, *, tk, tq, scale):
    h = pl.program_id(0)
    dl = pl.program_id(1) - (tq // tk - 1)
    kk = lax.broadcasted_iota(jnp.int32, (tk, tq), 0)
    qq = lax.broadcasted_iota(jnp.int32, (tk, tq), 1)
    d = dl * tk + qq - kk
    val = _bias_select(jnp.maximum(d, 0), tab_ref, h) * scale
    o_ref[0, 0] = jnp.where(d >= 0, val, NEG)


def _n_bias_tiles(tk, tq):
    n_const = 0
    while n_const * tk - (tk - 1) < SATURATED_DIST:
        n_const += 1
    return n_const + 1 + (tq // tk - 1)


def _bias_prompt(rel_table, tk, tq, scale):
    nd = _n_bias_tiles(tk, tq)
    return pl.pallas_call(
        functools.partial(_bias_prompt_kernel, tk=tk, tq=tq, scale=scale),
        out_shape=jax.ShapeDtypeStruct((N_HEADS, nd, tk, tq), F32),
        grid=(N_HEADS, nd),
        in_specs=[pl.BlockSpec(memory_space=pltpu.SMEM)],
        out_specs=pl.BlockSpec((1, 1, tk, tq), lambda h, d: (h, d, 0, 0)),
        compiler_params=_cparams(("arbitrary", "arbitrary")),
        name="bias_prompt",
    )(rel_table)


def _bias_decode_kernel(tab_ref, o_ref, on_ref, *, rows, past, width):
    h = pl.program_id(0)
    g = pl.program_id(1)
    i = lax.broadcasted_iota(jnp.int32, (rows, width), 0)
    kpos = g * width + lax.broadcasted_iota(jnp.int32, (rows, width), 1)
    val = _bias_select(past + i - kpos, tab_ref, h)
    o_ref[0, 0, 0] = val
    o_ref[0, 0, 1] = val
    i2 = lax.broadcasted_iota(jnp.int32, (rows, LANES), 0)
    j2 = lax.broadcasted_iota(jnp.int32, (rows, LANES), 1)
    d2 = i2 - j2
    vn = jnp.where(d2 >= 0, _bias_select(jnp.maximum(d2, 0), tab_ref, h), NEG)
    on_ref[0, 0] = vn
    on_ref[0, 1] = vn


def _bias_decode(rel_table, rows, past, width):
    ng = past // width
    return pl.pallas_call(
        functools.partial(_bias_decode_kernel, rows=rows, past=past, width=width),
        out_shape=(jax.ShapeDtypeStruct((ng, N_HEADS, 2, rows, width), F32),
                   jax.ShapeDtypeStruct((N_HEADS, 2, rows, LANES), F32)),
        grid=(N_HEADS, ng),
        in_specs=[pl.BlockSpec(memory_space=pltpu.SMEM)],
        out_specs=(pl.BlockSpec((1, 1, 2, rows, width), lambda h, g: (g, h, 0, 0, 0)),
                   pl.BlockSpec((1, 2, rows, LANES), lambda h, g: (h, 0, 0, 0))),
        compiler_params=_cparams(("arbitrary", "arbitrary")),
        name="bias_decode",
    )(rel_table)


def _group_mean_sq(x, gmat):
    sq = x * x
    outs = []
    w = gmat.shape[0]
    for c0 in range(0, x.shape[1], w):
        outs.append(_dot(sq[:, c0:c0 + w].astype(BF16), gmat))
    return jnp.concatenate(outs, axis=1)


def _hgrn_levels(sub):
    levels, h = [], sub // 2
    while h >= 1:
        levels.append(h)
        h //= 2
    return levels


def _hgrn_masks(chunk, sub, n):
    n_sub = chunk // sub
    masks = {"wide": None}
    if n_sub > 1:
        rr = lax.broadcasted_iota(jnp.int32, (chunk, n_sub * chunk), 0)
        cc = lax.broadcasted_iota(jnp.int32, (chunk, n_sub * chunk), 1)
        own_anchor = _div_pow2(cc, chunk) == _div_pow2(rr, sub)
        earlier = _div_pow2(jnp.bitwise_and(cc, chunk - 1), sub) < _div_pow2(rr, sub)
        masks["wide"] = jnp.where(own_anchor, jnp.where(earlier, 1.0, 0.0), 0.0)
    rr = lax.broadcasted_iota(jnp.int32, (n, n), 0)
    cc = lax.broadcasted_iota(jnp.int32, (n, n), 1)
    for h in _hgrn_levels(sub):
        same = _div_pow2(rr, 2 * h) == _div_pow2(cc, 2 * h)
        t_second = jnp.bitwise_and(rr, 2 * h - 1) >= h
        j_first = jnp.bitwise_and(cc, 2 * h - 1) < h
        masks[h] = jnp.where(same, jnp.where(t_second, jnp.where(j_first, 1.0, 0.0), 0.0), 0.0)
    masks["diag"] = jnp.where(rr == cc, 1.0, 0.0)
    return masks


def _level_boundary(bc, h, chunk):
    pieces = []
    if 2 * h >= SUBLANES:
        for blk in range(chunk // (2 * h)):
            r = blk * 2 * h + h - 1
            pieces.append(jnp.broadcast_to(bc[r:r + 1, :], (2 * h, HEAD_W)))
    else:
        per = SUBLANES // (2 * h)
        sub_row = lax.broadcasted_iota(jnp.int32, (SUBLANES, HEAD_W), 0)
        for g8 in range(chunk // SUBLANES):
            val = None
            for k in range(per):
                r = g8 * SUBLANES + k * 2 * h + h - 1
                row = jnp.broadcast_to(bc[r:r + 1, :], (SUBLANES, HEAD_W))
                val = row if val is None else jnp.where(sub_row >= k * 2 * h, row, val)
            pieces.append(val)
    return pieces[0] if len(pieces) == 1 else jnp.concatenate(pieces, axis=0)


def _hgrn_near(b, f, q, k, v, masks, sub):
    n = b.shape[0]
    kb16 = k.astype(BF16)
    att = _dot_nt(q.astype(BF16), kb16) * masks["diag"]
    odd = jnp.bitwise_and(lax.broadcasted_iota(jnp.int32, (n, HEAD_W), 0), 1) == 1
    for h in _hgrn_levels(sub):
        if h == 1:
            att_h = _dot_nt((q * jnp.where(odd, f, 1.0)).astype(BF16), kb16)
        else:
            w = jnp.exp(-jnp.abs(b - _level_boundary(b, h, n)))
            att_h = _dot_nt((q * w).astype(BF16), (k * w).astype(BF16))
        att = att + att_h * masks[h]
    return _dot(att.astype(BF16), v.astype(BF16))


def _hgrn_chunk(bc, qc, kc, vc, st, mask_wide, chunk, sub):
    n_sub = chunk // sub
    o = _dot_nt((qc * jnp.exp(bc)).astype(BF16), st.astype(BF16))
    if n_sub > 1:
        vb = vc.astype(BF16)
        q_parts, k_parts = [], []
        for i in range(n_sub):
            lo, hi = i * sub, (i + 1) * sub
            anchor = bc[max(lo - 1, 0):max(lo - 1, 0) + 1, :]
            q_parts.append(qc[lo:hi, :] * jnp.exp(jnp.minimum(bc[lo:hi, :] - anchor, 0.0)))
            k_parts.append((kc * jnp.exp(jnp.minimum(anchor - bc, 0.0))).astype(BF16))
        qa = jnp.concatenate(q_parts, axis=0).astype(BF16)
        att_w = _dot_nt(qa, jnp.concatenate(k_parts, axis=0)) * mask_wide
        o = o + _dot(att_w.astype(BF16), jnp.concatenate([vb] * n_sub, axis=0))
    b_last = bc[chunk - 1:chunk, :]
    ke = kc * jnp.exp(b_last - bc)
    st_new = st * jnp.exp(b_last) + _dot_tn(vc.astype(BF16), ke.astype(BF16))
    return o, st_new


def _mix_in_kernel(*refs, nseq, rows, chunk, sub, per_seq_state, emit_attn_kv, q_scale):
    it = iter(refs)
    x_ref = next(it); mod_ref = next(it); nw_ref = next(it); win_ref = next(it); lb_ref = next(it)
    hgw_ref = next(it); qnw_ref = next(it); knw_ref = next(it)
    s0_ref = next(it) if per_seq_state else None
    ya_ref = next(it); q_ref = next(it); kf_ref = next(it); vf_ref = next(it); gb_ref = next(it)
    kb_ref = next(it) if emit_attn_kv else None
    vt_ref = next(it) if emit_attn_kv else None
    sfin_ref = next(it)
    st_scr = next(it); b_scr = next(it); q_scr = next(it); k_scr = next(it); v_scr = next(it); o_scr = next(it)
    f_scr = next(it)

    tm = nseq * rows
    d = x_ref.shape[-1]
    n_chunks = tm // chunk

    x3 = x_ref[...]
    ms = jnp.mean(x3 * x3, axis=-1, keepdims=True)
    shift = mod_ref[:, :, 0:d]
    scale = mod_ref[:, :, d:2 * d]
    h3 = x3 * lax.rsqrt(ms + EPS) * nw_ref[...] * (1.0 + scale) + shift
    h = h3.reshape(tm, d).astype(BF16)

    def proj(idx):
        return _dot(h, win_ref[:, idx * GROUP_W:(idx + 1) * GROUP_W])

    def group_b():
        gw = 2 * LANES
        gi = _div_pow2(lax.broadcasted_iota(jnp.int32, (gw, gw), 0), HALF)
        gj = _div_pow2(lax.broadcasted_iota(jnp.int32, (gw, gw), 1), HALF)
        gmat = jnp.where(gi == gj, 1.0 / HALF, 0.0).astype(BF16)

        qb = proj(5)
        qn = qb * lax.rsqrt(_group_mean_sq(qb, gmat) + EPS) * qnw_ref[...] * q_scale
        q_ref[...] = qn.reshape(nseq, rows, GROUP_W).astype(q_ref.dtype)
        kb = proj(6)
        kn = kb * lax.rsqrt(_group_mean_sq(kb, gmat) + EPS) * knw_ref[...]
        vb = proj(7)
        for hh in range(N_HEADS):
            hs = slice(hh * HEAD_W, (hh + 1) * HEAD_W)
            kf_ref[:, pl.ds(hh, rows, stride=N_HEADS), :] = kn[:, hs].reshape(nseq, rows, HEAD_W)
            vf_ref[:, pl.ds(hh, rows, stride=N_HEADS), :] = vb[:, hs].reshape(nseq, rows, HEAD_W)
        if emit_attn_kv:
            kb_ref[...] = kn.reshape(nseq, rows, GROUP_W).astype(BF16)
            for hh in range(N_HEADS):
                vt_ref[0, hh, 0, 0:HEAD_W, :] = vb[:, hh * HEAD_W:(hh + 1) * HEAD_W].T.astype(BF16)
                vt_ref[0, hh, 0, HEAD_W:VT_ROWS, :] = jnp.ones((VT_ROWS - HEAD_W, tm), BF16)
        zb = proj(8)
        sgz, _ = _sigmoid_pair(zb)
        gb_ref[...] = (zb * sgz).reshape(nseq, rows, GROUP_W)

    lb = lb_ref[...]
    fa = proj(1)
    sig, nsig = _sigmoid_pair(fa)
    f_gate = lb + (1.0 - lb) * sig
    f_scr[...] = f_gate
    logf = jnp.log(f_gate)
    k_scr[...] = (1.0 - lb) * nsig
    q_scr[...] = proj(0) * (HEAD_W ** -0.5)
    v_scr[...] = proj(2)
    ri = lax.broadcasted_iota(jnp.int32, (tm, tm), 0)
    ci = lax.broadcasted_iota(jnp.int32, (tm, tm), 1)
    same_chunk = _div_pow2(ri, chunk) == _div_pow2(ci, chunk)
    tri = jnp.where(ci <= ri, jnp.where(same_chunk, 1.0, 0.0), 0.0).astype(BF16)
    l1, l2 = _split2(logf)
    b_scr[...] = _dot(tri, l1) + _dot(tri, l2)

    if not per_seq_state:
        @pl.when(pl.program_id(1) == 0)
        def _():
            st_scr[...] = jnp.zeros_like(st_scr)

    near_rows = tm
    masks = _hgrn_masks(chunk, sub, near_rows)
    for hh in range(N_HEADS):
        hs = slice(hh * HEAD_W, (hh + 1) * HEAD_W)
        for r0 in range(0, tm, near_rows):
            nr = slice(r0, r0 + near_rows)
            o_scr[nr, hs] = _hgrn_near(b_scr[nr, hs], f_scr[nr, hs], q_scr[nr, hs], k_scr[nr, hs], v_scr[nr, hs],
                                       masks, sub)
        st = None if per_seq_state else st_scr[hh]
        for c in range(n_chunks):
            rs = slice(c * chunk, (c + 1) * chunk)
            if per_seq_state:
                st = s0_ref[c, hh].T
            o, st = _hgrn_chunk(b_scr[rs, hs], q_scr[rs, hs], k_scr[rs, hs], v_scr[rs, hs], st, masks["wide"],
                                chunk, sub)
            o_scr[rs, hs] = o_scr[rs, hs] + o
            if per_seq_state:
                sfin_ref[c, hh] = st.T
        if not per_seq_state:
            st_scr[hh] = st

    group_b()

    if not per_seq_state:
        @pl.when(pl.program_id(1) == pl.num_programs(1) - 1)
        def _():
            for hh in range(N_HEADS):
                sfin_ref[0, hh] = st_scr[hh].T

    sg, _ = _sigmoid_pair(proj(3))
    za = proj(4)
    sz, _ = _sigmoid_pair(za)
    outs = []
    for hh in range(N_HEADS):
        hs = slice(hh * HEAD_W, (hh + 1) * HEAD_W)
        oh = o_scr[:, hs] * sg[:, hs]
        mo = jnp.mean(oh * oh, axis=-1, keepdims=True)
        outs.append(oh * lax.rsqrt(mo + EPS))
    ya = jnp.concatenate(outs, axis=1) * hgw_ref[...] * (za * sz)
    ya_ref[...] = ya.reshape(nseq, rows, GROUP_W).astype(ya_ref.dtype)


def _mix_in(x, mod, norm_w, w_in, lb, hgw, qnw, knw, *, nseq, rows, chunk, sub, s0, attn_tile, q_dtype, q_scale):
    n_s, r_total, d = x.shape
    per_seq_state = s0 is not None
    emit = attn_tile is not None
    tm = nseq * rows
    if per_seq_state:
        assert rows == r_total and rows == chunk and n_s % nseq == 0
        grid = (n_s // nseq,)
        tok = lambda i: (i, 0, 0)
        seq = lambda i: (i, 0, 0)
        const2 = lambda i: (0, 0)
        st_map = lambda i: (i, 0, 0, 0)
        sem = ("arbitrary",)
    else:
        assert nseq == 1 and r_total % rows == 0 and rows % chunk == 0
        grid = (n_s, r_total // rows)
        tok = lambda b, t: (b, t, 0)
        seq = lambda b, t: (b, 0, 0)
        const2 = lambda b, t: (0, 0)
        st_map = lambda b, t: (b, 0, 0, 0)
        sem = ("parallel", "arbitrary")
    if emit:
        assert rows == attn_tile
    d_in = w_in.shape[1]
    tok_spec = pl.BlockSpec((nseq, rows, GROUP_W), tok)
    tokhead_spec = pl.BlockSpec((nseq, rows * N_HEADS, HEAD_W), tok)
    in_specs = [pl.BlockSpec((nseq, rows, d), tok),
                pl.BlockSpec((nseq, 1, 3 * d), seq),
                pl.BlockSpec((1, d), const2),
                pl.BlockSpec((d, d_in), const2),
                pl.BlockSpec((1, GROUP_W), const2),
                pl.BlockSpec((1, GROUP_W), const2),
                pl.BlockSpec((1, GROUP_W), const2),
                pl.BlockSpec((1, GROUP_W), const2)]
    args = [x, mod, norm_w, w_in, lb, hgw, qnw, knw]
    if per_seq_state:
        in_specs.append(pl.BlockSpec((nseq, N_HEADS, HEAD_W, HEAD_W), st_map))
        args.append(s0)
    out_shape = [jax.ShapeDtypeStruct((n_s, r_total, GROUP_W), q_dtype),
                 jax.ShapeDtypeStruct((n_s, r_total, GROUP_W), q_dtype),
                 jax.ShapeDtypeStruct((n_s, r_total * N_HEADS, HEAD_W), F32),
                 jax.ShapeDtypeStruct((n_s, r_total * N_HEADS, HEAD_W), F32),
                 jax.ShapeDtypeStruct((n_s, r_total, GROUP_W), F32)]
    out_specs = [tok_spec, tok_spec, tokhead_spec, tokhead_spec, tok_spec]
    if emit:
        nt = r_total // attn_tile
        out_shape += [jax.ShapeDtypeStruct((n_s, r_total, GROUP_W), BF16),
                      jax.ShapeDtypeStruct((n_s, N_HEADS, nt, VT_ROWS, attn_tile), BF16)]
        out_specs += [tok_spec,
                      pl.BlockSpec((1, N_HEADS, 1, VT_ROWS, attn_tile), lambda b, t: (b, 0, t, 0, 0))]
    out_shape.append(jax.ShapeDtypeStruct((n_s, N_HEADS, HEAD_W, HEAD_W), F32))
    out_specs.append(pl.BlockSpec((nseq, N_HEADS, HEAD_W, HEAD_W), st_map))
    scratch = [pltpu.VMEM((N_HEADS, HEAD_W, HEAD_W), F32)] + [pltpu.VMEM((tm, GROUP_W), F32)] * 6
    kern = functools.partial(_mix_in_kernel, nseq=nseq, rows=rows, chunk=chunk, sub=sub,
                             per_seq_state=per_seq_state, emit_attn_kv=emit, q_scale=q_scale)
    return pl.pallas_call(
        kern, out_shape=tuple(out_shape), grid=grid, in_specs=in_specs, out_specs=tuple(out_specs),
        scratch_shapes=scratch, compiler_params=_cparams(sem),
        name="mix_in_sample" if per_seq_state else "mix_in_prompt",
    )(*args)


def _sub_norm_gate(o, snw, gate):
    ms = jnp.mean(o * o, axis=-1, keepdims=True)
    return o * lax.rsqrt(ms + EPS) * snw * (1.0 - LAMBDA_INIT) * gate


def _attn_kernel(lam_ref, q_ref, k_ref, vt_ref, bias_ref, g_ref, snw_ref, o_ref,
                 qs_scr, sa_scr, sb_scr, m_scr, acc_scr, *, tq, tk, n_bias):
    qi = pl.program_id(2)
    ratio = tq // tk
    n_tiles = ratio * (qi + 1)
    q = q_ref[0].astype(F32)
    lane = lax.broadcasted_iota(jnp.int32, q.shape, 1)
    qs_scr[0:tq, :] = jnp.where(lane < HALF, q, 0.0).astype(BF16)
    qs_scr[tq:2 * tq, :] = jnp.where(lane >= HALF, q, 0.0).astype(BF16)
    m_scr[...] = jnp.full(m_scr.shape, NEG, F32)
    acc_scr[...] = jnp.zeros_like(acc_scr)

    def scores(t, s_ref):
        r0 = pl.multiple_of(t * tk, tk)
        kt = k_ref[0, pl.ds(r0, tk), :]
        bt = bias_ref[0, jnp.clip(ratio * qi - t + (ratio - 1), 0, n_bias - 1)]
        for c in range(2):
            s_ref[:, c * tq:(c + 1) * tq] = _dot_nt(kt, qs_scr[c * tq:(c + 1) * tq, :]) + bt

    def softmax_pv(t, s_ref):
        vt = vt_ref[0, 0, t]
        for c in range(2):
            cols = slice(c * tq, (c + 1) * tq)
            m_old = m_scr[:, cols]
            m_new = jnp.maximum(m_old, jnp.max(s_ref[:, cols], axis=0, keepdims=True))
            alpha = jnp.exp2(m_old - m_new)
            p = jnp.exp2(s_ref[:, cols] - m_new)
            acc_scr[:, cols] = acc_scr[:, cols] * alpha + _dot(vt, p.astype(BF16))
            m_scr[:, cols] = m_new

    scores(0, sa_scr)

    def pair(j):
        t0 = 2 * j
        scores(t0 + 1, sb_scr)
        softmax_pv(t0, sa_scr)
        scores(jnp.minimum(t0 + 2, n_tiles - 1), sa_scr)
        softmax_pv(t0 + 1, sb_scr)

    def body(jj, carry):
        pair(2 * jj)
        pair(2 * jj + 1)
        return carry

    n_pairs = n_tiles // 2
    lax.fori_loop(0, n_pairs // 2, body, 0)

    @pl.when(n_pairs % 2 == 1)
    def _():
        pair(n_pairs - 1)

    lam = lam_ref[0, 0]
    inv = 1.0 / acc_scr[HEAD_W:HEAD_W + 1, :]
    acc = acc_scr[0:HEAD_W, :]
    o_t = acc[:, 0:tq] * inv[:, 0:tq] - lam * (acc[:, tq:2 * tq] * inv[:, tq:2 * tq])
    o_ref[0] = _sub_norm_gate(o_t.T, snw_ref[...], g_ref[0]).astype(o_ref.dtype)


def _attn_prompt(lam, q, kb, vt, bias, gate, snw, tq, tk):
    b, l, _ = q.shape
    assert tq % (2 * tk) == 0 and l % tq == 0
    nq = l // tq
    nt = l // tk
    nd = bias.shape[1]
    return pl.pallas_call(
        functools.partial(_attn_kernel, tq=tq, tk=tk, n_bias=nd),
        out_shape=jax.ShapeDtypeStruct((b, l, GROUP_W), BF16),
        grid=(b, N_HEADS, nq),
        in_specs=[pl.BlockSpec(memory_space=pltpu.SMEM),
                  pl.BlockSpec((1, tq, HEAD_W), lambda bb, h, i: (bb, i, h)),
                  pl.BlockSpec((1, l, HEAD_W), lambda bb, h, i: (bb, 0, h)),
                  pl.BlockSpec((1, 1, nt, VT_ROWS, tk), lambda bb, h, i: (bb, h, 0, 0, 0)),
                  pl.BlockSpec((1, nd, tk, tq), lambda bb, h, i: (h, 0, 0, 0)),
                  pl.BlockSpec((1, tq, HEAD_W), lambda bb, h, i: (bb, i, h)),
                  pl.BlockSpec((1, HEAD_W), lambda bb, h, i: (0, 0))],
        out_specs=pl.BlockSpec((1, tq, HEAD_W), lambda bb, h, i: (bb, i, h)),
        scratch_shapes=[pltpu.VMEM((2 * tq, HEAD_W), BF16),
                        pltpu.VMEM((tk, 2 * tq), F32),
                        pltpu.VMEM((tk, 2 * tq), F32),
                        pltpu.VMEM((1, 2 * tq), F32),
                        pltpu.VMEM((VT_ROWS, 2 * tq), F32)],
        compiler_params=_cparams(("parallel", "parallel", "arbitrary")),
        name="attn_prompt",
    )(lam, q, kb, vt, bias, gate, snw)


def _decode_kernel(pt_ref, lam_ref, q_ref, kn_ref, vn_ref, g_ref, snw_ref, bias_ref, biasn_ref, ck_hbm, cv_hbm,
                   o_ref, kbuf, vbuf, sem, qst_scr, m_scr, l_scr, acc_scr, kpg_scr, vpg_scr,
                   *, grp, ring, rows, page, n_seq, groups_per_seq):
    hr = 2 * rows
    total_groups = n_seq * groups_per_seq

    def group_copies(gidx, slot0):
        b = _div_pow2(gidx, groups_per_seq)
        p0 = jnp.bitwise_and(gidx, groups_per_seq - 1) * grp
        copies = []
        for j in range(grp):
            pid = pt_ref[b, p0 + j]
            copies.append(pltpu.make_async_copy(ck_hbm.at[pid], kbuf.at[slot0 + j], sem.at[0, slot0 + j]))
            copies.append(pltpu.make_async_copy(cv_hbm.at[pid], vbuf.at[slot0 + j], sem.at[1, slot0 + j]))
        return copies

    def start_group(gidx, slot0):
        for cp in group_copies(gidx, slot0):
            cp.start()

    def wait_group(gidx, slot0):
        for cp in group_copies(gidx, slot0):
            cp.wait()

    def head_page(buf, slot, hh):
        return buf[slot, pl.ds(hh, page, stride=N_HEADS), :].astype(BF16)

    def init_sequence(b):
        q = q_ref[b].astype(F32)
        lane = lax.broadcasted_iota(jnp.int32, (rows, HEAD_W), 1)
        kpg_scr[...] = jnp.zeros_like(kpg_scr)
        vpg_scr[...] = jnp.zeros_like(vpg_scr)
        for hh in range(N_HEADS):
            qh = q[:, hh * HEAD_W:(hh + 1) * HEAD_W]
            qst_scr[hh] = jnp.concatenate([jnp.where(lane < HALF, qh, 0.0), jnp.where(lane >= HALF, qh, 0.0)],
                                          axis=0).astype(BF16)
            kpg_scr[hh, 0:rows, :] = kn_ref[b, pl.ds(hh, rows, stride=N_HEADS), :]
            vpg_scr[hh, 0:rows, :] = vn_ref[b, pl.ds(hh, rows, stride=N_HEADS), :]
        m_scr[...] = jnp.full(m_scr.shape, NEG, F32)
        l_scr[...] = jnp.zeros_like(l_scr)
        acc_scr[...] = jnp.zeros_like(acc_scr)

    def update(s, n_blocks, value_fn):
        m_old = m_scr[...]
        m_new = jnp.maximum(m_old, jnp.max(s, axis=-1, keepdims=True))
        alpha = jnp.exp(m_old - m_new)
        p = jnp.exp(s - m_new)
        l_scr[...] = alpha * l_scr[...] + jnp.sum(p, axis=-1, keepdims=True)
        pb = p.astype(BF16)
        pv_rows = []
        for hh in range(N_HEADS):
            ph = pb[hh * hr:(hh + 1) * hr, :]
            pv = _dot(ph[:, 0:page], value_fn(hh, 0))
            for j in range(1, n_blocks):
                pv = pv + _dot(ph[:, j * page:(j + 1) * page], value_fn(hh, j))
            pv_rows.append(pv)
        acc_scr[...] = acc_scr[...] * alpha + jnp.concatenate(pv_rows, axis=0)
        m_scr[...] = m_new

    def scores(n_blocks, key_fn):
        s_rows = []
        for hh in range(N_HEADS):
            qh = qst_scr[hh]
            blocks = [_dot_nt(qh, key_fn(hh, j)) for j in range(n_blocks)]
            s_rows.append(blocks[0] if n_blocks == 1 else jnp.concatenate(blocks, axis=1))
        return jnp.concatenate(s_rows, axis=0)

    def consume_group(gs, slot0):
        s = scores(grp, lambda hh, j: head_page(kbuf, slot0 + j, hh)) + bias_ref[gs]
        update(s, grp, lambda hh, j: head_page(vbuf, slot0 + j, hh))

    def finish_sequence(b):
        s_new = scores(1, lambda hh, j: kpg_scr[hh].astype(BF16)) + biasn_ref[...]
        update(s_new, 1, lambda hh, j: vpg_scr[hh].astype(BF16))
        lam = lam_ref[0, 0]
        inv = 1.0 / l_scr[...]
        acc = acc_scr[...] * inv
        gate = g_ref[b]
        for hh in range(N_HEADS):
            hs = slice(hh * HEAD_W, (hh + 1) * HEAD_W)
            r0 = hh * hr
            o = acc[r0:r0 + rows, :] - lam * acc[r0 + rows:r0 + 2 * rows, :]
            o_ref[b, :, hs] = _sub_norm_gate(o, snw_ref[...], gate[:, hs]).astype(o_ref.dtype)

    for part in range(ring):
        start_group(part, part * grp)

    def seq_body(b, carry):
        init_sequence(b)

        def ring_body(it, c):
            g0 = b * groups_per_seq + ring * it
            for part in range(ring):
                gidx = g0 + part
                slot0 = part * grp
                wait_group(gidx, slot0)
                consume_group(ring * it + part, slot0)

                @pl.when(gidx + ring < total_groups)
                def _():
                    start_group(gidx + ring, slot0)
            return c

        lax.fori_loop(0, groups_per_seq // ring, ring_body, 0)
        finish_sequence(b)
        return carry

    lax.fori_loop(0, n_seq, seq_body, 0)


def _attn_decode(page_table, lam, q, k_new, v_new, gate, snw, bias_dec, bias_new, cache_k, cache_v, n_pages):
    nb, rows, _ = q.shape
    pages_per_seq = page_table.shape[1]
    page = cache_k.shape[1] // N_HEADS
    grp = n_pages
    groups_per_seq = pages_per_seq // grp
    assert pages_per_seq % grp == 0 and page == LANES
    ring = min(DECODE_RING_GROUPS, groups_per_seq)
    assert groups_per_seq % ring == 0 and groups_per_seq & (groups_per_seq - 1) == 0
    assert bias_dec.shape == (groups_per_seq, 2 * N_HEADS * rows, grp * page)
    n_rows = 2 * N_HEADS * rows
    smem = pl.BlockSpec(memory_space=pltpu.SMEM)
    vmem = pl.BlockSpec(memory_space=pltpu.VMEM)
    hbm = pl.BlockSpec(memory_space=pl.ANY)
    return pl.pallas_call(
        functools.partial(_decode_kernel, grp=grp, ring=ring, rows=rows, page=page, n_seq=nb,
                          groups_per_seq=groups_per_seq),
        out_shape=jax.ShapeDtypeStruct((nb, rows, GROUP_W), F32),
        in_specs=[smem, smem, vmem, vmem, vmem, vmem, vmem, vmem, vmem, hbm, hbm],
        out_specs=vmem,
        scratch_shapes=[pltpu.VMEM((ring * grp, page * N_HEADS, HEAD_W), F32),
                        pltpu.VMEM((ring * grp, page * N_HEADS, HEAD_W), F32),
                        pltpu.SemaphoreType.DMA((2, ring * grp)),
                        pltpu.VMEM((N_HEADS, 2 * rows, HEAD_W), BF16),
                        pltpu.VMEM((n_rows, 1), F32),
                        pltpu.VMEM((n_rows, 1), F32),
                        pltpu.VMEM((n_rows, HEAD_W), F32),
                        pltpu.VMEM((N_HEADS, page, HEAD_W), F32),
                        pltpu.VMEM((N_HEADS, page, HEAD_W), F32)],
        compiler_params=pltpu.CompilerParams(vmem_limit_bytes=VMEM_LIMIT_BYTES),
        name="attn_decode",
    )(page_table, lam, q, k_new, v_new, gate, snw, bias_dec, bias_new, cache_k, cache_v)


def _mix_out_kernel(ya_ref, yb_ref, x_ref, mod_ref, w_ref, o_ref, *, nseq, rows):
    d = x_ref.shape[-1]
    tm = nseq * rows
    ya = ya_ref[...].reshape(tm, GROUP_W).astype(BF16)
    yb = yb_ref[...].reshape(tm, GROUP_W).astype(BF16)
    out = _dot(ya, w_ref[0:GROUP_W, :]) + _dot(yb, w_ref[GROUP_W:2 * GROUP_W, :])
    gate = mod_ref[:, :, 2 * d:3 * d]
    o_ref[...] = x_ref[...] + gate * out.reshape(nseq, rows, d)


def _mix_out(ya, yb, x, mod, w_out, *, nseq, rows):
    n_s, r_total, d = x.shape
    grid = (n_s // nseq, r_total // rows)
    tok = lambda s, t: (s, t, 0)
    return pl.pallas_call(
        functools.partial(_mix_out_kernel, nseq=nseq, rows=rows),
        out_shape=jax.ShapeDtypeStruct(x.shape, F32),
        grid=grid,
        in_specs=[pl.BlockSpec((nseq, rows, GROUP_W), tok),
                  pl.BlockSpec((nseq, rows, GROUP_W), tok),
                  pl.BlockSpec((nseq, rows, d), tok),
                  pl.BlockSpec((nseq, 1, 3 * d), lambda s, t: (s, 0, 0)),
                  pl.BlockSpec((2 * GROUP_W, d), lambda s, t: (0, 0))],
        out_specs=pl.BlockSpec((nseq, rows, d), tok),
        compiler_params=_cparams(("parallel", "arbitrary")),
        name="mix_out",
    )(ya, yb, x, mod, w_out)


def _largest_tile(n, cap):
    t = cap
    while n % t:
        t //= 2
    return t


def kernel(x_prompt, x_sample, c_prompt, c_sample, cache_k, cache_v, state_hgrn, page_table, norm_w, w_ada,
           b_ada, w_in, w_out, lb_param, hg_norm_w, q_norm_w, k_norm_w, lam_q1, lam_k1, lam_q2, lam_k2,
           sub_norm_w, rel_table):
    depth = w_in.shape[0]
    assert depth == 1
    b, l, d = x_prompt.shape
    db, dl, _ = x_sample.shape
    assert d == 2 * GROUP_W and w_in.shape[2] == 9 * GROUP_W
    page = cache_k.shape[2]
    past = page_table.shape[1] * page
    assert dl % SUBLANES == 0 and cache_k.shape[3] * cache_k.shape[4] == GROUP_W

    w_in_b = w_in[0].astype(BF16)
    w_out_b = w_out[0].astype(BF16)
    nw = norm_w.reshape(1, d)
    hgw = jnp.tile(hg_norm_w.reshape(1, HEAD_W), (1, N_HEADS))
    qnw = jnp.tile(q_norm_w.reshape(1, HALF), (1, GROUP_W // HALF))
    knw = jnp.tile(k_norm_w.reshape(1, HALF), (1, GROUP_W // HALF))
    snw = sub_norm_w.reshape(1, HEAD_W)

    mod = _adaln_mod(jnp.concatenate([c_prompt, c_sample], axis=0), w_ada[0], b_ada[0])
    mod_p = mod[:b].reshape(b, 1, 3 * d)
    mod_s = mod[b:].reshape(db, 1, 3 * d)
    lb, lam = _small_params(lb_param, lam_q1, lam_k1, lam_q2, lam_k2)

    log2e = math.log2(math.e)
    ya, q, kf, vf, gb, kb, vt, st_p = _mix_in(
        x_prompt, mod_p, nw, w_in_b, lb, hgw, qnw, knw, nseq=1, rows=ATTN_TK,
        chunk=CHUNK if l % CHUNK == 0 else l, sub=SUB, s0=None, attn_tile=ATTN_TK, q_dtype=BF16,
        q_scale=HALF ** -0.5 * log2e)
    bias_p = _bias_prompt(rel_table, ATTN_TK, ATTN_TQ, log2e)
    yb = _attn_prompt(lam, q, kb, vt, bias_p, gb, snw, ATTN_TQ, ATTN_TK)
    y_prompt = _mix_out(ya, yb, x_prompt, mod_p, w_out_b, nseq=1, rows=_largest_tile(l, 512))

    nseq = _largest_tile(db, 8)
    ya_s, q_s, kf_s, vf_s, gb_s, st_s = _mix_in(
        x_sample, mod_s, nw, w_in_b, lb, hgw, qnw, knw, nseq=nseq, rows=dl,
        chunk=dl, sub=dl, s0=state_hgrn[0], attn_tile=None, q_dtype=F32, q_scale=HALF ** -0.5)
    grp = _largest_tile(page_table.shape[1] // 2, DECODE_GROUP_PAGES)
    bias_d, bias_n = _bias_decode(rel_table, dl, past, grp * page)
    n_rows = 2 * N_HEADS * dl
    yb_s = _attn_decode(page_table, lam, q_s, kf_s, vf_s, gb_s, snw,
                        bias_d.reshape(-1, n_rows, grp * page), bias_n.reshape(n_rows, LANES),
                        cache_k.reshape(-1, page * N_HEADS, HEAD_W), cache_v.reshape(-1, page * N_HEADS, HEAD_W),
                        n_pages=grp)
    y_sample = _mix_out(ya_s, yb_s, x_sample, mod_s, w_out_b, nseq=nseq, rows=dl)

    hb = N_HEADS
    return (y_prompt, y_sample,
            kf.reshape(1, b, l, hb, HEAD_W), vf.reshape(1, b, l, hb, HEAD_W), st_p[None],
            kf_s.reshape(1, db, dl, hb, HEAD_W), vf_s.reshape(1, db, dl, hb, HEAD_W), st_s[None])
```

```python
import functools
import math

import jax
import jax.numpy as jnp
from jax import lax
from jax.experimental import pallas as pl
from jax.experimental.pallas import tpu as pltpu

F32 = jnp.float32
BF16 = jnp.bfloat16

N_HEADS = 4
HEAD_W = 128
HALF = 64
GROUP_W = N_HEADS * HEAD_W
N_BUCKETS = 32
MAX_EXACT = N_BUCKETS // 2
MAX_DIST = 1024
CHUNK = 64
SUB = 16
EPS = 1e-6
LAMBDA_INIT = 0.8 - 0.6 * math.exp(-0.3 * 0)
NEG = -1e30
ATTN_TQ = 512
ATTN_TK = 256
MIX_ROWS = 512
MIX_BLOCK_ROWS = 256
ATTN_HEADS_PER_STEP = 1
VT_ROWS = HEAD_W + 16
DECODE_GROUP_PAGES = 8
DECODE_RING_GROUPS = 4

LANES = 128
SUBLANES = 8
VMEM_LIMIT_BYTES = 56 * 1024 * 1024


def _bucket_lower_bounds():
    los = list(range(MAX_EXACT))
    span = N_BUCKETS - MAX_EXACT
    ratio = MAX_DIST // MAX_EXACT
    for m in range(MAX_EXACT, N_BUCKETS):
        d = MAX_EXACT
        while d ** span < (MAX_EXACT ** span) * (ratio ** (m - MAX_EXACT)):
            d += 1
        los.append(d)
    return los


BUCKET_LO = _bucket_lower_bounds()
SATURATED_DIST = BUCKET_LO[-1]


def _cparams(sem):
    return pltpu.CompilerParams(dimension_semantics=sem, vmem_limit_bytes=VMEM_LIMIT_BYTES)


def _split2(x):
    hi = x.astype(BF16)
    lo = (x - hi.astype(F32)).astype(BF16)
    return hi, lo


def _split3(x):
    hi = x.astype(BF16)
    r = x - hi.astype(F32)
    mid = r.astype(BF16)
    lo = (r - mid.astype(F32)).astype(BF16)
    return hi, mid, lo


def _dot(a, b):
    return jnp.dot(a, b, preferred_element_type=F32)


def _dot_nt(a, b):
    return lax.dot_general(a, b, (((1,), (1,)), ((), ())), preferred_element_type=F32)


def _dot_tn(a, b):
    return lax.dot_general(a, b, (((0,), (0,)), ((), ())), preferred_element_type=F32)


def _div_pow2(x, n):
    assert n > 0 and n & (n - 1) == 0
    return lax.shift_right_logical(x, n.bit_length() - 1)


def _sigmoid_pair(x):
    t = jnp.exp(-jnp.abs(x))
    r = 1.0 / (1.0 + t)
    tr = t * r
    pos = x >= 0
    return jnp.where(pos, r, tr), jnp.where(pos, tr, r)


def _mod_kernel(c_ref, w_ref, b_ref, o_ref):
    c = c_ref[...]
    sig, _ = _sigmoid_pair(c)
    s = c * sig
    w = w_ref[...]
    s1, s2 = _split2(s)
    w1, w2 = _split2(w)
    o_ref[...] = _dot(s1, w1) + (_dot(s1, w2) + _dot(s2, w1)) + b_ref[...]


def _adaln_mod(c_all, w_ada, b_ada):
    n, d = c_all.shape
    d3 = w_ada.shape[1]
    bn = d
    return pl.pallas_call(
        _mod_kernel,
        out_shape=jax.ShapeDtypeStruct((n, d3), F32),
        grid=(d3 // bn,),
        in_specs=[pl.BlockSpec((n, d), lambda j: (0, 0)),
                  pl.BlockSpec((d, bn), lambda j: (0, j)),
                  pl.BlockSpec((1, bn), lambda j: (0, j))],
        out_specs=pl.BlockSpec((n, bn), lambda j: (0, j)),
        compiler_params=_cparams(("arbitrary",)),
        name="adaln_mod",
    )(c_all, w_ada, b_ada.reshape(1, d3))


def _small_kernel(lbp_ref, q1_ref, k1_ref, q2_ref, k2_ref, lb_ref, lam_ref):
    p = lbp_ref[...]
    m = jnp.max(p, axis=0, keepdims=True)
    e = jnp.exp(p - m)
    lb_ref[...] = e[0:1, :] / jnp.sum(e, axis=0, keepdims=True)
    s1 = jnp.sum(q1_ref[...] * k1_ref[...], axis=-1, keepdims=True)
    s2 = jnp.sum(q2_ref[...] * k2_ref[...], axis=-1, keepdims=True)
    lam_ref[...] = jnp.exp(s1) - jnp.exp(s2) + LAMBDA_INIT


def _small_params(lb_param, lq1, lk1, lq2, lk2):
    return pl.pallas_call(
        _small_kernel,
        out_shape=(jax.ShapeDtypeStruct((1, lb_param.shape[1]), F32),
                   jax.ShapeDtypeStruct((1, 1), F32)),
        name="small_params",
    )(lb_param, lq1, lk1, lq2, lk2)


def _bias_select(d, tab_ref, h):
    val = jnp.full(d.shape, tab_ref[N_BUCKETS - 1, h], F32)
    for m in range(N_BUCKETS - 2, -1, -1):
        val = jnp.where(d < BUCKET_LO[m + 1], tab_ref[m, h], val)
    return val


def _bias_prompt_kernel(tab_ref, o_ref, *, tk, tq, scale):
    h = pl.program_id(0)
    dl = pl.program_id(1) - (tq // tk - 1)
    kk = lax.broadcasted_iota(jnp.int32, (tk, tq), 0)
    qq = lax.broadcasted_iota(jnp.int32, (tk, tq), 1)
    d = dl * tk + qq - kk
    val = _bias_select(jnp.maximum(d, 0), tab_ref, h) * scale
    o_ref[0, 0] = jnp.where(d >= 0, val, NEG)


def _n_bias_tiles(tk, tq):
    n_const = 0
    while n_const * tk - (tk - 1) < SATURATED_DIST:
        n_const += 1
    return n_const + 1 + (tq // tk - 1)


def _bias_prompt(rel_table, tk, tq, scale):
    nd = _n_bias_tiles(tk, tq)
    return pl.pallas_call(
        functools.partial(_bias_prompt_kernel, tk=tk, tq=tq, scale=scale),
        out_shape=jax.ShapeDtypeStruct((N_HEADS, nd, tk, tq), F32),
        grid=(N_HEADS, nd),
        in_specs=[pl.BlockSpec(memory_space=pltpu.SMEM)],
        out_specs=pl.BlockSpec((1, 1, tk, tq), lambda h, d: (h, d, 0, 0)),
        compiler_params=_cparams(("arbitrary", "arbitrary")),
        name="bias_prompt",
    )(rel_table)


def _bias_decode_kernel(tab_ref, o_ref, on_ref, *, rows, past, width):
    h = pl.program_id(0)
    g = pl.program_id(1)
    i = lax.broadcasted_iota(jnp.int32, (rows, width), 0)
    kpos = g * width + lax.broadcasted_iota(jnp.int32, (rows, width), 1)
    val = _bias_select(past + i - kpos, tab_ref, h)
    o_ref[0, 0, 0] = val
    o_ref[0, 0, 1] = val
    i2 = lax.broadcasted_iota(jnp.int32, (rows, LANES), 0)
    j2 = lax.broadcasted_iota(jnp.int32, (rows, LANES), 1)
    d2 = i2 - j2
    vn = jnp.where(d2 >= 0, _bias_select(jnp.maximum(d2, 0), tab_ref, h), NEG)
    on_ref[0, 0] = vn
    on_ref[0, 1] = vn


def _bias_decode(rel_table, rows, past, width):
    ng = past // width
    return pl.pallas_call(
        functools.partial(_bias_decode_kernel, rows=rows, past=past, width=width),
        out_shape=(jax.ShapeDtypeStruct((ng, N_HEADS, 2, rows, width), F32),
                   jax.ShapeDtypeStruct((N_HEADS, 2, rows, LANES), F32)),
        grid=(N_HEADS, ng),
        in_specs=[pl.BlockSpec(memory_space=pltpu.SMEM)],
        out_specs=(pl.BlockSpec((1, 1, 2, rows, width), lambda h, g: (g, h, 0, 0, 0)),
                   pl.BlockSpec((1, 2, rows, LANES), lambda h, g: (h, 0, 0, 0))),
        compiler_params=_cparams(("arbitrary", "arbitrary")),
        name="bias_decode",
    )(rel_table)


def _group_mean_sq(x, gmat):
    sq = x * x
    outs = []
    w = gmat.shape[0]
    for c0 in range(0, x.shape[1], w):
        outs.append(_dot(sq[:, c0:c0 + w].astype(BF16), gmat))
    return jnp.concatenate(outs, axis=1)


def _hgrn_levels(sub):
    levels, h = [], sub // 2
    while h >= 1:
        levels.append(h)
        h //= 2
    return levels


def _hgrn_masks(chunk, sub, n):
    n_sub = chunk // sub
    masks = {"wide": None}
    if n_sub > 1:
        rr = lax.broadcasted_iota(jnp.int32, (chunk, n_sub * chunk), 0)
        cc = lax.broadcasted_iota(jnp.int32, (chunk, n_sub * chunk), 1)
        own_anchor = _div_pow2(cc, chunk) == _div_pow2(rr, sub)
        earlier = _div_pow2(jnp.bitwise_and(cc, chunk - 1), sub) < _div_pow2(rr, sub)
        masks["wide"] = jnp.where(own_anchor, jnp.where(earlier, 1.0, 0.0), 0.0)
    rr = lax.broadcasted_iota(jnp.int32, (n, n), 0)
    cc = lax.broadcasted_iota(jnp.int32, (n, n), 1)
    for h in _hgrn_levels(sub):
        same = _div_pow2(rr, 2 * h) == _div_pow2(cc, 2 * h)
        t_second = jnp.bitwise_and(rr, 2 * h - 1) >= h
        j_first = jnp.bitwise_and(cc, 2 * h - 1) < h
        masks[h] = jnp.where(same, jnp.where(t_second, jnp.where(j_first, 1.0, 0.0), 0.0), 0.0)
    masks["diag"] = jnp.where(rr == cc, 1.0, 0.0)
    return masks


def _level_boundary(bc, h, chunk):
    pieces = []
    if 2 * h >= SUBLANES:
        for blk in range(chunk // (2 * h)):
            r = blk * 2 * h + h - 1
            pieces.append(jnp.broadcast_to(bc[r:r + 1, :], (2 * h, HEAD_W)))
    else:
        per = SUBLANES // (2 * h)
        sub_row = lax.broadcasted_iota(jnp.int32, (SUBLANES, HEAD_W), 0)
        for g8 in range(chunk // SUBLANES):
            val = None
            for k in range(per):
                r = g8 * SUBLANES + k * 2 * h + h - 1
                row = jnp.broadcast_to(bc[r:r + 1, :], (SUBLANES, HEAD_W))
                val = row if val is None else jnp.where(sub_row >= k * 2 * h, row, val)
            pieces.append(val)
    return pieces[0] if len(pieces) == 1 else jnp.concatenate(pieces, axis=0)


def _hgrn_near(b, f, q, k, v, masks, sub):
    n = b.shape[0]
    kb16 = k.astype(BF16)
    att = _dot_nt(q.astype(BF16), kb16) * masks["diag"]
    odd = jnp.bitwise_and(lax.broadcasted_iota(jnp.int32, (n, HEAD_W), 0), 1) == 1
    for h in _hgrn_levels(sub):
        if h == 1:
            att_h = _dot_nt((q * jnp.where(odd, f, 1.0)).astype(BF16), kb16)
        else:
            w = jnp.exp(-jnp.abs(b - _level_boundary(b, h, n)))
            att_h = _dot_nt((q * w).astype(BF16), (k * w).astype(BF16))
        att = att + att_h * masks[h]
    return _dot(att.astype(BF16), v.astype(BF16))


def _hgrn_chunk(bc, qc, kc, vc, st, mask_wide, chunk, sub):
    n_sub = chunk // sub
    o = _dot_nt((qc * jnp.exp(bc)).astype(BF16), st.astype(BF16))
    if n_sub > 1:
        vb = vc.astype(BF16)
        q_parts, k_parts = [], []
        for i in range(n_sub):
            lo, hi = i * sub, (i + 1) * sub
            anchor = bc[max(lo - 1, 0):max(lo - 1, 0) + 1, :]
            q_parts.append(qc[lo:hi, :] * jnp.exp(jnp.minimum(bc[lo:hi, :] - anchor, 0.0)))
            k_parts.append((kc * jnp.exp(jnp.minimum(anchor - bc, 0.0))).astype(BF16))
        qa = jnp.concatenate(q_parts, axis=0).astype(BF16)
        att_w = _dot_nt(qa, jnp.concatenate(k_parts, axis=0)) * mask_wide
        o = o + _dot(att_w.astype(BF16), jnp.concatenate([vb] * n_sub, axis=0))
    b_last = bc[chunk - 1:chunk, :]
    ke = kc * jnp.exp(b_last - bc)
    st_new = st * jnp.exp(b_last) + _dot_tn(vc.astype(BF16), ke.astype(BF16))
    return o, st_new


def _mix_in_kernel(*refs, nseq, rows, chunk, sub, per_seq_state, emit_attn_kv, q_scale):
    it = iter(refs)
    x_ref = next(it); mod_ref = next(it); nw_ref = next(it); win_ref = next(it); lb_ref = next(it)
    hgw_ref = next(it); qnw_ref = next(it); knw_ref = next(it)
    s0_ref = next(it) if per_seq_state else None
    ya_ref = next(it); q_ref = next(it); kf_ref = next(it); vf_ref = next(it); gb_ref = next(it)
    kb_ref = next(it) if emit_attn_kv else None
    vt_ref = next(it) if emit_attn_kv else None
    sfin_ref = next(it)
    st_scr = next(it); b_scr = next(it); q_scr = next(it); k_scr = next(it); v_scr = next(it); o_scr = next(it)
    f_scr = next(it)

    tm = nseq * rows
    d = x_ref.shape[-1]
    n_chunks = tm // chunk
    blk = min(tm, MIX_BLOCK_ROWS)
    assert tm % blk == 0 and blk % chunk == 0

    x3 = x_ref[...]
    ms = jnp.mean(x3 * x3, axis=-1, keepdims=True)
    shift = mod_ref[:, :, 0:d]
    scale = mod_ref[:, :, d:2 * d]
    h3 = x3 * lax.rsqrt(ms + EPS) * nw_ref[...] * (1.0 + scale) + shift
    h = h3.reshape(tm, d).astype(BF16)

    def proj(idx):
        return _dot(h, win_ref[:, idx * GROUP_W:(idx + 1) * GROUP_W])

    def group_b():
        gw = 2 * LANES
        gi = _div_pow2(lax.broadcasted_iota(jnp.int32, (gw, gw), 0), HALF)
        gj = _div_pow2(lax.broadcasted_iota(jnp.int32, (gw, gw), 1), HALF)
        gmat = jnp.where(gi == gj, 1.0 / HALF, 0.0).astype(BF16)

        qb = proj(5)
        qn = qb * lax.rsqrt(_group_mean_sq(qb, gmat) + EPS) * qnw_ref[...] * q_scale
        q_ref[...] = qn.reshape(nseq, rows, GROUP_W).astype(q_ref.dtype)
        kb = proj(6)
        kn = kb * lax.rsqrt(_group_mean_sq(kb, gmat) + EPS) * knw_ref[...]
        vb = proj(7)
        for hh in range(N_HEADS):
            hs = slice(hh * HEAD_W, (hh + 1) * HEAD_W)
            kf_ref[:, pl.ds(hh, rows, stride=N_HEADS), :] = kn[:, hs].reshape(nseq, rows, HEAD_W)
            vf_ref[:, pl.ds(hh, rows, stride=N_HEADS), :] = vb[:, hs].reshape(nseq, rows, HEAD_W)
        if emit_attn_kv:
            kb_ref[...] = kn.reshape(nseq, rows, GROUP_W).astype(BF16)
            vt_tile = vt_ref.shape[-1]
            for hh in range(N_HEADS):
                for j in range(tm // vt_tile):
                    vt_ref[0, hh, j, 0:HEAD_W, :] = (
                        vb[j * vt_tile:(j + 1) * vt_tile, hh * HEAD_W:(hh + 1) * HEAD_W].T.astype(BF16))
                    vt_ref[0, hh, j, HEAD_W:VT_ROWS, :] = jnp.ones((VT_ROWS - HEAD_W, vt_tile), BF16)
        zb = proj(8)
        sgz, _ = _sigmoid_pair(zb)
        gb_ref[...] = (zb * sgz).reshape(nseq, rows, GROUP_W)

    lb = lb_ref[...]
    fa = proj(1)
    sig, nsig = _sigmoid_pair(fa)
    f_gate = lb + (1.0 - lb) * sig
    f_scr[...] = f_gate
    logf = jnp.log(f_gate)
    k_scr[...] = (1.0 - lb) * nsig
    q_scr[...] = proj(0) * (HEAD_W ** -0.5)
    v_scr[...] = proj(2)
    ri = lax.broadcasted_iota(jnp.int32, (blk, blk), 0)
    ci = lax.broadcasted_iota(jnp.int32, (blk, blk), 1)
    same_chunk = _div_pow2(ri, chunk) == _div_pow2(ci, chunk)
    tri = jnp.where(ci <= ri, jnp.where(same_chunk, 1.0, 0.0), 0.0).astype(BF16)
    l1, l2 = _split2(logf)
    for r0 in range(0, tm, blk):
        b_scr[r0:r0 + blk, :] = _dot(tri, l1[r0:r0 + blk, :]) + _dot(tri, l2[r0:r0 + blk, :])

    if not per_seq_state:
        @pl.when(pl.program_id(1) == 0)
        def _():
            st_scr[...] = jnp.zeros_like(st_scr)

    near_rows = blk
    masks = _hgrn_masks(chunk, sub, near_rows)
    for hh in range(N_HEADS):
        hs = slice(hh * HEAD_W, (hh + 1) * HEAD_W)
        for r0 in range(0, tm, near_rows):
            nr = slice(r0, r0 + near_rows)
            o_scr[nr, hs] = _hgrn_near(b_scr[nr, hs], f_scr[nr, hs], q_scr[nr, hs], k_scr[nr, hs], v_scr[nr, hs],
                                       masks, sub)
        st = None if per_seq_state else st_scr[hh]
        for c in range(n_chunks):
            rs = slice(c * chunk, (c + 1) * chunk)
            if per_seq_state:
                st = s0_ref[c, hh].T
            o, st = _hgrn_chunk(b_scr[rs, hs], q_scr[rs, hs], k_scr[rs, hs], v_scr[rs, hs], st, masks["wide"],
                                chunk, sub)
            o_scr[rs, hs] = o_scr[rs, hs] + o
            if per_seq_state:
                sfin_ref[c, hh] = st.T
        if not per_seq_state:
            st_scr[hh] = st

    group_b()

    if not per_seq_state:
        @pl.when(pl.program_id(1) == pl.num_programs(1) - 1)
        def _():
            for hh in range(N_HEADS):
                sfin_ref[0, hh] = st_scr[hh].T

    sg, _ = _sigmoid_pair(proj(3))
    za = proj(4)
    sz, _ = _sigmoid_pair(za)
    outs = []
    for hh in range(N_HEADS):
        hs = slice(hh * HEAD_W, (hh + 1) * HEAD_W)
        oh = o_scr[:, hs] * sg[:, hs]
        mo = jnp.mean(oh * oh, axis=-1, keepdims=True)
        outs.append(oh * lax.rsqrt(mo + EPS))
    ya = jnp.concatenate(outs, axis=1) * hgw_ref[...] * (za * sz)
    ya_ref[...] = ya.reshape(nseq, rows, GROUP_W).astype(ya_ref.dtype)


def _mix_in(x, mod, norm_w, w_in, lb, hgw, qnw, knw, *, nseq, rows, chunk, sub, s0, attn_tile, q_dtype, q_scale):
    n_s, r_total, d = x.shape
    per_seq_state = s0 is not None
    emit = attn_tile is not None
    tm = nseq * rows
    if per_seq_state:
        assert rows == r_total and rows == chunk and n_s % nseq == 0
        grid = (n_s // nseq,)
        tok = lambda i: (i, 0, 0)
        seq = lambda i: (i, 0, 0)
        const2 = lambda i: (0, 0)
        st_map = lambda i: (i, 0, 0, 0)
        sem = ("arbitrary",)
    else:
        assert nseq == 1 and r_total % rows == 0 and rows % chunk == 0
        grid = (n_s, r_total // rows)
        tok = lambda b, t: (b, t, 0)
        seq = lambda b, t: (b, 0, 0)
        const2 = lambda b, t: (0, 0)
        st_map = lambda b, t: (b, 0, 0, 0)
        sem = ("parallel", "arbitrary")
    if emit:
        assert rows % attn_tile == 0
    d_in = w_in.shape[1]
    tok_spec = pl.BlockSpec((nseq, rows, GROUP_W), tok)
    tokhead_spec = pl.BlockSpec((nseq, rows * N_HEADS, HEAD_W), tok)
    in_specs = [pl.BlockSpec((nseq, rows, d), tok),
                pl.BlockSpec((nseq, 1, 3 * d), seq),
                pl.BlockSpec((1, d), const2),
                pl.BlockSpec((d, d_in), const2),
                pl.BlockSpec((1, GROUP_W), const2),
                pl.BlockSpec((1, GROUP_W), const2),
                pl.BlockSpec((1, GROUP_W), const2),
                pl.BlockSpec((1, GROUP_W), const2)]
    args = [x, mod, norm_w, w_in, lb, hgw, qnw, knw]
    if per_seq_state:
        in_specs.append(pl.BlockSpec((nseq, N_HEADS, HEAD_W, HEAD_W), st_map))
        args.append(s0)
    out_shape = [jax.ShapeDtypeStruct((n_s, r_total, GROUP_W), q_dtype),
                 jax.ShapeDtypeStruct((n_s, r_total, GROUP_W), q_dtype),
                 jax.ShapeDtypeStruct((n_s, r_total * N_HEADS, HEAD_W), F32),
                 jax.ShapeDtypeStruct((n_s, r_total * N_HEADS, HEAD_W), F32),
                 jax.ShapeDtypeStruct((n_s, r_total, GROUP_W), F32)]
    out_specs = [tok_spec, tok_spec, tokhead_spec, tokhead_spec, tok_spec]
    if emit:
        nt = r_total // attn_tile
        out_shape += [jax.ShapeDtypeStruct((n_s, r_total, GROUP_W), BF16),
                      jax.ShapeDtypeStruct((n_s, N_HEADS, nt, VT_ROWS, attn_tile), BF16)]
        out_specs += [tok_spec,
                      pl.BlockSpec((1, N_HEADS, rows // attn_tile, VT_ROWS, attn_tile),
                                   lambda b, t: (b, 0, t, 0, 0))]
    out_shape.append(jax.ShapeDtypeStruct((n_s, N_HEADS, HEAD_W, HEAD_W), F32))
    out_specs.append(pl.BlockSpec((nseq, N_HEADS, HEAD_W, HEAD_W), st_map))
    scratch = [pltpu.VMEM((N_HEADS, HEAD_W, HEAD_W), F32)] + [pltpu.VMEM((tm, GROUP_W), F32)] * 6
    kern = functools.partial(_mix_in_kernel, nseq=nseq, rows=rows, chunk=chunk, sub=sub,
                             per_seq_state=per_seq_state, emit_attn_kv=emit, q_scale=q_scale)
    return pl.pallas_call(
        kern, out_shape=tuple(out_shape), grid=grid, in_specs=in_specs, out_specs=tuple(out_specs),
        scratch_shapes=scratch, compiler_params=_cparams(sem),
        name="mix_in_sample" if per_seq_state else "mix_in_prompt",
    )(*args)


def _sub_norm_gate(o, snw, gate):
    ms = jnp.mean(o * o, axis=-1, keepdims=True)
    return o * lax.rsqrt(ms + EPS) * snw * (1.0 - LAMBDA_INIT) * gate


def _attn_kernel(lam_ref, q_ref, k_ref, vt_ref, bias_ref, g_ref, snw_ref, o_ref,
                 qs_scr, sa_scr, sb_scr, m_scr, acc_scr, *, tq, tk, n_bias):
    qi = pl.program_id(2)
    ratio = tq // tk
    n_tiles = ratio * (qi + 1)
    n_heads = qs_scr.shape[0]
    heads = range(n_heads)
    lane = lax.broadcasted_iota(jnp.int32, (tq, HEAD_W), 1)
    for hh in heads:
        q = q_ref[0, :, hh * HEAD_W:(hh + 1) * HEAD_W].astype(F32)
        qs_scr[hh, 0:tq, :] = jnp.where(lane < HALF, q, 0.0).astype(BF16)
        qs_scr[hh, tq:2 * tq, :] = jnp.where(lane >= HALF, q, 0.0).astype(BF16)
    m_scr[...] = jnp.full(m_scr.shape, NEG, F32)
    acc_scr[...] = jnp.zeros_like(acc_scr)

    def scores(hh, t, s_ref):
        r0 = pl.multiple_of(t * tk, tk)
        kt = k_ref[0, pl.ds(r0, tk), hh * HEAD_W:(hh + 1) * HEAD_W]
        bt = bias_ref[hh, jnp.clip(ratio * qi - t + (ratio - 1), 0, n_bias - 1)]
        for c in range(2):
            s_ref[hh, :, c * tq:(c + 1) * tq] = _dot_nt(kt, qs_scr[hh, c * tq:(c + 1) * tq, :]) + bt

    def softmax_pv(hh, t, s_ref):
        vt = vt_ref[0, hh, t]
        for c in range(2):
            cols = slice(c * tq, (c + 1) * tq)
            m_old = m_scr[hh, :, cols]
            m_new = jnp.maximum(m_old, jnp.max(s_ref[hh, :, cols], axis=0, keepdims=True))
            alpha = jnp.exp2(m_old - m_new)
            p = jnp.exp2(s_ref[hh, :, cols] - m_new)
            acc_scr[hh, :, cols] = acc_scr[hh, :, cols] * alpha + _dot(vt, p.astype(BF16))
            m_scr[hh, :, cols] = m_new

    for hh in heads:
        scores(hh, 0, sa_scr)

    def pair(j):
        t0 = 2 * j
        for hh in heads:
            scores(hh, t0 + 1, sb_scr)
        for hh in heads:
            softmax_pv(hh, t0, sa_scr)
        for hh in heads:
            scores(hh, jnp.minimum(t0 + 2, n_tiles - 1), sa_scr)
        for hh in heads:
            softmax_pv(hh, t0 + 1, sb_scr)

    def body(jj, carry):
        pair(2 * jj)
        pair(2 * jj + 1)
        return carry

    n_pairs = n_tiles // 2
    lax.fori_loop(0, n_pairs // 2, body, 0)

    @pl.when(n_pairs % 2 == 1)
    def _():
        pair(n_pairs - 1)

    lam = lam_ref[0, 0]
    for hh in heads:
        hs = slice(hh * HEAD_W, (hh + 1) * HEAD_W)
        inv = 1.0 / acc_scr[hh, HEAD_W:HEAD_W + 1, :]
        acc = acc_scr[hh, 0:HEAD_W, :]
        o_t = acc[:, 0:tq] * inv[:, 0:tq] - lam * (acc[:, tq:2 * tq] * inv[:, tq:2 * tq])
        o_ref[0, :, hs] = _sub_norm_gate(o_t.T, snw_ref[...], g_ref[0, :, hs]).astype(o_ref.dtype)


def _attn_prompt(lam, q, kb, vt, bias, gate, snw, tq, tk):
    b, l, _ = q.shape
    assert tq % (2 * tk) == 0 and l % tq == 0
    nq = l // tq
    nt = l // tk
    nd = bias.shape[1]
    hp = ATTN_HEADS_PER_STEP
    assert N_HEADS % hp == 0
    return pl.pallas_call(
        functools.partial(_attn_kernel, tq=tq, tk=tk, n_bias=nd),
        out_shape=jax.ShapeDtypeStruct((b, l, GROUP_W), BF16),
        grid=(b, N_HEADS // hp, nq),
        in_specs=[pl.BlockSpec(memory_space=pltpu.SMEM),
                  pl.BlockSpec((1, tq, hp * HEAD_W), lambda bb, h, i: (bb, i, h)),
                  pl.BlockSpec((1, l, hp * HEAD_W), lambda bb, h, i: (bb, 0, h)),
                  pl.BlockSpec((1, hp, nt, VT_ROWS, tk), lambda bb, h, i: (bb, h, 0, 0, 0)),
                  pl.BlockSpec((hp, nd, tk, tq), lambda bb, h, i: (h, 0, 0, 0)),
                  pl.BlockSpec((1, tq, hp * HEAD_W), lambda bb, h, i: (bb, i, h)),
                  pl.BlockSpec((1, HEAD_W), lambda bb, h, i: (0, 0))],
        out_specs=pl.BlockSpec((1, tq, hp * HEAD_W), lambda bb, h, i: (bb, i, h)),
        scratch_shapes=[pltpu.VMEM((hp, 2 * tq, HEAD_W), BF16),
                        pltpu.VMEM((hp, tk, 2 * tq), F32),
                        pltpu.VMEM((hp, tk, 2 * tq), F32),
                        pltpu.VMEM((hp, 1, 2 * tq), F32),
                        pltpu.VMEM((hp, VT_ROWS, 2 * tq), F32)],
        compiler_params=_cparams(("parallel", "parallel", "arbitrary")),
        name="attn_prompt",
    )(lam, q, kb, vt, bias, gate, snw)


def _decode_kernel(pt_ref, lam_ref, q_ref, kn_ref, vn_ref, g_ref, snw_ref, bias_ref, biasn_ref, ck_hbm, cv_hbm,
                   o_ref, kbuf, vbuf, sem, qst_scr, m_scr, l_scr, acc_scr, kpg_scr, vpg_scr,
                   *, grp, ring, rows, page, n_seq, groups_per_seq):
    hr = 2 * rows
    total_groups = n_seq * groups_per_seq

    def group_copies(gidx, slot0):
        b = _div_pow2(gidx, groups_per_seq)
        p0 = jnp.bitwise_and(gidx, groups_per_seq - 1) * grp
        copies = []
        for j in range(grp):
            pid = pt_ref[b, p0 + j]
            copies.append(pltpu.make_async_copy(ck_hbm.at[pid], kbuf.at[slot0 + j], sem.at[0, slot0 + j]))
            copies.append(pltpu.make_async_copy(cv_hbm.at[pid], vbuf.at[slot0 + j], sem.at[1, slot0 + j]))
        return copies

    def start_group(gidx, slot0):
        for cp in group_copies(gidx, slot0):
            cp.start()

    def wait_group(gidx, slot0):
        for cp in group_copies(gidx, slot0):
            cp.wait()

    def head_page(buf, slot, hh):
        return buf[slot, pl.ds(hh, page, stride=N_HEADS), :].astype(BF16)

    def init_sequence(b):
        q = q_ref[b].astype(F32)
        lane = lax.broadcasted_iota(jnp.int32, (rows, HEAD_W), 1)
        kpg_scr[...] = jnp.zeros_like(kpg_scr)
        vpg_scr[...] = jnp.zeros_like(vpg_scr)
        for hh in range(N_HEADS):
            qh = q[:, hh * HEAD_W:(hh + 1) * HEAD_W]
            qst_scr[hh] = jnp.concatenate([jnp.where(lane < HALF, qh, 0.0), jnp.where(lane >= HALF, qh, 0.0)],
                                          axis=0).astype(BF16)
            kpg_scr[hh, 0:rows, :] = kn_ref[b, pl.ds(hh, rows, stride=N_HEADS), :]
            vpg_scr[hh, 0:rows, :] = vn_ref[b, pl.ds(hh, rows, stride=N_HEADS), :]
        m_scr[...] = jnp.full(m_scr.shape, NEG, F32)
        l_scr[...] = jnp.zeros_like(l_scr)
        acc_scr[...] = jnp.zeros_like(acc_scr)

    def update(s, n_blocks, value_fn):
        m_old = m_scr[...]
        m_new = jnp.maximum(m_old, jnp.max(s, axis=-1, keepdims=True))
        alpha = jnp.exp(m_old - m_new)
        p = jnp.exp(s - m_new)
        l_scr[...] = alpha * l_scr[...] + jnp.sum(p, axis=-1, keepdims=True)
        pb = p.astype(BF16)
        pv_rows = []
        for hh in range(N_HEADS):
            ph = pb[hh * hr:(hh + 1) * hr, :]
            pv = _dot(ph[:, 0:page], value_fn(hh, 0))
            for j in range(1, n_blocks):
                pv = pv + _dot(ph[:, j * page:(j + 1) * page], value_fn(hh, j))
            pv_rows.append(pv)
        acc_scr[...] = acc_scr[...] * alpha + jnp.concatenate(pv_rows, axis=0)
        m_scr[...] = m_new

    def scores(n_blocks, key_fn):
        s_rows = []
        for hh in range(N_HEADS):
            qh = qst_scr[hh]
            blocks = [_dot_nt(qh, key_fn(hh, j)) for j in range(n_blocks)]
            s_rows.append(blocks[0] if n_blocks == 1 else jnp.concatenate(blocks, axis=1))
        return jnp.concatenate(s_rows, axis=0)

    def consume_group(gs, slot0):
        s = scores(grp, lambda hh, j: head_page(kbuf, slot0 + j, hh)) + bias_ref[gs]
        update(s, grp, lambda hh, j: head_page(vbuf, slot0 + j, hh))

    def finish_sequence(b):
        s_new = scores(1, lambda hh, j: kpg_scr[hh].astype(BF16)) + biasn_ref[...]
        update(s_new, 1, lambda hh, j: vpg_scr[hh].astype(BF16))
        lam = lam_ref[0, 0]
        inv = 1.0 / l_scr[...]
        acc = acc_scr[...] * inv
        gate = g_ref[b]
        for hh in range(N_HEADS):
            hs = slice(hh * HEAD_W, (hh + 1) * HEAD_W)
            r0 = hh * hr
            o = acc[r0:r0 + rows, :] - lam * acc[r0 + rows:r0 + 2 * rows, :]
            o_ref[b, :, hs] = _sub_norm_gate(o, snw_ref[...], gate[:, hs]).astype(o_ref.dtype)

    for part in range(ring):
        start_group(part, part * grp)

    def seq_body(b, carry):
        init_sequence(b)

        def ring_body(it, c):
            g0 = b * groups_per_seq + ring * it
            for part in range(ring):
                gidx = g0 + part
                slot0 = part * grp
                wait_group(gidx, slot0)
                consume_group(ring * it + part, slot0)

                @pl.when(gidx + ring < total_groups)
                def _():
                    start_group(gidx + ring, slot0)
            return c

        lax.fori_loop(0, groups_per_seq // ring, ring_body, 0)
        finish_sequence(b)
        return carry

    lax.fori_loop(0, n_seq, seq_body, 0)


def _attn_decode(page_table, lam, q, k_new, v_new, gate, snw, bias_dec, bias_new, cache_k, cache_v, n_pages):
    nb, rows, _ = q.shape
    pages_per_seq = page_table.shape[1]
    page = cache_k.shape[1] // N_HEADS
    grp = n_pages
    groups_per_seq = pages_per_seq // grp
    assert pages_per_seq % grp == 0 and page == LANES
    ring = min(DECODE_RING_GROUPS, groups_per_seq)
    assert groups_per_seq % ring == 0 and groups_per_seq & (groups_per_seq - 1) == 0
    assert bias_dec.shape == (groups_per_seq, 2 * N_HEADS * rows, grp * page)
    n_rows = 2 * N_HEADS * rows
    smem = pl.BlockSpec(memory_space=pltpu.SMEM)
    vmem = pl.BlockSpec(memory_space=pltpu.VMEM)
    hbm = pl.BlockSpec(memory_space=pl.ANY)
    return pl.pallas_call(
        functools.partial(_decode_kernel, grp=grp, ring=ring, rows=rows, page=page, n_seq=nb,
                          groups_per_seq=groups_per_seq),
        out_shape=jax.ShapeDtypeStruct((nb, rows, GROUP_W), F32),
        in_specs=[smem, smem, vmem, vmem, vmem, vmem, vmem, vmem, vmem, hbm, hbm],
        out_specs=vmem,
        scratch_shapes=[pltpu.VMEM((ring * grp, page * N_HEADS, HEAD_W), F32),
                        pltpu.VMEM((ring * grp, page * N_HEADS, HEAD_W), F32),
                        pltpu.SemaphoreType.DMA((2, ring * grp)),
                        pltpu.VMEM((N_HEADS, 2 * rows, HEAD_W), BF16),
                        pltpu.VMEM((n_rows, 1), F32),
                        pltpu.VMEM((n_rows, 1), F32),
                        pltpu.VMEM((n_rows, HEAD_W), F32),
                        pltpu.VMEM((N_HEADS, page, HEAD_W), F32),
                        pltpu.VMEM((N_HEADS, page, HEAD_W), F32)],
        compiler_params=pltpu.CompilerParams(vmem_limit_bytes=VMEM_LIMIT_BYTES),
        name="attn_decode",
    )(page_table, lam, q, k_new, v_new, gate, snw, bias_dec, bias_new, cache_k, cache_v)


def _mix_out_kernel(ya_ref, yb_ref, x_ref, mod_ref, w_ref, o_ref, *, nseq, rows):
    d = x_ref.shape[-1]
    tm = nseq * rows
    ya = ya_ref[...].reshape(tm, GROUP_W).astype(BF16)
    yb = yb_ref[...].reshape(tm, GROUP_W).astype(BF16)
    out = _dot(ya, w_ref[0:GROUP_W, :]) + _dot(yb, w_ref[GROUP_W:2 * GROUP_W, :])
    gate = mod_ref[:, :, 2 * d:3 * d]
    o_ref[...] = x_ref[...] + gate * out.reshape(nseq, rows, d)


def _mix_out(ya, yb, x, mod, w_out, *, nseq, rows):
    n_s, r_total, d = x.shape
    grid = (n_s // nseq, r_total // rows)
    tok = lambda s, t: (s, t, 0)
    return pl.pallas_call(
        functools.partial(_mix_out_kernel, nseq=nseq, rows=rows),
        out_shape=jax.ShapeDtypeStruct(x.shape, F32),
        grid=grid,
        in_specs=[pl.BlockSpec((nseq, rows, GROUP_W), tok),
                  pl.BlockSpec((nseq, rows, GROUP_W), tok),
                  pl.BlockSpec((nseq, rows, d), tok),
                  pl.BlockSpec((nseq, 1, 3 * d), lambda s, t: (s, 0, 0)),
                  pl.BlockSpec((2 * GROUP_W, d), lambda s, t: (0, 0))],
        out_specs=pl.BlockSpec((nseq, rows, d), tok),
        compiler_params=_cparams(("parallel", "arbitrary")),
        name="mix_out",
    )(ya, yb, x, mod, w_out)


def _largest_tile(n, cap):
    t = cap
    while n % t:
        t //= 2
    return t


def kernel(x_prompt, x_sample, c_prompt, c_sample, cache_k, cache_v, state_hgrn, page_table, norm_w, w_ada,
           b_ada, w_in, w_out, lb_param, hg_norm_w, q_norm_w, k_norm_w, lam_q1, lam_k1, lam_q2, lam_k2,
           sub_norm_w, rel_table):
    depth = w_in.shape[0]
    assert depth == 1
    b, l, d = x_prompt.shape
    db, dl, _ = x_sample.shape
    assert d == 2 * GROUP_W and w_in.shape[2] == 9 * GROUP_W
    page = cache_k.shape[2]
    past = page_table.shape[1] * page
    assert dl % SUBLANES == 0 and cache_k.shape[3] * cache_k.shape[4] == GROUP_W

    w_in_b = w_in[0].astype(BF16)
    w_out_b = w_out[0].astype(BF16)
    nw = norm_w.reshape(1, d)
    hgw = jnp.tile(hg_norm_w.reshape(1, HEAD_W), (1, N_HEADS))
    qnw = jnp.tile(q_norm_w.reshape(1, HALF), (1, GROUP_W // HALF))
    knw = jnp.tile(k_norm_w.reshape(1, HALF), (1, GROUP_W // HALF))
    snw = sub_norm_w.reshape(1, HEAD_W)

    mod = _adaln_mod(jnp.concatenate([c_prompt, c_sample], axis=0), w_ada[0], b_ada[0])
    mod_p = mod[:b].reshape(b, 1, 3 * d)
    mod_s = mod[b:].reshape(db, 1, 3 * d)
    lb, lam = _small_params(lb_param, lam_q1, lam_k1, lam_q2, lam_k2)

    log2e = math.log2(math.e)
    ya, q, kf, vf, gb, kb, vt, st_p = _mix_in(
        x_prompt, mod_p, nw, w_in_b, lb, hgw, qnw, knw, nseq=1, rows=_largest_tile(l, MIX_ROWS),
        chunk=CHUNK if l % CHUNK == 0 else l, sub=SUB, s0=None, attn_tile=ATTN_TK, q_dtype=BF16,
        q_scale=HALF ** -0.5 * log2e)
    bias_p = _bias_prompt(rel_table, ATTN_TK, ATTN_TQ, log2e)
    yb = _attn_prompt(lam, q, kb, vt, bias_p, gb, snw, ATTN_TQ, ATTN_TK)
    y_prompt = _mix_out(ya, yb, x_prompt, mod_p, w_out_b, nseq=1, rows=_largest_tile(l, 512))

    nseq = _largest_tile(db, 8)
    ya_s, q_s, kf_s, vf_s, gb_s, st_s = _mix_in(
        x_sample, mod_s, nw, w_in_b, lb, hgw, qnw, knw, nseq=nseq, rows=dl,
        chunk=dl, sub=dl, s0=state_hgrn[0], attn_tile=None, q_dtype=F32, q_scale=HALF ** -0.5)
    grp = _largest_tile(page_table.shape[1] // 2, DECODE_GROUP_PAGES)
    bias_d, bias_n = _bias_decode(rel_table, dl, past, grp * page)
    n_rows = 2 * N_HEADS * dl
    yb_s = _attn_decode(page_table, lam, q_s, kf_s, vf_s, gb_s, snw,
                        bias_d.reshape(-1, n_rows, grp * page), bias_n.reshape(n_rows, LANES),
                        cache_k.reshape(-1, page * N_HEADS, HEAD_W), cache_v.reshape(-1, page * N_HEADS, HEAD_W),
                        n_pages=grp)
    y_sample = _mix_out(ya_s, yb_s, x_sample, mod_s, w_out_b, nseq=nseq, rows=dl)

    hb = N_HEADS
    return (y_prompt, y_sample,
            kf.reshape(1, b, l, hb, HEAD_W), vf.reshape(1, b, l, hb, HEAD_W), st_p[None],
            kf_s.reshape(1, db, dl, hb, HEAD_W), vf_s.reshape(1, db, dl, hb, HEAD_W), st_s[None])
```

```python
import functools
import math

import jax
import jax.numpy as jnp
from jax import lax
from jax.experimental import pallas as pl
from jax.experimental.pallas import tpu as pltpu

F32 = jnp.float32
BF16 = jnp.bfloat16

N_HEADS = 4
HEAD_W = 128
HALF = 64
GROUP_W = N_HEADS * HEAD_W
N_BUCKETS = 32
MAX_EXACT = N_BUCKETS // 2
MAX_DIST = 1024
CHUNK = 64
SUB = 16
EPS = 1e-6
LAMBDA_INIT = 0.8 - 0.6 * math.exp(-0.3 * 0)
NEG = -1e30
ATTN_TQ = 512
ATTN_TK = 256
MIX_ROWS = 512
MIX_BLOCK_ROWS = 256
ATTN_HEADS_PER_STEP = 1
VT_ROWS = HEAD_W + 16
DECODE_GROUP_PAGES = 8
DECODE_RING_GROUPS = 4

LANES = 128
SUBLANES = 8
VMEM_LIMIT_BYTES = 56 * 1024 * 1024


def _bucket_lower_bounds():
    los = list(range(MAX_EXACT))
    span = N_BUCKETS - MAX_EXACT
    ratio = MAX_DIST // MAX_EXACT
    for m in range(MAX_EXACT, N_BUCKETS):
        d = MAX_EXACT
        while d ** span < (MAX_EXACT ** span) * (ratio ** (m - MAX_EXACT)):
            d += 1
        los.append(d)
    return los


BUCKET_LO = _bucket_lower_bounds()
SATURATED_DIST = BUCKET_LO[-1]


def _cparams(sem):
    return pltpu.CompilerParams(dimension_semantics=sem, vmem_limit_bytes=VMEM_LIMIT_BYTES)


def _split2(x):
    hi = x.astype(BF16)
    lo = (x - hi.astype(F32)).astype(BF16)
    return hi, lo


def _split3(x):
    hi = x.astype(BF16)
    r = x - hi.astype(F32)
    mid = r.astype(BF16)
    lo = (r - mid.astype(F32)).astype(BF16)
    return hi, mid, lo


def _dot(a, b):
    return jnp.dot(a, b, preferred_element_type=F32)


def _dot_nt(a, b):
    return lax.dot_general(a, b, (((1,), (1,)), ((), ())), preferred_element_type=F32)


def _dot_tn(a, b):
    return lax.dot_general(a, b, (((0,), (0,)), ((), ())), preferred_element_type=F32)


def _div_pow2(x, n):
    assert n > 0 and n & (n - 1) == 0
    return lax.shift_right_logical(x, n.bit_length() - 1)


def _sigmoid_pair(x):
    t = jnp.exp(-jnp.abs(x))
    r = 1.0 / (1.0 + t)
    tr = t * r
    pos = x >= 0
    return jnp.where(pos, r, tr), jnp.where(pos, tr, r)


def _mod_kernel(c_ref, w_ref, b_ref, o_ref):
    c = c_ref[...]
    sig, _ = _sigmoid_pair(c)
    s = c * sig
    w = w_ref[...]
    s1, s2 = _split2(s)
    w1, w2 = _split2(w)
    o_ref[...] = _dot(s1, w1) + (_dot(s1, w2) + _dot(s2, w1)) + b_ref[...]


def _adaln_mod(c_all, w_ada, b_ada):
    n, d = c_all.shape
    d3 = w_ada.shape[1]
    bn = d
    return pl.pallas_call(
        _mod_kernel,
        out_shape=jax.ShapeDtypeStruct((n, d3), F32),
        grid=(d3 // bn,),
        in_specs=[pl.BlockSpec((n, d), lambda j: (0, 0)),
                  pl.BlockSpec((d, bn), lambda j: (0, j)),
                  pl.BlockSpec((1, bn), lambda j: (0, j))],
        out_specs=pl.BlockSpec((n, bn), lambda j: (0, j)),
        compiler_params=_cparams(("arbitrary",)),
        name="adaln_mod",
    )(c_all, w_ada, b_ada.reshape(1, d3))


def _small_kernel(lbp_ref, q1_ref, k1_ref, q2_ref, k2_ref, lb_ref, lam_ref):
    p = lbp_ref[...]
    m = jnp.max(p, axis=0, keepdims=True)
    e = jnp.exp(p - m)
    lb_ref[...] = e[0:1, :] / jnp.sum(e, axis=0, keepdims=True)
    s1 = jnp.sum(q1_ref[...] * k1_ref[...], axis=-1, keepdims=True)
    s2 = jnp.sum(q2_ref[...] * k2_ref[...], axis=-1, keepdims=True)
    lam_ref[...] = jnp.exp(s1) - jnp.exp(s2) + LAMBDA_INIT


def _small_params(lb_param, lq1, lk1, lq2, lk2):
    return pl.pallas_call(
        _small_kernel,
        out_shape=(jax.ShapeDtypeStruct((1, lb_param.shape[1]), F32),
                   jax.ShapeDtypeStruct((1, 1), F32)),
        name="small_params",
    )(lb_param, lq1, lk1, lq2, lk2)


def _bucket_of(d):
    return max(m for m in range(N_BUCKETS) if d >= BUCKET_LO[m])


def _bias_select(d, tab_ref, h, d_min, d_max):
    m_lo, m_hi = _bucket_of(d_min), _bucket_of(d_max)
    val = jnp.full(d.shape, tab_ref[m_hi, h], F32)
    for m in range(m_hi - 1, m_lo - 1, -1):
        val = jnp.where(d < BUCKET_LO[m + 1], tab_ref[m, h], val)
    return val


def _bias_prompt_kernel(tab_ref, o_ref, *, tk, tq, scale):
    h = pl.program_id(0)
    kk = lax.broadcasted_iota(jnp.int32, (tk, tq), 0)
    qq = lax.broadcasted_iota(jnp.int32, (tk, tq), 1)
    for idx in range(o_ref.shape[1]):
        dl = idx - (tq // tk - 1)
        d = dl * tk + qq - kk
        d_min, d_max = dl * tk - (tk - 1), dl * tk + tq - 1
        val = _bias_select(jnp.maximum(d, 0), tab_ref, h, max(d_min, 0), max(d_max, 0)) * scale
        o_ref[0, idx] = jnp.where(d >= 0, val, NEG) if d_min < 0 else val


def _n_bias_tiles(tk, tq):
    n_const = 0
    while n_const * tk - (tk - 1) < SATURATED_DIST:
        n_const += 1
    return n_const + 1 + (tq // tk - 1)


def _bias_prompt(rel_table, tk, tq, scale):
    nd = _n_bias_tiles(tk, tq)
    return pl.pallas_call(
        functools.partial(_bias_prompt_kernel, tk=tk, tq=tq, scale=scale),
        out_shape=jax.ShapeDtypeStruct((N_HEADS, nd, tk, tq), F32),
        grid=(N_HEADS,),
        in_specs=[pl.BlockSpec(memory_space=pltpu.SMEM)],
        out_specs=pl.BlockSpec((1, nd, tk, tq), lambda h: (h, 0, 0, 0)),
        compiler_params=_cparams(("arbitrary",)),
        name="bias_prompt",
    )(rel_table)


def _bias_decode_kernel(tab_ref, o_ref, on_ref, *, rows, past, width):
    h = pl.program_id(0)
    i = lax.broadcasted_iota(jnp.int32, (rows, width), 0)
    lane = lax.broadcasted_iota(jnp.int32, (rows, width), 1)
    for g in range(past // width):
        d_min, d_max = past - (g + 1) * width + 1, past - g * width + rows - 1
        val = _bias_select(past + i - (g * width + lane), tab_ref, h, d_min, d_max)
        o_ref[g, 0, 0] = val
        o_ref[g, 0, 1] = val
    i2 = lax.broadcasted_iota(jnp.int32, (rows, LANES), 0)
    j2 = lax.broadcasted_iota(jnp.int32, (rows, LANES), 1)
    d2 = i2 - j2
    vn = jnp.where(d2 >= 0, _bias_select(jnp.maximum(d2, 0), tab_ref, h, 0, rows - 1), NEG)
    on_ref[0, 0] = vn
    on_ref[0, 1] = vn


def _bias_decode(rel_table, rows, past, width):
    ng = past // width
    return pl.pallas_call(
        functools.partial(_bias_decode_kernel, rows=rows, past=past, width=width),
        out_shape=(jax.ShapeDtypeStruct((ng, N_HEADS, 2, rows, width), F32),
                   jax.ShapeDtypeStruct((N_HEADS, 2, rows, LANES), F32)),
        grid=(N_HEADS,),
        in_specs=[pl.BlockSpec(memory_space=pltpu.SMEM)],
        out_specs=(pl.BlockSpec((ng, 1, 2, rows, width), lambda h: (0, h, 0, 0, 0)),
                   pl.BlockSpec((1, 2, rows, LANES), lambda h: (h, 0, 0, 0))),
        compiler_params=_cparams(("arbitrary",)),
        name="bias_decode",
    )(rel_table)


def _group_mean_sq(x, gmat):
    sq = x * x
    outs = []
    w = gmat.shape[0]
    for c0 in range(0, x.shape[1], w):
        outs.append(_dot(sq[:, c0:c0 + w].astype(BF16), gmat))
    return jnp.concatenate(outs, axis=1)


def _hgrn_levels(sub):
    levels, h = [], sub // 2
    while h >= 1:
        levels.append(h)
        h //= 2
    return levels


def _hgrn_masks(chunk, sub, n):
    n_sub = chunk // sub
    masks = {"wide": None}
    if n_sub > 1:
        rr = lax.broadcasted_iota(jnp.int32, (chunk, n_sub * chunk), 0)
        cc = lax.broadcasted_iota(jnp.int32, (chunk, n_sub * chunk), 1)
        own_anchor = _div_pow2(cc, chunk) == _div_pow2(rr, sub)
        earlier = _div_pow2(jnp.bitwise_and(cc, chunk - 1), sub) < _div_pow2(rr, sub)
        masks["wide"] = jnp.where(own_anchor, jnp.where(earlier, 1.0, 0.0), 0.0)
    rr = lax.broadcasted_iota(jnp.int32, (n, n), 0)
    cc = lax.broadcasted_iota(jnp.int32, (n, n), 1)
    for h in _hgrn_levels(sub):
        same = _div_pow2(rr, 2 * h) == _div_pow2(cc, 2 * h)
        t_second = jnp.bitwise_and(rr, 2 * h - 1) >= h
        j_first = jnp.bitwise_and(cc, 2 * h - 1) < h
        masks[h] = jnp.where(same, jnp.where(t_second, jnp.where(j_first, 1.0, 0.0), 0.0), 0.0)
    masks["diag"] = jnp.where(rr == cc, 1.0, 0.0)
    return masks


def _level_boundary(bc, h, chunk):
    pieces = []
    if 2 * h >= SUBLANES:
        for blk in range(chunk // (2 * h)):
            r = blk * 2 * h + h - 1
            pieces.append(jnp.broadcast_to(bc[r:r + 1, :], (2 * h, HEAD_W)))
    else:
        per = SUBLANES // (2 * h)
        sub_row = lax.broadcasted_iota(jnp.int32, (SUBLANES, HEAD_W), 0)
        for g8 in range(chunk // SUBLANES):
            val = None
            for k in range(per):
                r = g8 * SUBLANES + k * 2 * h + h - 1
                row = jnp.broadcast_to(bc[r:r + 1, :], (SUBLANES, HEAD_W))
                val = row if val is None else jnp.where(sub_row >= k * 2 * h, row, val)
            pieces.append(val)
    return pieces[0] if len(pieces) == 1 else jnp.concatenate(pieces, axis=0)


def _hgrn_near(b, f, q, k, v, masks, sub):
    n = b.shape[0]
    kb16 = k.astype(BF16)
    att = _dot_nt(q.astype(BF16), kb16) * masks["diag"]
    odd = jnp.bitwise_and(lax.broadcasted_iota(jnp.int32, (n, HEAD_W), 0), 1) == 1
    for h in _hgrn_levels(sub):
        if h == 1:
            att_h = _dot_nt((q * jnp.where(odd, f, 1.0)).astype(BF16), kb16)
        else:
            w = jnp.exp(-jnp.abs(b - _level_boundary(b, h, n)))
            att_h = _dot_nt((q * w).astype(BF16), (k * w).astype(BF16))
        att = att + att_h * masks[h]
    return _dot(att.astype(BF16), v.astype(BF16))


def _hgrn_chunk(bc, qc, kc, vc, st, mask_wide, chunk, sub):
    n_sub = chunk // sub
    o = _dot_nt((qc * jnp.exp(bc)).astype(BF16), st.astype(BF16))
    if n_sub > 1:
        vb = vc.astype(BF16)
        q_parts, k_parts = [], []
        for i in range(n_sub):
            lo, hi = i * sub, (i + 1) * sub
            anchor = bc[max(lo - 1, 0):max(lo - 1, 0) + 1, :]
            q_parts.append(qc[lo:hi, :] * jnp.exp(jnp.minimum(bc[lo:hi, :] - anchor, 0.0)))
            k_parts.append((kc * jnp.exp(jnp.minimum(anchor - bc, 0.0))).astype(BF16))
        qa = jnp.concatenate(q_parts, axis=0).astype(BF16)
        att_w = _dot_nt(qa, jnp.concatenate(k_parts, axis=0)) * mask_wide
        o = o + _dot(att_w.astype(BF16), jnp.concatenate([vb] * n_sub, axis=0))
    b_last = bc[chunk - 1:chunk, :]
    ke = kc * jnp.exp(b_last - bc)
    st_new = st * jnp.exp(b_last) + _dot_tn(vc.astype(BF16), ke.astype(BF16))
    return o, st_new


def _mix_in_kernel(*refs, nseq, rows, chunk, sub, per_seq_state, emit_attn_kv, q_scale):
    it = iter(refs)
    x_ref = next(it); mod_ref = next(it); nw_ref = next(it); win_ref = next(it); lb_ref = next(it)
    hgw_ref = next(it); qnw_ref = next(it); knw_ref = next(it)
    s0_ref = next(it) if per_seq_state else None
    ya_ref = next(it); q_ref = next(it); kf_ref = next(it); vf_ref = next(it); gb_ref = next(it)
    kb_ref = next(it) if emit_attn_kv else None
    vt_ref = next(it) if emit_attn_kv else None
    sfin_ref = next(it)
    st_scr = next(it); b_scr = next(it); q_scr = next(it); k_scr = next(it); v_scr = next(it); o_scr = next(it)
    f_scr = next(it)

    tm = nseq * rows
    d = x_ref.shape[-1]
    n_chunks = tm // chunk
    blk = min(tm, MIX_BLOCK_ROWS)
    assert tm % blk == 0 and blk % chunk == 0

    x3 = x_ref[...]
    ms = jnp.mean(x3 * x3, axis=-1, keepdims=True)
    shift = mod_ref[:, :, 0:d]
    scale = mod_ref[:, :, d:2 * d]
    h3 = x3 * lax.rsqrt(ms + EPS) * nw_ref[...] * (1.0 + scale) + shift
    h = h3.reshape(tm, d).astype(BF16)

    def proj(idx):
        return _dot(h, win_ref[:, idx * GROUP_W:(idx + 1) * GROUP_W])

    def group_b():
        gw = 2 * LANES
        gi = _div_pow2(lax.broadcasted_iota(jnp.int32, (gw, gw), 0), HALF)
        gj = _div_pow2(lax.broadcasted_iota(jnp.int32, (gw, gw), 1), HALF)
        gmat = jnp.where(gi == gj, 1.0 / HALF, 0.0).astype(BF16)

        qb = proj(5)
        qn = qb * lax.rsqrt(_group_mean_sq(qb, gmat) + EPS) * qnw_ref[...] * q_scale
        q_ref[...] = qn.reshape(nseq, rows, GROUP_W).astype(q_ref.dtype)
        kb = proj(6)
        kn = kb * lax.rsqrt(_group_mean_sq(kb, gmat) + EPS) * knw_ref[...]
        vb = proj(7)
        for hh in range(N_HEADS):
            hs = slice(hh * HEAD_W, (hh + 1) * HEAD_W)
            kf_ref[:, pl.ds(hh, rows, stride=N_HEADS), :] = kn[:, hs].reshape(nseq, rows, HEAD_W)
            vf_ref[:, pl.ds(hh, rows, stride=N_HEADS), :] = vb[:, hs].reshape(nseq, rows, HEAD_W)
        if emit_attn_kv:
            kb_ref[...] = kn.reshape(nseq, rows, GROUP_W).astype(BF16)
            vt_tile = vt_ref.shape[-1]
            for hh in range(N_HEADS):
                for j in range(tm // vt_tile):
                    vt_ref[0, hh, j, 0:HEAD_W, :] = (
                        vb[j * vt_tile:(j + 1) * vt_tile, hh * HEAD_W:(hh + 1) * HEAD_W].T.astype(BF16))
                    vt_ref[0, hh, j, HEAD_W:VT_ROWS, :] = jnp.ones((VT_ROWS - HEAD_W, vt_tile), BF16)
        zb = proj(8)
        sgz, _ = _sigmoid_pair(zb)
        gb_ref[...] = (zb * sgz).reshape(nseq, rows, GROUP_W)

    lb = lb_ref[...]
    fa = proj(1)
    sig, nsig = _sigmoid_pair(fa)
    f_gate = lb + (1.0 - lb) * sig
    f_scr[...] = f_gate
    logf = jnp.log(f_gate)
    k_scr[...] = (1.0 - lb) * nsig
    q_scr[...] = proj(0) * (HEAD_W ** -0.5)
    v_scr[...] = proj(2)
    ri = lax.broadcasted_iota(jnp.int32, (blk, blk), 0)
    ci = lax.broadcasted_iota(jnp.int32, (blk, blk), 1)
    same_chunk = _div_pow2(ri, chunk) == _div_pow2(ci, chunk)
    tri = jnp.where(ci <= ri, jnp.where(same_chunk, 1.0, 0.0), 0.0).astype(BF16)
    l1, l2 = _split2(logf)
    for r0 in range(0, tm, blk):
        b_scr[r0:r0 + blk, :] = _dot(tri, l1[r0:r0 + blk, :]) + _dot(tri, l2[r0:r0 + blk, :])

    if not per_seq_state:
        @pl.when(pl.program_id(1) == 0)
        def _():
            st_scr[...] = jnp.zeros_like(st_scr)

    near_rows = blk
    masks = _hgrn_masks(chunk, sub, near_rows)
    for hh in range(N_HEADS):
        hs = slice(hh * HEAD_W, (hh + 1) * HEAD_W)
        for r0 in range(0, tm, near_rows):
            nr = slice(r0, r0 + near_rows)
            o_scr[nr, hs] = _hgrn_near(b_scr[nr, hs], f_scr[nr, hs], q_scr[nr, hs], k_scr[nr, hs], v_scr[nr, hs],
                                       masks, sub)
        st = None if per_seq_state else st_scr[hh]
        for c in range(n_chunks):
            rs = slice(c * chunk, (c + 1) * chunk)
            if per_seq_state:
                st = s0_ref[c, hh].T
            o, st = _hgrn_chunk(b_scr[rs, hs], q_scr[rs, hs], k_scr[rs, hs], v_scr[rs, hs], st, masks["wide"],
                                chunk, sub)
            o_scr[rs, hs] = o_scr[rs, hs] + o
            if per_seq_state:
                sfin_ref[c, hh] = st.T
        if not per_seq_state:
            st_scr[hh] = st

    group_b()

    if not per_seq_state:
        @pl.when(pl.program_id(1) == pl.num_programs(1) - 1)
        def _():
            for hh in range(N_HEADS):
                sfin_ref[0, hh] = st_scr[hh].T

    sg, _ = _sigmoid_pair(proj(3))
    za = proj(4)
    sz, _ = _sigmoid_pair(za)
    outs = []
    for hh in range(N_HEADS):
        hs = slice(hh * HEAD_W, (hh + 1) * HEAD_W)
        oh = o_scr[:, hs] * sg[:, hs]
        mo = jnp.mean(oh * oh, axis=-1, keepdims=True)
        outs.append(oh * lax.rsqrt(mo + EPS))
    ya = jnp.concatenate(outs, axis=1) * hgw_ref[...] * (za * sz)
    ya_ref[...] = ya.reshape(nseq, rows, GROUP_W).astype(ya_ref.dtype)


def _mix_in(x, mod, norm_w, w_in, lb, hgw, qnw, knw, *, nseq, rows, chunk, sub, s0, attn_tile, q_dtype, q_scale):
    n_s, r_total, d = x.shape
    per_seq_state = s0 is not None
    emit = attn_tile is not None
    tm = nseq * rows
    if per_seq_state:
        assert rows == r_total and rows == chunk and n_s % nseq == 0
        grid = (n_s // nseq,)
        tok = lambda i: (i, 0, 0)
        seq = lambda i: (i, 0, 0)
        const2 = lambda i: (0, 0)
        st_map = lambda i: (i, 0, 0, 0)
        sem = ("arbitrary",)
    else:
        assert nseq == 1 and r_total % rows == 0 and rows % chunk == 0
        grid = (n_s, r_total // rows)
        tok = lambda b, t: (b, t, 0)
        seq = lambda b, t: (b, 0, 0)
        const2 = lambda b, t: (0, 0)
        st_map = lambda b, t: (b, 0, 0, 0)
        sem = ("parallel", "arbitrary")
    if emit:
        assert rows % attn_tile == 0
    d_in = w_in.shape[1]
    tok_spec = pl.BlockSpec((nseq, rows, GROUP_W), tok)
    tokhead_spec = pl.BlockSpec((nseq, rows * N_HEADS, HEAD_W), tok)
    in_specs = [pl.BlockSpec((nseq, rows, d), tok),
                pl.BlockSpec((nseq, 1, 3 * d), seq),
                pl.BlockSpec((1, d), const2),
                pl.BlockSpec((d, d_in), const2),
                pl.BlockSpec((1, GROUP_W), const2),
                pl.BlockSpec((1, GROUP_W), const2),
                pl.BlockSpec((1, GROUP_W), const2),
                pl.BlockSpec((1, GROUP_W), const2)]
    args = [x, mod, norm_w, w_in, lb, hgw, qnw, knw]
    if per_seq_state:
        in_specs.append(pl.BlockSpec((nseq, N_HEADS, HEAD_W, HEAD_W), st_map))
        args.append(s0)
    out_shape = [jax.ShapeDtypeStruct((n_s, r_total, GROUP_W), q_dtype),
                 jax.ShapeDtypeStruct((n_s, r_total, GROUP_W), q_dtype),
                 jax.ShapeDtypeStruct((n_s, r_total * N_HEADS, HEAD_W), F32),
                 jax.ShapeDtypeStruct((n_s, r_total * N_HEADS, HEAD_W), F32),
                 jax.ShapeDtypeStruct((n_s, r_total, GROUP_W), F32)]
    out_specs = [tok_spec, tok_spec, tokhead_spec, tokhead_spec, tok_spec]
    if emit:
        nt = r_total // attn_tile
        out_shape += [jax.ShapeDtypeStruct((n_s, r_total, GROUP_W), BF16),
                      jax.ShapeDtypeStruct((n_s, N_HEADS, nt, VT_ROWS, attn_tile), BF16)]
        out_specs += [tok_spec,
                      pl.BlockSpec((1, N_HEADS, rows // attn_tile, VT_ROWS, attn_tile),
                                   lambda b, t: (b, 0, t, 0, 0))]
    out_shape.append(jax.ShapeDtypeStruct((n_s, N_HEADS, HEAD_W, HEAD_W), F32))
    out_specs.append(pl.BlockSpec((nseq, N_HEADS, HEAD_W, HEAD_W), st_map))
    scratch = [pltpu.VMEM((N_HEADS, HEAD_W, HEAD_W), F32)] + [pltpu.VMEM((tm, GROUP_W), F32)] * 6
    kern = functools.partial(_mix_in_kernel, nseq=nseq, rows=rows, chunk=chunk, sub=sub,
                             per_seq_state=per_seq_state, emit_attn_kv=emit, q_scale=q_scale)
    return pl.pallas_call(
        kern, out_shape=tuple(out_shape), grid=grid, in_specs=in_specs, out_specs=tuple(out_specs),
        scratch_shapes=scratch, compiler_params=_cparams(sem),
        name="mix_in_sample" if per_seq_state else "mix_in_prompt",
    )(*args)


def _sub_norm_gate(o, snw, gate):
    ms = jnp.mean(o * o, axis=-1, keepdims=True)
    return o * lax.rsqrt(ms + EPS) * snw * (1.0 - LAMBDA_INIT) * gate


def _attn_kernel(lam_ref, q_ref, k_ref, vt_ref, bias_ref, g_ref, snw_ref, o_ref,
                 qs_scr, sa_scr, sb_scr, m_scr, acc_scr, *, tq, tk, n_bias):
    qi = pl.program_id(2)
    ratio = tq // tk
    n_tiles = ratio * (qi + 1)
    n_heads = qs_scr.shape[0]
    heads = range(n_heads)
    lane = lax.broadcasted_iota(jnp.int32, (tq, HEAD_W), 1)
    for hh in heads:
        q = q_ref[0, :, hh * HEAD_W:(hh + 1) * HEAD_W].astype(F32)
        qs_scr[hh, 0:tq, :] = jnp.where(lane < HALF, q, 0.0).astype(BF16)
        qs_scr[hh, tq:2 * tq, :] = jnp.where(lane >= HALF, q, 0.0).astype(BF16)
    m_scr[...] = jnp.full(m_scr.shape, NEG, F32)
    acc_scr[...] = jnp.zeros_like(acc_scr)

    def scores(hh, t, s_ref, q0=0):
        r0 = pl.multiple_of(t * tk, tk)
        kt = k_ref[0, pl.ds(r0, tk), hh * HEAD_W:(hh + 1) * HEAD_W]
        bt = bias_ref[hh, jnp.clip(ratio * qi - t + (ratio - 1), 0, n_bias - 1)]
        for c in range(2):
            cols = slice(c * tq + q0, (c + 1) * tq)
            s_ref[hh, :, cols] = _dot_nt(kt, qs_scr[hh, cols, :]) + bt[:, q0:tq]

    def softmax_pv(hh, t, s_ref, q0=0):
        vt = vt_ref[0, hh, t]
        for c in range(2):
            cols = slice(c * tq + q0, (c + 1) * tq)
            m_old = m_scr[hh, :, cols]
            m_new = jnp.maximum(m_old, jnp.max(s_ref[hh, :, cols], axis=0, keepdims=True))
            alpha = jnp.exp2(m_old - m_new)
            p = jnp.exp2(s_ref[hh, :, cols] - m_new)
            acc_scr[hh, :, cols] = acc_scr[hh, :, cols] * alpha + _dot(vt, p.astype(BF16))
            m_scr[hh, :, cols] = m_new

    for hh in heads:
        scores(hh, 0, sa_scr)

    def pair(j):
        t0 = 2 * j
        for hh in heads:
            scores(hh, t0 + 1, sb_scr)
        for hh in heads:
            softmax_pv(hh, t0, sa_scr)
        for hh in heads:
            scores(hh, t0 + 2, sa_scr)
        for hh in heads:
            softmax_pv(hh, t0 + 1, sb_scr)

    def body(jj, carry):
        pair(2 * jj)
        pair(2 * jj + 1)
        return carry

    lax.fori_loop(0, qi // 2, body, 0)

    @pl.when(qi % 2 == 1)
    def _():
        pair(qi - 1)

    q_last = tq - tk
    for hh in heads:
        scores(hh, n_tiles - 1, sb_scr, q_last)
    for hh in heads:
        softmax_pv(hh, n_tiles - 2, sa_scr)
    for hh in heads:
        softmax_pv(hh, n_tiles - 1, sb_scr, q_last)

    lam = lam_ref[0, 0]
    for hh in heads:
        hs = slice(hh * HEAD_W, (hh + 1) * HEAD_W)
        inv = 1.0 / acc_scr[hh, HEAD_W:HEAD_W + 1, :]
        acc = acc_scr[hh, 0:HEAD_W, :]
        o_t = acc[:, 0:tq] * inv[:, 0:tq] - lam * (acc[:, tq:2 * tq] * inv[:, tq:2 * tq])
        o_n = o_t * lax.rsqrt(jnp.mean(o_t * o_t, axis=0, keepdims=True) + EPS)
        o_ref[0, :, hs] = (o_n.T * snw_ref[...] * (1.0 - LAMBDA_INIT) * g_ref[0, :, hs]).astype(o_ref.dtype)


def _attn_prompt(lam, q, kb, vt, bias, gate, snw, tq, tk):
    b, l, _ = q.shape
    assert tq == 2 * tk and l % tq == 0
    nq = l // tq
    nt = l // tk
    nd = bias.shape[1]
    hp = ATTN_HEADS_PER_STEP
    assert N_HEADS % hp == 0
    return pl.pallas_call(
        functools.partial(_attn_kernel, tq=tq, tk=tk, n_bias=nd),
        out_shape=jax.ShapeDtypeStruct((b, l, GROUP_W), BF16),
        grid=(b, N_HEADS // hp, nq),
        in_specs=[pl.BlockSpec(memory_space=pltpu.SMEM),
                  pl.BlockSpec((1, tq, hp * HEAD_W), lambda bb, h, i: (bb, i, h)),
                  pl.BlockSpec((1, l, hp * HEAD_W), lambda bb, h, i: (bb, 0, h)),
                  pl.BlockSpec((1, hp, nt, VT_ROWS, tk), lambda bb, h, i: (bb, h, 0, 0, 0)),
                  pl.BlockSpec((hp, nd, tk, tq), lambda bb, h, i: (h, 0, 0, 0)),
                  pl.BlockSpec((1, tq, hp * HEAD_W), lambda bb, h, i: (bb, i, h)),
                  pl.BlockSpec((1, HEAD_W), lambda bb, h, i: (0, 0))],
        out_specs=pl.BlockSpec((1, tq, hp * HEAD_W), lambda bb, h, i: (bb, i, h)),
        scratch_shapes=[pltpu.VMEM((hp, 2 * tq, HEAD_W), BF16),
                        pltpu.VMEM((hp, tk, 2 * tq), F32),
                        pltpu.VMEM((hp, tk, 2 * tq), F32),
                        pltpu.VMEM((hp, 1, 2 * tq), F32),
                        pltpu.VMEM((hp, VT_ROWS, 2 * tq), F32)],
        compiler_params=_cparams(("parallel", "parallel", "arbitrary")),
        name="attn_prompt",
    )(lam, q, kb, vt, bias, gate, snw)


def _decode_kernel(pt_ref, lam_ref, q_ref, kn_ref, vn_ref, g_ref, snw_ref, bias_ref, biasn_ref, ck_hbm, cv_hbm,
                   o_ref, kbuf, vbuf, sem, qst_scr, m_scr, l_scr, acc_scr, kpg_scr, vpg_scr,
                   *, grp, ring, rows, page, n_seq, groups_per_seq):
    hr = 2 * rows
    total_groups = n_seq * groups_per_seq

    def group_copies(gidx, slot0):
        b = _div_pow2(gidx, groups_per_seq)
        p0 = jnp.bitwise_and(gidx, groups_per_seq - 1) * grp
        copies = []
        for j in range(grp):
            pid = pt_ref[b, p0 + j]
            copies.append(pltpu.make_async_copy(ck_hbm.at[pid], kbuf.at[slot0 + j], sem.at[0, slot0 + j]))
            copies.append(pltpu.make_async_copy(cv_hbm.at[pid], vbuf.at[slot0 + j], sem.at[1, slot0 + j]))
        return copies

    def start_group(gidx, slot0):
        for cp in group_copies(gidx, slot0):
            cp.start()

    def wait_group(gidx, slot0):
        for cp in group_copies(gidx, slot0):
            cp.wait()

    def head_page(buf, slot, hh):
        return buf[slot, pl.ds(hh, page, stride=N_HEADS), :].astype(BF16)

    def init_sequence(b):
        q = q_ref[b].astype(F32)
        lane = lax.broadcasted_iota(jnp.int32, (rows, HEAD_W), 1)
        kpg_scr[...] = jnp.zeros_like(kpg_scr)
        vpg_scr[...] = jnp.zeros_like(vpg_scr)
        for hh in range(N_HEADS):
            qh = q[:, hh * HEAD_W:(hh + 1) * HEAD_W]
            qst_scr[hh] = jnp.concatenate([jnp.where(lane < HALF, qh, 0.0), jnp.where(lane >= HALF, qh, 0.0)],
                                          axis=0).astype(BF16)
            kpg_scr[hh, 0:rows, :] = kn_ref[b, pl.ds(hh, rows, stride=N_HEADS), :]
            vpg_scr[hh, 0:rows, :] = vn_ref[b, pl.ds(hh, rows, stride=N_HEADS), :]
        m_scr[...] = jnp.full(m_scr.shape, NEG, F32)
        l_scr[...] = jnp.zeros_like(l_scr)
        acc_scr[...] = jnp.zeros_like(acc_scr)

    def update(s, n_blocks, value_fn):
        m_old = m_scr[...]
        m_new = jnp.maximum(m_old, jnp.max(s, axis=-1, keepdims=True))
        alpha = jnp.exp(m_old - m_new)
        p = jnp.exp(s - m_new)
        l_scr[...] = alpha * l_scr[...] + jnp.sum(p, axis=-1, keepdims=True)
        pb = p.astype(BF16)
        pv_rows = []
        for hh in range(N_HEADS):
            ph = pb[hh * hr:(hh + 1) * hr, :]
            pv = _dot(ph[:, 0:page], value_fn(hh, 0))
            for j in range(1, n_blocks):
                pv = pv + _dot(ph[:, j * page:(j + 1) * page], value_fn(hh, j))
            pv_rows.append(pv)
        acc_scr[...] = acc_scr[...] * alpha + jnp.concatenate(pv_rows, axis=0)
        m_scr[...] = m_new

    def scores(n_blocks, key_fn):
        s_rows = []
        for hh in range(N_HEADS):
            qh = qst_scr[hh]
            blocks = [_dot_nt(qh, key_fn(hh, j)) for j in range(n_blocks)]
            s_rows.append(blocks[0] if n_blocks == 1 else jnp.concatenate(blocks, axis=1))
        return jnp.concatenate(s_rows, axis=0)

    def consume_group(gs, slot0):
        s = scores(grp, lambda hh, j: head_page(kbuf, slot0 + j, hh)) + bias_ref[gs]
        update(s, grp, lambda hh, j: head_page(vbuf, slot0 + j, hh))

    def finish_sequence(b):
        s_new = scores(1, lambda hh, j: kpg_scr[hh].astype(BF16)) + biasn_ref[...]
        update(s_new, 1, lambda hh, j: vpg_scr[hh].astype(BF16))
        lam = lam_ref[0, 0]
        inv = 1.0 / l_scr[...]
        acc = acc_scr[...] * inv
        gate = g_ref[b]
        for hh in range(N_HEADS):
            hs = slice(hh * HEAD_W, (hh + 1) * HEAD_W)
            r0 = hh * hr
            o = acc[r0:r0 + rows, :] - lam * acc[r0 + rows:r0 + 2 * rows, :]
            o_ref[b, :, hs] = _sub_norm_gate(o, snw_ref[...], gate[:, hs]).astype(o_ref.dtype)

    for part in range(ring):
        start_group(part, part * grp)

    def seq_body(b, carry):
        init_sequence(b)

        def ring_body(it, c):
            g0 = b * groups_per_seq + ring * it
            for part in range(ring):
                gidx = g0 + part
                slot0 = part * grp
                wait_group(gidx, slot0)
                consume_group(ring * it + part, slot0)

                @pl.when(gidx + ring < total_groups)
                def _():
                    start_group(gidx + ring, slot0)
            return c

        lax.fori_loop(0, groups_per_seq // ring, ring_body, 0)
        finish_sequence(b)
        return carry

    lax.fori_loop(0, n_seq, seq_body, 0)


def _attn_decode(page_table, lam, q, k_new, v_new, gate, snw, bias_dec, bias_new, cache_k, cache_v, n_pages):
    nb, rows, _ = q.shape
    pages_per_seq = page_table.shape[1]
    page = cache_k.shape[1] // N_HEADS
    grp = n_pages
    groups_per_seq = pages_per_seq // grp
    assert pages_per_seq % grp == 0 and page == LANES
    ring = min(DECODE_RING_GROUPS, groups_per_seq)
    assert groups_per_seq % ring == 0 and groups_per_seq & (groups_per_seq - 1) == 0
    assert bias_dec.shape == (groups_per_seq, 2 * N_HEADS * rows, grp * page)
    n_rows = 2 * N_HEADS * rows
    smem = pl.BlockSpec(memory_space=pltpu.SMEM)
    vmem = pl.BlockSpec(memory_space=pltpu.VMEM)
    hbm = pl.BlockSpec(memory_space=pl.ANY)
    return pl.pallas_call(
        functools.partial(_decode_kernel, grp=grp, ring=ring, rows=rows, page=page, n_seq=nb,
                          groups_per_seq=groups_per_seq),
        out_shape=jax.ShapeDtypeStruct((nb, rows, GROUP_W), F32),
        in_specs=[smem, smem, vmem, vmem, vmem, vmem, vmem, vmem, vmem, hbm, hbm],
        out_specs=vmem,
        scratch_shapes=[pltpu.VMEM((ring * grp, page * N_HEADS, HEAD_W), F32),
                        pltpu.VMEM((ring * grp, page * N_HEADS, HEAD_W), F32),
                        pltpu.SemaphoreType.DMA((2, ring * grp)),
                        pltpu.VMEM((N_HEADS, 2 * rows, HEAD_W), BF16),
                        pltpu.VMEM((n_rows, 1), F32),
                        pltpu.VMEM((n_rows, 1), F32),
                        pltpu.VMEM((n_rows, HEAD_W), F32),
                        pltpu.VMEM((N_HEADS, page, HEAD_W), F32),
                        pltpu.VMEM((N_HEADS, page, HEAD_W), F32)],
        compiler_params=pltpu.CompilerParams(vmem_limit_bytes=VMEM_LIMIT_BYTES),
        name="attn_decode",
    )(page_table, lam, q, k_new, v_new, gate, snw, bias_dec, bias_new, cache_k, cache_v)


def _mix_out_kernel(ya_ref, yb_ref, x_ref, mod_ref, w_ref, o_ref, *, nseq, rows):
    d = x_ref.shape[-1]
    tm = nseq * rows
    ya = ya_ref[...].reshape(tm, GROUP_W).astype(BF16)
    yb = yb_ref[...].reshape(tm, GROUP_W).astype(BF16)
    out = _dot(ya, w_ref[0:GROUP_W, :]) + _dot(yb, w_ref[GROUP_W:2 * GROUP_W, :])
    gate = mod_ref[:, :, 2 * d:3 * d]
    o_ref[...] = x_ref[...] + gate * out.reshape(nseq, rows, d)


def _mix_out(ya, yb, x, mod, w_out, *, nseq, rows):
    n_s, r_total, d = x.shape
    grid = (n_s // nseq, r_total // rows)
    tok = lambda s, t: (s, t, 0)
    return pl.pallas_call(
        functools.partial(_mix_out_kernel, nseq=nseq, rows=rows),
        out_shape=jax.ShapeDtypeStruct(x.shape, F32),
        grid=grid,
        in_specs=[pl.BlockSpec((nseq, rows, GROUP_W), tok),
                  pl.BlockSpec((nseq, rows, GROUP_W), tok),
                  pl.BlockSpec((nseq, rows, d), tok),
                  pl.BlockSpec((nseq, 1, 3 * d), lambda s, t: (s, 0, 0)),
                  pl.BlockSpec((2 * GROUP_W, d), lambda s, t: (0, 0))],
        out_specs=pl.BlockSpec((nseq, rows, d), tok),
        compiler_params=_cparams(("parallel", "arbitrary")),
        name="mix_out",
    )(ya, yb, x, mod, w_out)


def _largest_tile(n, cap):
    t = cap
    while n % t:
        t //= 2
    return t


def kernel(x_prompt, x_sample, c_prompt, c_sample, cache_k, cache_v, state_hgrn, page_table, norm_w, w_ada,
           b_ada, w_in, w_out, lb_param, hg_norm_w, q_norm_w, k_norm_w, lam_q1, lam_k1, lam_q2, lam_k2,
           sub_norm_w, rel_table):
    depth = w_in.shape[0]
    assert depth == 1
    b, l, d = x_prompt.shape
    db, dl, _ = x_sample.shape
    assert d == 2 * GROUP_W and w_in.shape[2] == 9 * GROUP_W
    page = cache_k.shape[2]
    past = page_table.shape[1] * page
    assert dl % SUBLANES == 0 and cache_k.shape[3] * cache_k.shape[4] == GROUP_W

    w_in_b = w_in[0].astype(BF16)
    w_out_b = w_out[0].astype(BF16)
    nw = norm_w.reshape(1, d)
    hgw = jnp.tile(hg_norm_w.reshape(1, HEAD_W), (1, N_HEADS))
    qnw = jnp.tile(q_norm_w.reshape(1, HALF), (1, GROUP_W // HALF))
    knw = jnp.tile(k_norm_w.reshape(1, HALF), (1, GROUP_W // HALF))
    snw = sub_norm_w.reshape(1, HEAD_W)

    mod = _adaln_mod(jnp.concatenate([c_prompt, c_sample], axis=0), w_ada[0], b_ada[0])
    mod_p = mod[:b].reshape(b, 1, 3 * d)
    mod_s = mod[b:].reshape(db, 1, 3 * d)
    lb, lam = _small_params(lb_param, lam_q1, lam_k1, lam_q2, lam_k2)

    log2e = math.log2(math.e)
    ya, q, kf, vf, gb, kb, vt, st_p = _mix_in(
        x_prompt, mod_p, nw, w_in_b, lb, hgw, qnw, knw, nseq=1, rows=_largest_tile(l, MIX_ROWS),
        chunk=CHUNK if l % CHUNK == 0 else l, sub=SUB, s0=None, attn_tile=ATTN_TK, q_dtype=BF16,
        q_scale=HALF ** -0.5 * log2e)
    bias_p = _bias_prompt(rel_table, ATTN_TK, ATTN_TQ, log2e)
    yb = _attn_prompt(lam, q, kb, vt, bias_p, gb, snw, ATTN_TQ, ATTN_TK)
    y_prompt = _mix_out(ya, yb, x_prompt, mod_p, w_out_b, nseq=1, rows=_largest_tile(l, 512))

    nseq = _largest_tile(db, 8)
    ya_s, q_s, kf_s, vf_s, gb_s, st_s = _mix_in(
        x_sample, mod_s, nw, w_in_b, lb, hgw, qnw, knw, nseq=nseq, rows=dl,
        chunk=dl, sub=dl, s0=state_hgrn[0], attn_tile=None, q_dtype=F32, q_scale=HALF ** -0.5)
    grp = _largest_tile(page_table.shape[1] // 2, DECODE_GROUP_PAGES)
    bias_d, bias_n = _bias_decode(rel_table, dl, past, grp * page)
    n_rows = 2 * N_HEADS * dl
    yb_s = _attn_decode(page_table, lam, q_s, kf_s, vf_s, gb_s, snw,
                        bias_d.reshape(-1, n_rows, grp * page), bias_n.reshape(n_rows, LANES),
                        cache_k.reshape(-1, page * N_HEADS, HEAD_W), cache_v.reshape(-1, page * N_HEADS, HEAD_W),
                        n_pages=grp)
    y_sample = _mix_out(ya_s, yb_s, x_sample, mod_s, w_out_b, nseq=nseq, rows=dl)

    hb = N_HEADS
    return (y_prompt, y_sample,
            kf.reshape(1, b, l, hb, HEAD_W), vf.reshape(1, b, l, hb, HEAD_W), st_p[None],
            kf_s.reshape(1, db, dl, hb, HEAD_W), vf_s.reshape(1, db, dl, hb, HEAD_W), st_s[None])
```

```python
import functools
import math

import jax
import jax.numpy as jnp
from jax import lax
from jax.experimental import pallas as pl
from jax.experimental.pallas import tpu as pltpu

F32 = jnp.float32
BF16 = jnp.bfloat16

N_HEADS = 4
HEAD_W = 128
HALF = 64
GROUP_W = N_HEADS * HEAD_W
N_BUCKETS = 32
MAX_EXACT = N_BUCKETS // 2
MAX_DIST = 1024
CHUNK = 64
SUB = 16
EPS = 1e-6
LAMBDA_INIT = 0.8 - 0.6 * math.exp(-0.3 * 0)
NEG = -1e30
ATTN_TQ = 512
ATTN_TK = 256
MIX_ROWS = 512
MIX_BLOCK_ROWS = 256
MIX_OUT_ROWS = 1024
ATTN_HEADS_PER_STEP = 1
VT_ROWS = HEAD_W + 16
DECODE_GROUP_PAGES = 8
DECODE_RING_GROUPS = 4

LANES = 128
SUBLANES = 8
VMEM_LIMIT_BYTES = 56 * 1024 * 1024


def _bucket_lower_bounds():
    los = list(range(MAX_EXACT))
    span = N_BUCKETS - MAX_EXACT
    ratio = MAX_DIST // MAX_EXACT
    for m in range(MAX_EXACT, N_BUCKETS):
        d = MAX_EXACT
        while d ** span < (MAX_EXACT ** span) * (ratio ** (m - MAX_EXACT)):
            d += 1
        los.append(d)
    return los


BUCKET_LO = _bucket_lower_bounds()
SATURATED_DIST = BUCKET_LO[-1]


def _cparams(sem):
    return pltpu.CompilerParams(dimension_semantics=sem, vmem_limit_bytes=VMEM_LIMIT_BYTES)


def _split2(x):
    hi = x.astype(BF16)
    lo = (x - hi.astype(F32)).astype(BF16)
    return hi, lo


def _dot(a, b):
    return jnp.dot(a, b, preferred_element_type=F32)


def _dot_nt(a, b):
    return lax.dot_general(a, b, (((1,), (1,)), ((), ())), preferred_element_type=F32)


def _dot_tn(a, b):
    return lax.dot_general(a, b, (((0,), (0,)), ((), ())), preferred_element_type=F32)


def _div_pow2(x, n):
    assert n > 0 and n & (n - 1) == 0
    return lax.shift_right_logical(x, n.bit_length() - 1)


def _sigmoid_pair(x):
    t = jnp.exp(-jnp.abs(x))
    r = 1.0 / (1.0 + t)
    tr = t * r
    pos = x >= 0
    return jnp.where(pos, r, tr), jnp.where(pos, tr, r)


def _mod_kernel(c_ref, w_ref, b_ref, o_ref):
    c = c_ref[...]
    sig, _ = _sigmoid_pair(c)
    s = c * sig
    w = w_ref[...]
    s1, s2 = _split2(s)
    w1, w2 = _split2(w)
    o_ref[...] = _dot(s1, w1) + (_dot(s1, w2) + _dot(s2, w1)) + b_ref[...]


def _adaln_mod(c_all, w_ada, b_ada):
    n, d = c_all.shape
    d3 = w_ada.shape[1]
    bn = d
    return pl.pallas_call(
        _mod_kernel,
        out_shape=jax.ShapeDtypeStruct((n, d3), F32),
        grid=(d3 // bn,),
        in_specs=[pl.BlockSpec((n, d), lambda j: (0, 0)),
                  pl.BlockSpec((d, bn), lambda j: (0, j)),
                  pl.BlockSpec((1, bn), lambda j: (0, j))],
        out_specs=pl.BlockSpec((n, bn), lambda j: (0, j)),
        compiler_params=_cparams(("arbitrary",)),
        name="adaln_mod",
    )(c_all, w_ada, b_ada.reshape(1, d3))


def _small_kernel(lbp_ref, q1_ref, k1_ref, q2_ref, k2_ref, lb_ref, lam_ref):
    p = lbp_ref[...]
    m = jnp.max(p, axis=0, keepdims=True)
    e = jnp.exp(p - m)
    lb_ref[...] = e[0:1, :] / jnp.sum(e, axis=0, keepdims=True)
    s1 = jnp.sum(q1_ref[...] * k1_ref[...], axis=-1, keepdims=True)
    s2 = jnp.sum(q2_ref[...] * k2_ref[...], axis=-1, keepdims=True)
    lam_ref[...] = jnp.exp(s1) - jnp.exp(s2) + LAMBDA_INIT


def _small_params(lb_param, lq1, lk1, lq2, lk2):
    return pl.pallas_call(
        _small_kernel,
        out_shape=(jax.ShapeDtypeStruct((1, lb_param.shape[1]), F32),
                   jax.ShapeDtypeStruct((1, 1), F32)),
        name="small_params",
    )(lb_param, lq1, lk1, lq2, lk2)


def _bucket_of(d):
    return max(m for m in range(N_BUCKETS) if d >= BUCKET_LO[m])


def _bias_select(d, tab_ref, h, d_min, d_max):
    m_lo, m_hi = _bucket_of(d_min), _bucket_of(d_max)
    val = jnp.full(d.shape, tab_ref[m_hi, h], F32)
    for m in range(m_hi - 1, m_lo - 1, -1):
        val = jnp.where(d < BUCKET_LO[m + 1], tab_ref[m, h], val)
    return val


def _bias_prompt_kernel(tab_ref, o_ref, *, tk, tq, scale):
    h = pl.program_id(0)
    kk = lax.broadcasted_iota(jnp.int32, (tk, tq), 0)
    qq = lax.broadcasted_iota(jnp.int32, (tk, tq), 1)
    for idx in range(o_ref.shape[1]):
        dl = idx - (tq // tk - 1)
        d = dl * tk + qq - kk
        d_min, d_max = dl * tk - (tk - 1), dl * tk + tq - 1
        val = _bias_select(jnp.maximum(d, 0), tab_ref, h, max(d_min, 0), max(d_max, 0)) * scale
        o_ref[0, idx] = jnp.where(d >= 0, val, NEG) if d_min < 0 else val


def _n_bias_tiles(tk, tq):
    n_const = 0
    while n_const * tk - (tk - 1) < SATURATED_DIST:
        n_const += 1
    return n_const + 1 + (tq // tk - 1)


def _bias_prompt(rel_table, tk, tq, scale):
    nd = _n_bias_tiles(tk, tq)
    return pl.pallas_call(
        functools.partial(_bias_prompt_kernel, tk=tk, tq=tq, scale=scale),
        out_shape=jax.ShapeDtypeStruct((N_HEADS, nd, tk, tq), F32),
        grid=(N_HEADS,),
        in_specs=[pl.BlockSpec(memory_space=pltpu.SMEM)],
        out_specs=pl.BlockSpec((1, nd, tk, tq), lambda h: (h, 0, 0, 0)),
        compiler_params=_cparams(("arbitrary",)),
        name="bias_prompt",
    )(rel_table)


def _bias_decode_kernel(tab_ref, o_ref, on_ref, *, rows, past, width):
    h = pl.program_id(0)
    i = lax.broadcasted_iota(jnp.int32, (rows, width), 0)
    lane = lax.broadcasted_iota(jnp.int32, (rows, width), 1)
    for g in range(past // width):
        d_min, d_max = past - (g + 1) * width + 1, past - g * width + rows - 1
        val = _bias_select(past + i - (g * width + lane), tab_ref, h, d_min, d_max)
        o_ref[g, 0, 0] = val
        o_ref[g, 0, 1] = val
    i2 = lax.broadcasted_iota(jnp.int32, (rows, LANES), 0)
    j2 = lax.broadcasted_iota(jnp.int32, (rows, LANES), 1)
    d2 = i2 - j2
    vn = jnp.where(d2 >= 0, _bias_select(jnp.maximum(d2, 0), tab_ref, h, 0, rows - 1), NEG)
    on_ref[0, 0] = vn
    on_ref[0, 1] = vn


def _bias_decode(rel_table, rows, past, width):
    ng = past // width
    return pl.pallas_call(
        functools.partial(_bias_decode_kernel, rows=rows, past=past, width=width),
        out_shape=(jax.ShapeDtypeStruct((ng, N_HEADS, 2, rows, width), F32),
                   jax.ShapeDtypeStruct((N_HEADS, 2, rows, LANES), F32)),
        grid=(N_HEADS,),
        in_specs=[pl.BlockSpec(memory_space=pltpu.SMEM)],
        out_specs=(pl.BlockSpec((ng, 1, 2, rows, width), lambda h: (0, h, 0, 0, 0)),
                   pl.BlockSpec((1, 2, rows, LANES), lambda h: (h, 0, 0, 0))),
        compiler_params=_cparams(("arbitrary",)),
        name="bias_decode",
    )(rel_table)


def _group_mean_sq(x, gmat):
    sq = x * x
    outs = []
    w = gmat.shape[0]
    for c0 in range(0, x.shape[1], w):
        outs.append(_dot(sq[:, c0:c0 + w].astype(BF16), gmat))
    return jnp.concatenate(outs, axis=1)


def _hgrn_levels(sub):
    levels, h = [], sub // 2
    while h >= 1:
        levels.append(h)
        h //= 2
    return levels


def _hgrn_masks(chunk, sub, n):
    n_sub = chunk // sub
    masks = {"wide": None}
    if n_sub > 1:
        rr = lax.broadcasted_iota(jnp.int32, (chunk, n_sub * chunk), 0)
        cc = lax.broadcasted_iota(jnp.int32, (chunk, n_sub * chunk), 1)
        own_anchor = _div_pow2(cc, chunk) == _div_pow2(rr, sub)
        earlier = _div_pow2(jnp.bitwise_and(cc, chunk - 1), sub) < _div_pow2(rr, sub)
        masks["wide"] = jnp.where(own_anchor, jnp.where(earlier, 1.0, 0.0), 0.0)
    rr = lax.broadcasted_iota(jnp.int32, (n, n), 0)
    cc = lax.broadcasted_iota(jnp.int32, (n, n), 1)
    for h in _hgrn_levels(sub):
        same = _div_pow2(rr, 2 * h) == _div_pow2(cc, 2 * h)
        t_second = jnp.bitwise_and(rr, 2 * h - 1) >= h
        j_first = jnp.bitwise_and(cc, 2 * h - 1) < h
        masks[h] = jnp.where(same, jnp.where(t_second, jnp.where(j_first, 1.0, 0.0), 0.0), 0.0)
    masks["diag"] = jnp.where(rr == cc, 1.0, 0.0)
    return masks


def _level_boundary(bc, h, chunk):
    pieces = []
    if 2 * h >= SUBLANES:
        for blk in range(chunk // (2 * h)):
            r = blk * 2 * h + h - 1
            pieces.append(jnp.broadcast_to(bc[r:r + 1, :], (2 * h, HEAD_W)))
    else:
        per = SUBLANES // (2 * h)
        sub_row = lax.broadcasted_iota(jnp.int32, (SUBLANES, HEAD_W), 0)
        for g8 in range(chunk // SUBLANES):
            val = None
            for k in range(per):
                r = g8 * SUBLANES + k * 2 * h + h - 1
                row = jnp.broadcast_to(bc[r:r + 1, :], (SUBLANES, HEAD_W))
                val = row if val is None else jnp.where(sub_row >= k * 2 * h, row, val)
            pieces.append(val)
    return pieces[0] if len(pieces) == 1 else jnp.concatenate(pieces, axis=0)


def _hgrn_near(b, f, q, k, v, masks, sub):
    n = b.shape[0]
    kb16 = k.astype(BF16)
    att = _dot_nt(q.astype(BF16), kb16) * masks["diag"]
    odd = jnp.bitwise_and(lax.broadcasted_iota(jnp.int32, (n, HEAD_W), 0), 1) == 1
    for h in _hgrn_levels(sub):
        if h == 1:
            att_h = _dot_nt((q * jnp.where(odd, f, 1.0)).astype(BF16), kb16)
        else:
            w = jnp.exp(-jnp.abs(b - _level_boundary(b, h, n)))
            att_h = _dot_nt((q * w).astype(BF16), (k * w).astype(BF16))
        att = att + att_h * masks[h]
    return _dot(att.astype(BF16), v.astype(BF16))


def _hgrn_chunk(bc, qc, kc, vc, st, mask_wide, chunk, sub):
    n_sub = chunk // sub
    o = _dot_nt((qc * jnp.exp(bc)).astype(BF16), st.astype(BF16))
    if n_sub > 1:
        vb = vc.astype(BF16)
        q_parts, k_parts = [], []
        for i in range(n_sub):
            lo, hi = i * sub, (i + 1) * sub
            anchor = bc[max(lo - 1, 0):max(lo - 1, 0) + 1, :]
            q_parts.append(qc[lo:hi, :] * jnp.exp(jnp.minimum(bc[lo:hi, :] - anchor, 0.0)))
            k_parts.append((kc * jnp.exp(jnp.minimum(anchor - bc, 0.0))).astype(BF16))
        qa = jnp.concatenate(q_parts, axis=0).astype(BF16)
        att_w = _dot_nt(qa, jnp.concatenate(k_parts, axis=0)) * mask_wide
        o = o + _dot(att_w.astype(BF16), jnp.concatenate([vb] * n_sub, axis=0))
    b_last = bc[chunk - 1:chunk, :]
    ke = kc * jnp.exp(b_last - bc)
    st_new = st * jnp.exp(b_last) + _dot_tn(vc.astype(BF16), ke.astype(BF16))
    return o, st_new


def _mix_in_kernel(*refs, nseq, rows, chunk, sub, per_seq_state, emit_attn_kv, q_scale):
    it = iter(refs)
    x_ref = next(it); mod_ref = next(it); nw_ref = next(it); win_ref = next(it); lb_ref = next(it)
    hgw_ref = next(it); qnw_ref = next(it); knw_ref = next(it)
    s0_ref = next(it) if per_seq_state else None
    ya_ref = next(it); q_ref = next(it); kf_ref = next(it); vf_ref = next(it); gb_ref = next(it)
    kb_ref = next(it) if emit_attn_kv else None
    vt_ref = next(it) if emit_attn_kv else None
    sfin_ref = next(it)
    st_scr = next(it); b_scr = next(it); q_scr = next(it); k_scr = next(it); v_scr = next(it); o_scr = next(it)
    f_scr = next(it)

    tm = nseq * rows
    d = x_ref.shape[-1]
    n_chunks = tm // chunk
    blk = min(tm, MIX_BLOCK_ROWS)
    assert tm % blk == 0 and blk % chunk == 0

    x3 = x_ref[...]
    ms = jnp.mean(x3 * x3, axis=-1, keepdims=True)
    shift = mod_ref[:, :, 0:d]
    scale = mod_ref[:, :, d:2 * d]
    h3 = x3 * lax.rsqrt(ms + EPS) * nw_ref[...] * (1.0 + scale) + shift
    h = h3.reshape(tm, d).astype(BF16)

    def proj(idx):
        return _dot(h, win_ref[:, idx * GROUP_W:(idx + 1) * GROUP_W])

    def group_b():
        gw = 2 * LANES
        gi = _div_pow2(lax.broadcasted_iota(jnp.int32, (gw, gw), 0), HALF)
        gj = _div_pow2(lax.broadcasted_iota(jnp.int32, (gw, gw), 1), HALF)
        gmat = jnp.where(gi == gj, 1.0 / HALF, 0.0).astype(BF16)

        qb = proj(5)
        qn = qb * lax.rsqrt(_group_mean_sq(qb, gmat) + EPS) * qnw_ref[...] * q_scale
        q_ref[...] = qn.reshape(nseq, rows, GROUP_W).astype(q_ref.dtype)
        kb = proj(6)
        kn = kb * lax.rsqrt(_group_mean_sq(kb, gmat) + EPS) * knw_ref[...]
        vb = proj(7)
        for hh in range(N_HEADS):
            hs = slice(hh * HEAD_W, (hh + 1) * HEAD_W)
            kf_ref[:, pl.ds(hh, rows, stride=N_HEADS), :] = kn[:, hs].reshape(nseq, rows, HEAD_W)
            vf_ref[:, pl.ds(hh, rows, stride=N_HEADS), :] = vb[:, hs].reshape(nseq, rows, HEAD_W)
        if emit_attn_kv:
            kb_ref[...] = kn.reshape(nseq, rows, GROUP_W).astype(BF16)
            vt_tile = vt_ref.shape[-1]
            for hh in range(N_HEADS):
                for j in range(tm // vt_tile):
                    vt_ref[0, hh, j, 0:HEAD_W, :] = (
                        vb[j * vt_tile:(j + 1) * vt_tile, hh * HEAD_W:(hh + 1) * HEAD_W].T.astype(BF16))
                    vt_ref[0, hh, j, HEAD_W:VT_ROWS, :] = jnp.ones((VT_ROWS - HEAD_W, vt_tile), BF16)
        zb = proj(8)
        sgz, _ = _sigmoid_pair(zb)
        gb_ref[...] = (zb * sgz).reshape(nseq, rows, GROUP_W)

    lb = lb_ref[...]
    fa = proj(1)
    sig, nsig = _sigmoid_pair(fa)
    f_gate = lb + (1.0 - lb) * sig
    f_scr[...] = f_gate
    logf = jnp.log(f_gate)
    k_scr[...] = (1.0 - lb) * nsig
    q_scr[...] = proj(0) * (HEAD_W ** -0.5)
    v_scr[...] = proj(2)
    ri = lax.broadcasted_iota(jnp.int32, (blk, blk), 0)
    ci = lax.broadcasted_iota(jnp.int32, (blk, blk), 1)
    same_chunk = _div_pow2(ri, chunk) == _div_pow2(ci, chunk)
    tri = jnp.where(ci <= ri, jnp.where(same_chunk, 1.0, 0.0), 0.0).astype(BF16)
    l1, l2 = _split2(logf)
    for r0 in range(0, tm, blk):
        b_scr[r0:r0 + blk, :] = _dot(tri, l1[r0:r0 + blk, :]) + _dot(tri, l2[r0:r0 + blk, :])

    if not per_seq_state:
        @pl.when(pl.program_id(1) == 0)
        def _():
            st_scr[...] = jnp.zeros_like(st_scr)

    near_rows = blk
    masks = _hgrn_masks(chunk, sub, near_rows)
    for hh in range(N_HEADS):
        hs = slice(hh * HEAD_W, (hh + 1) * HEAD_W)
        for r0 in range(0, tm, near_rows):
            nr = slice(r0, r0 + near_rows)
            o_scr[nr, hs] = _hgrn_near(b_scr[nr, hs], f_scr[nr, hs], q_scr[nr, hs], k_scr[nr, hs], v_scr[nr, hs],
                                       masks, sub)
        st = None if per_seq_state else st_scr[hh]
        for c in range(n_chunks):
            rs = slice(c * chunk, (c + 1) * chunk)
            if per_seq_state:
                st = s0_ref[c, hh].T
            o, st = _hgrn_chunk(b_scr[rs, hs], q_scr[rs, hs], k_scr[rs, hs], v_scr[rs, hs], st, masks["wide"],
                                chunk, sub)
            o_scr[rs, hs] = o_scr[rs, hs] + o
            if per_seq_state:
                sfin_ref[c, hh] = st.T
        if not per_seq_state:
            st_scr[hh] = st

    group_b()

    if not per_seq_state:
        @pl.when(pl.program_id(1) == pl.num_programs(1) - 1)
        def _():
            for hh in range(N_HEADS):
                sfin_ref[0, hh] = st_scr[hh].T

    sg, _ = _sigmoid_pair(proj(3))
    za = proj(4)
    sz, _ = _sigmoid_pair(za)
    outs = []
    for hh in range(N_HEADS):
        hs = slice(hh * HEAD_W, (hh + 1) * HEAD_W)
        oh = o_scr[:, hs] * sg[:, hs]
        mo = jnp.mean(oh * oh, axis=-1, keepdims=True)
        outs.append(oh * lax.rsqrt(mo + EPS))
    ya = jnp.concatenate(outs, axis=1) * hgw_ref[...] * (za * sz)
    ya_ref[...] = ya.reshape(nseq, rows, GROUP_W).astype(ya_ref.dtype)


def _mix_in(x, mod, norm_w, w_in, lb, hgw, qnw, knw, *, nseq, rows, chunk, sub, s0, attn_tile, q_dtype, q_scale):
    n_s, r_total, d = x.shape
    per_seq_state = s0 is not None
    emit = attn_tile is not None
    tm = nseq * rows
    if per_seq_state:
        assert rows == r_total and rows == chunk and n_s % nseq == 0
        grid = (n_s // nseq,)
        tok = lambda i: (i, 0, 0)
        seq = lambda i: (i, 0, 0)
        const2 = lambda i: (0, 0)
        st_map = lambda i: (i, 0, 0, 0)
        sem = ("arbitrary",)
    else:
        assert nseq == 1 and r_total % rows == 0 and rows % chunk == 0
        grid = (n_s, r_total // rows)
        tok = lambda b, t: (b, t, 0)
        seq = lambda b, t: (b, 0, 0)
        const2 = lambda b, t: (0, 0)
        st_map = lambda b, t: (b, 0, 0, 0)
        sem = ("parallel", "arbitrary")
    if emit:
        assert rows % attn_tile == 0
    d_in = w_in.shape[1]
    tok_spec = pl.BlockSpec((nseq, rows, GROUP_W), tok)
    tokhead_spec = pl.BlockSpec((nseq, rows * N_HEADS, HEAD_W), tok)
    in_specs = [pl.BlockSpec((nseq, rows, d), tok),
                pl.BlockSpec((nseq, 1, 3 * d), seq),
                pl.BlockSpec((1, d), const2),
                pl.BlockSpec((d, d_in), const2),
                pl.BlockSpec((1, GROUP_W), const2),
                pl.BlockSpec((1, GROUP_W), const2),
                pl.BlockSpec((1, GROUP_W), const2),
                pl.BlockSpec((1, GROUP_W), const2)]
    args = [x, mod, norm_w, w_in, lb, hgw, qnw, knw]
    if per_seq_state:
        in_specs.append(pl.BlockSpec((nseq, N_HEADS, HEAD_W, HEAD_W), st_map))
        args.append(s0)
    out_shape = [jax.ShapeDtypeStruct((n_s, r_total, GROUP_W), q_dtype),
                 jax.ShapeDtypeStruct((n_s, r_total, GROUP_W), q_dtype),
                 jax.ShapeDtypeStruct((n_s, r_total * N_HEADS, HEAD_W), F32),
                 jax.ShapeDtypeStruct((n_s, r_total * N_HEADS, HEAD_W), F32),
                 jax.ShapeDtypeStruct((n_s, r_total, GROUP_W), F32)]
    out_specs = [tok_spec, tok_spec, tokhead_spec, tokhead_spec, tok_spec]
    if emit:
        nt = r_total // attn_tile
        out_shape += [jax.ShapeDtypeStruct((n_s, r_total, GROUP_W), BF16),
                      jax.ShapeDtypeStruct((n_s, N_HEADS, nt, VT_ROWS, attn_tile), BF16)]
        out_specs += [tok_spec,
                      pl.BlockSpec((1, N_HEADS, rows // attn_tile, VT_ROWS, attn_tile),
                                   lambda b, t: (b, 0, t, 0, 0))]
    out_shape.append(jax.ShapeDtypeStruct((n_s, N_HEADS, HEAD_W, HEAD_W), F32))
    out_specs.append(pl.BlockSpec((nseq, N_HEADS, HEAD_W, HEAD_W), st_map))
    scratch = [pltpu.VMEM((N_HEADS, HEAD_W, HEAD_W), F32)] + [pltpu.VMEM((tm, GROUP_W), F32)] * 6
    kern = functools.partial(_mix_in_kernel, nseq=nseq, rows=rows, chunk=chunk, sub=sub,
                             per_seq_state=per_seq_state, emit_attn_kv=emit, q_scale=q_scale)
    return pl.pallas_call(
        kern, out_shape=tuple(out_shape), grid=grid, in_specs=in_specs, out_specs=tuple(out_specs),
        scratch_shapes=scratch, compiler_params=_cparams(sem),
        name="mix_in_sample" if per_seq_state else "mix_in_prompt",
    )(*args)


def _sub_norm_gate(o, snw, gate):
    ms = jnp.mean(o * o, axis=-1, keepdims=True)
    return o * lax.rsqrt(ms + EPS) * snw * (1.0 - LAMBDA_INIT) * gate


def _attn_kernel(lam_ref, q_ref, k_ref, vt_ref, bias_ref, g_ref, snw_ref, o_ref,
                 qs_scr, sa_scr, sb_scr, m_scr, acc_scr, *, tq, tk, n_bias):
    qi = pl.program_id(2)
    ratio = tq // tk
    n_tiles = ratio * (qi + 1)
    n_heads = qs_scr.shape[0]
    heads = range(n_heads)
    lane = lax.broadcasted_iota(jnp.int32, (tq, HEAD_W), 1)
    for hh in heads:
        q = q_ref[0, :, hh * HEAD_W:(hh + 1) * HEAD_W].astype(F32)
        qs_scr[hh, 0:tq, :] = jnp.where(lane < HALF, q, 0.0).astype(BF16)
        qs_scr[hh, tq:2 * tq, :] = jnp.where(lane >= HALF, q, 0.0).astype(BF16)
    m_scr[...] = jnp.full(m_scr.shape, NEG, F32)
    acc_scr[...] = jnp.zeros_like(acc_scr)

    def scores(hh, t, s_ref, q0=0):
        r0 = pl.multiple_of(t * tk, tk)
        kt = k_ref[0, pl.ds(r0, tk), hh * HEAD_W:(hh + 1) * HEAD_W]
        bt = bias_ref[hh, jnp.clip(ratio * qi - t + (ratio - 1), 0, n_bias - 1)]
        for c in range(2):
            cols = slice(c * tq + q0, (c + 1) * tq)
            s_ref[hh, :, cols] = _dot_nt(kt, qs_scr[hh, cols, :]) + bt[:, q0:tq]

    def softmax_pv(hh, t, s_ref, q0=0):
        vt = vt_ref[0, hh, t]
        for c in range(2):
            cols = slice(c * tq + q0, (c + 1) * tq)
            m_old = m_scr[hh, :, cols]
            m_new = jnp.maximum(m_old, jnp.max(s_ref[hh, :, cols], axis=0, keepdims=True))
            alpha = jnp.exp2(m_old - m_new)
            p = jnp.exp2(s_ref[hh, :, cols] - m_new)
            acc_scr[hh, :, cols] = acc_scr[hh, :, cols] * alpha + _dot(vt, p.astype(BF16))
            m_scr[hh, :, cols] = m_new

    for hh in heads:
        scores(hh, 0, sa_scr)

    def pair(j):
        t0 = 2 * j
        for hh in heads:
            scores(hh, t0 + 1, sb_scr)
        for hh in heads:
            softmax_pv(hh, t0, sa_scr)
        for hh in heads:
            scores(hh, t0 + 2, sa_scr)
        for hh in heads:
            softmax_pv(hh, t0 + 1, sb_scr)

    def body(jj, carry):
        pair(2 * jj)
        pair(2 * jj + 1)
        return carry

    lax.fori_loop(0, qi // 2, body, 0)

    @pl.when(qi % 2 == 1)
    def _():
        pair(qi - 1)

    q_last = tq - tk
    for hh in heads:
        scores(hh, n_tiles - 1, sb_scr, q_last)
    for hh in heads:
        softmax_pv(hh, n_tiles - 2, sa_scr)
    for hh in heads:
        softmax_pv(hh, n_tiles - 1, sb_scr, q_last)

    lam = lam_ref[0, 0]
    for hh in heads:
        hs = slice(hh * HEAD_W, (hh + 1) * HEAD_W)
        inv = 1.0 / acc_scr[hh, HEAD_W:HEAD_W + 1, :]
        acc = acc_scr[hh, 0:HEAD_W, :]
        o_t = acc[:, 0:tq] * inv[:, 0:tq] - lam * (acc[:, tq:2 * tq] * inv[:, tq:2 * tq])
        o_n = o_t * lax.rsqrt(jnp.mean(o_t * o_t, axis=0, keepdims=True) + EPS)
        o_ref[0, :, hs] = (o_n.T * snw_ref[...] * (1.0 - LAMBDA_INIT) * g_ref[0, :, hs]).astype(o_ref.dtype)


def _attn_prompt(lam, q, kb, vt, bias, gate, snw, tq, tk):
    b, l, _ = q.shape
    assert tq == 2 * tk and l % tq == 0
    nq = l // tq
    nt = l // tk
    nd = bias.shape[1]
    hp = ATTN_HEADS_PER_STEP
    assert N_HEADS % hp == 0
    return pl.pallas_call(
        functools.partial(_attn_kernel, tq=tq, tk=tk, n_bias=nd),
        out_shape=jax.ShapeDtypeStruct((b, l, GROUP_W), BF16),
        grid=(b, N_HEADS // hp, nq),
        in_specs=[pl.BlockSpec(memory_space=pltpu.SMEM),
                  pl.BlockSpec((1, tq, hp * HEAD_W), lambda bb, h, i: (bb, i, h)),
                  pl.BlockSpec((1, l, hp * HEAD_W), lambda bb, h, i: (bb, 0, h)),
                  pl.BlockSpec((1, hp, nt, VT_ROWS, tk), lambda bb, h, i: (bb, h, 0, 0, 0)),
                  pl.BlockSpec((hp, nd, tk, tq), lambda bb, h, i: (h, 0, 0, 0)),
                  pl.BlockSpec((1, tq, hp * HEAD_W), lambda bb, h, i: (bb, i, h)),
                  pl.BlockSpec((1, HEAD_W), lambda bb, h, i: (0, 0))],
        out_specs=pl.BlockSpec((1, tq, hp * HEAD_W), lambda bb, h, i: (bb, i, h)),
        scratch_shapes=[pltpu.VMEM((hp, 2 * tq, HEAD_W), BF16),
                        pltpu.VMEM((hp, tk, 2 * tq), F32),
                        pltpu.VMEM((hp, tk, 2 * tq), F32),
                        pltpu.VMEM((hp, 1, 2 * tq), F32),
                        pltpu.VMEM((hp, VT_ROWS, 2 * tq), F32)],
        compiler_params=_cparams(("parallel", "parallel", "arbitrary")),
        name="attn_prompt",
    )(lam, q, kb, vt, bias, gate, snw)


def _decode_kernel(pt_ref, lam_ref, q_ref, kn_ref, vn_ref, g_ref, snw_ref, bias_ref, biasn_ref, ck_hbm, cv_hbm,
                   o_ref, kbuf, vbuf, sem, qst_scr, m_scr, l_scr, acc_scr, kpg_scr, vpg_scr,
                   *, grp, ring, rows, page, n_seq, groups_per_seq):
    hr = 2 * rows
    total_groups = n_seq * groups_per_seq

    def group_copies(gidx, slot0):
        b = _div_pow2(gidx, groups_per_seq)
        p0 = jnp.bitwise_and(gidx, groups_per_seq - 1) * grp
        copies = []
        for j in range(grp):
            pid = pt_ref[b, p0 + j]
            copies.append(pltpu.make_async_copy(ck_hbm.at[pid], kbuf.at[slot0 + j], sem.at[0, slot0 + j]))
            copies.append(pltpu.make_async_copy(cv_hbm.at[pid], vbuf.at[slot0 + j], sem.at[1, slot0 + j]))
        return copies

    def start_group(gidx, slot0):
        for cp in group_copies(gidx, slot0):
            cp.start()

    def wait_group(gidx, slot0):
        for cp in group_copies(gidx, slot0):
            cp.wait()

    def head_page(buf, slot, hh):
        return buf[slot, pl.ds(hh, page, stride=N_HEADS), :].astype(BF16)

    def init_sequence(b):
        q = q_ref[b].astype(F32)
        lane = lax.broadcasted_iota(jnp.int32, (rows, HEAD_W), 1)
        kpg_scr[...] = jnp.zeros_like(kpg_scr)
        vpg_scr[...] = jnp.zeros_like(vpg_scr)
        for hh in range(N_HEADS):
            qh = q[:, hh * HEAD_W:(hh + 1) * HEAD_W]
            qst_scr[hh] = jnp.concatenate([jnp.where(lane < HALF, qh, 0.0), jnp.where(lane >= HALF, qh, 0.0)],
                                          axis=0).astype(BF16)
            kpg_scr[hh, 0:rows, :] = kn_ref[b, pl.ds(hh, rows, stride=N_HEADS), :]
            vpg_scr[hh, 0:rows, :] = vn_ref[b, pl.ds(hh, rows, stride=N_HEADS), :]
        m_scr[...] = jnp.full(m_scr.shape, NEG, F32)
        l_scr[...] = jnp.zeros_like(l_scr)
        acc_scr[...] = jnp.zeros_like(acc_scr)

    def update(s, n_blocks, value_fn):
        m_old = m_scr[...]
        m_new = jnp.maximum(m_old, jnp.max(s, axis=-1, keepdims=True))
        alpha = jnp.exp(m_old - m_new)
        p = jnp.exp(s - m_new)
        l_scr[...] = alpha * l_scr[...] + jnp.sum(p, axis=-1, keepdims=True)
        pb = p.astype(BF16)
        pv_rows = []
        for hh in range(N_HEADS):
            ph = pb[hh * hr:(hh + 1) * hr, :]
            pv = _dot(ph[:, 0:page], value_fn(hh, 0))
            for j in range(1, n_blocks):
                pv = pv + _dot(ph[:, j * page:(j + 1) * page], value_fn(hh, j))
            pv_rows.append(pv)
        acc_scr[...] = acc_scr[...] * alpha + jnp.concatenate(pv_rows, axis=0)
        m_scr[...] = m_new

    def scores(n_blocks, key_fn):
        s_rows = []
        for hh in range(N_HEADS):
            qh = qst_scr[hh]
            blocks = [_dot_nt(qh, key_fn(hh, j)) for j in range(n_blocks)]
            s_rows.append(blocks[0] if n_blocks == 1 else jnp.concatenate(blocks, axis=1))
        return jnp.concatenate(s_rows, axis=0)

    def consume_group(gs, slot0):
        s = scores(grp, lambda hh, j: head_page(kbuf, slot0 + j, hh)) + bias_ref[gs]
        update(s, grp, lambda hh, j: head_page(vbuf, slot0 + j, hh))

    def finish_sequence(b):
        s_new = scores(1, lambda hh, j: kpg_scr[hh].astype(BF16)) + biasn_ref[...]
        update(s_new, 1, lambda hh, j: vpg_scr[hh].astype(BF16))
        lam = lam_ref[0, 0]
        inv = 1.0 / l_scr[...]
        acc = acc_scr[...] * inv
        gate = g_ref[b]
        for hh in range(N_HEADS):
            hs = slice(hh * HEAD_W, (hh + 1) * HEAD_W)
            r0 = hh * hr
            o = acc[r0:r0 + rows, :] - lam * acc[r0 + rows:r0 + 2 * rows, :]
            o_ref[b, :, hs] = _sub_norm_gate(o, snw_ref[...], gate[:, hs]).astype(o_ref.dtype)

    for part in range(ring):
        start_group(part, part * grp)

    def seq_body(b, carry):
        init_sequence(b)

        def ring_body(it, c):
            g0 = b * groups_per_seq + ring * it
            for part in range(ring):
                gidx = g0 + part
                slot0 = part * grp
                wait_group(gidx, slot0)
                consume_group(ring * it + part, slot0)

                @pl.when(gidx + ring < total_groups)
                def _():
                    start_group(gidx + ring, slot0)
            return c

        lax.fori_loop(0, groups_per_seq // ring, ring_body, 0)
        finish_sequence(b)
        return carry

    lax.fori_loop(0, n_seq, seq_body, 0)


def _attn_decode(page_table, lam, q, k_new, v_new, gate, snw, bias_dec, bias_new, cache_k, cache_v, n_pages):
    nb, rows, _ = q.shape
    pages_per_seq = page_table.shape[1]
    page = cache_k.shape[1] // N_HEADS
    grp = n_pages
    groups_per_seq = pages_per_seq // grp
    assert pages_per_seq % grp == 0 and page == LANES
    ring = min(DECODE_RING_GROUPS, groups_per_seq)
    assert groups_per_seq % ring == 0 and groups_per_seq & (groups_per_seq - 1) == 0
    assert bias_dec.shape == (groups_per_seq, 2 * N_HEADS * rows, grp * page)
    n_rows = 2 * N_HEADS * rows
    smem = pl.BlockSpec(memory_space=pltpu.SMEM)
    vmem = pl.BlockSpec(memory_space=pltpu.VMEM)
    hbm = pl.BlockSpec(memory_space=pl.ANY)
    return pl.pallas_call(
        functools.partial(_decode_kernel, grp=grp, ring=ring, rows=rows, page=page, n_seq=nb,
                          groups_per_seq=groups_per_seq),
        out_shape=jax.ShapeDtypeStruct((nb, rows, GROUP_W), F32),
        in_specs=[smem, smem, vmem, vmem, vmem, vmem, vmem, vmem, vmem, hbm, hbm],
        out_specs=vmem,
        scratch_shapes=[pltpu.VMEM((ring * grp, page * N_HEADS, HEAD_W), F32),
                        pltpu.VMEM((ring * grp, page * N_HEADS, HEAD_W), F32),
                        pltpu.SemaphoreType.DMA((2, ring * grp)),
                        pltpu.VMEM((N_HEADS, 2 * rows, HEAD_W), BF16),
                        pltpu.VMEM((n_rows, 1), F32),
                        pltpu.VMEM((n_rows, 1), F32),
                        pltpu.VMEM((n_rows, HEAD_W), F32),
                        pltpu.VMEM((N_HEADS, page, HEAD_W), F32),
                        pltpu.VMEM((N_HEADS, page, HEAD_W), F32)],
        compiler_params=pltpu.CompilerParams(vmem_limit_bytes=VMEM_LIMIT_BYTES),
        name="attn_decode",
    )(page_table, lam, q, k_new, v_new, gate, snw, bias_dec, bias_new, cache_k, cache_v)


def _mix_out_kernel(ya_ref, yb_ref, x_ref, mod_ref, w_ref, o_ref, *, nseq, rows):
    d = x_ref.shape[-1]
    tm = nseq * rows
    ya = ya_ref[...].reshape(tm, GROUP_W).astype(BF16)
    yb = yb_ref[...].reshape(tm, GROUP_W).astype(BF16)
    out = _dot(ya, w_ref[0:GROUP_W, :]) + _dot(yb, w_ref[GROUP_W:2 * GROUP_W, :])
    gate = mod_ref[:, :, 2 * d:3 * d]
    o_ref[...] = x_ref[...] + gate * out.reshape(nseq, rows, d)


def _mix_out(ya, yb, x, mod, w_out, *, nseq, rows):
    n_s, r_total, d = x.shape
    grid = (n_s // nseq, r_total // rows)
    tok = lambda s, t: (s, t, 0)
    return pl.pallas_call(
        functools.partial(_mix_out_kernel, nseq=nseq, rows=rows),
        out_shape=jax.ShapeDtypeStruct(x.shape, F32),
        grid=grid,
        in_specs=[pl.BlockSpec((nseq, rows, GROUP_W), tok),
                  pl.BlockSpec((nseq, rows, GROUP_W), tok),
                  pl.BlockSpec((nseq, rows, d), tok),
                  pl.BlockSpec((nseq, 1, 3 * d), lambda s, t: (s, 0, 0)),
                  pl.BlockSpec((2 * GROUP_W, d), lambda s, t: (0, 0))],
        out_specs=pl.BlockSpec((nseq, rows, d), tok),
        compiler_params=_cparams(("parallel", "arbitrary")),
        name="mix_out",
    )(ya, yb, x, mod, w_out)


def _largest_tile(n, cap):
    t = cap
    while n % t:
        t //= 2
    return t


def kernel(x_prompt, x_sample, c_prompt, c_sample, cache_k, cache_v, state_hgrn, page_table, norm_w, w_ada,
           b_ada, w_in, w_out, lb_param, hg_norm_w, q_norm_w, k_norm_w, lam_q1, lam_k1, lam_q2, lam_k2,
           sub_norm_w, rel_table):
    depth = w_in.shape[0]
    assert depth == 1
    b, l, d = x_prompt.shape
    db, dl, _ = x_sample.shape
    assert d == 2 * GROUP_W and w_in.shape[2] == 9 * GROUP_W
    page = cache_k.shape[2]
    past = page_table.shape[1] * page
    assert dl % SUBLANES == 0 and cache_k.shape[3] * cache_k.shape[4] == GROUP_W

    w_in_b = w_in[0].astype(BF16)
    w_out_b = w_out[0].astype(BF16)
    nw = norm_w.reshape(1, d)
    hgw = jnp.tile(hg_norm_w.reshape(1, HEAD_W), (1, N_HEADS))
    qnw = jnp.tile(q_norm_w.reshape(1, HALF), (1, GROUP_W // HALF))
    knw = jnp.tile(k_norm_w.reshape(1, HALF), (1, GROUP_W // HALF))
    snw = sub_norm_w.reshape(1, HEAD_W)

    mod = _adaln_mod(jnp.concatenate([c_prompt, c_sample], axis=0), w_ada[0], b_ada[0])
    mod_p = mod[:b].reshape(b, 1, 3 * d)
    mod_s = mod[b:].reshape(db, 1, 3 * d)
    lb, lam = _small_params(lb_param, lam_q1, lam_k1, lam_q2, lam_k2)

    log2e = math.log2(math.e)
    ya, q, kf, vf, gb, kb, vt, st_p = _mix_in(
        x_prompt, mod_p, nw, w_in_b, lb, hgw, qnw, knw, nseq=1, rows=_largest_tile(l, MIX_ROWS),
        chunk=CHUNK if l % CHUNK == 0 else l, sub=SUB, s0=None, attn_tile=ATTN_TK, q_dtype=BF16,
        q_scale=HALF ** -0.5 * log2e)
    bias_p = _bias_prompt(rel_table, ATTN_TK, ATTN_TQ, log2e)
    yb = _attn_prompt(lam, q, kb, vt, bias_p, gb, snw, ATTN_TQ, ATTN_TK)
    y_prompt = _mix_out(ya, yb, x_prompt, mod_p, w_out_b, nseq=1, rows=_largest_tile(l, MIX_OUT_ROWS))

    nseq = _largest_tile(db, 8)
    ya_s, q_s, kf_s, vf_s, gb_s, st_s = _mix_in(
        x_sample, mod_s, nw, w_in_b, lb, hgw, qnw, knw, nseq=nseq, rows=dl,
        chunk=dl, sub=dl, s0=state_hgrn[0], attn_tile=None, q_dtype=F32, q_scale=HALF ** -0.5)
    grp = _largest_tile(page_table.shape[1] // 2, DECODE_GROUP_PAGES)
    bias_d, bias_n = _bias_decode(rel_table, dl, past, grp * page)
    n_rows = 2 * N_HEADS * dl
    yb_s = _attn_decode(page_table, lam, q_s, kf_s, vf_s, gb_s, snw,
                        bias_d.reshape(-1, n_rows, grp * page), bias_n.reshape(n_rows, LANES),
                        cache_k.reshape(-1, page * N_HEADS, HEAD_W), cache_v.reshape(-1, page * N_HEADS, HEAD_W),
                        n_pages=grp)
    y_sample = _mix_out(ya_s, yb_s, x_sample, mod_s, w_out_b, nseq=nseq, rows=dl)

    hb = N_HEADS
    return (y_prompt, y_sample,
            kf.reshape(1, b, l, hb, HEAD_W), vf.reshape(1, b, l, hb, HEAD_W), st_p[None],
            kf_s.reshape(1, db, dl, hb, HEAD_W), vf_s.reshape(1, db, dl, hb, HEAD_W), st_s[None])
```

```python
import functools
import math

import jax
import jax.numpy as jnp
from jax import lax
from jax.experimental import pallas as pl
from jax.experimental.pallas import tpu as pltpu

F32 = jnp.float32
BF16 = jnp.bfloat16

N_HEADS = 4
HEAD_W = 128
HALF = 64
GROUP_W = N_HEADS * HEAD_W
N_BUCKETS = 32
MAX_EXACT = N_BUCKETS // 2
MAX_DIST = 1024
CHUNK = 64
SUB = 16
EPS = 1e-6
LAMBDA_INIT = 0.8 - 0.6 * math.exp(-0.3 * 0)
NEG = -1e30
ATTN_TQ = 512
ATTN_TK = 256
MIX_ROWS = 1024
MIX_BLOCK_ROWS = 256
MIX_OUT_ROWS = 1024
ATTN_HEADS_PER_STEP = 1
VT_ROWS = HEAD_W + 16
DECODE_GROUP_PAGES = 8
DECODE_RING_GROUPS = 4

LANES = 128
SUBLANES = 8
VMEM_LIMIT_BYTES = 56 * 1024 * 1024


def _bucket_lower_bounds():
    los = list(range(MAX_EXACT))
    span = N_BUCKETS - MAX_EXACT
    ratio = MAX_DIST // MAX_EXACT
    for m in range(MAX_EXACT, N_BUCKETS):
        d = MAX_EXACT
        while d ** span < (MAX_EXACT ** span) * (ratio ** (m - MAX_EXACT)):
            d += 1
        los.append(d)
    return los


BUCKET_LO = _bucket_lower_bounds()
SATURATED_DIST = BUCKET_LO[-1]


def _cparams(sem):
    return pltpu.CompilerParams(dimension_semantics=sem, vmem_limit_bytes=VMEM_LIMIT_BYTES)


def _split2(x):
    hi = x.astype(BF16)
    lo = (x - hi.astype(F32)).astype(BF16)
    return hi, lo


def _dot(a, b):
    return jnp.dot(a, b, preferred_element_type=F32)


def _dot_nt(a, b):
    return lax.dot_general(a, b, (((1,), (1,)), ((), ())), preferred_element_type=F32)


def _dot_tn(a, b):
    return lax.dot_general(a, b, (((0,), (0,)), ((), ())), preferred_element_type=F32)


def _div_pow2(x, n):
    assert n > 0 and n & (n - 1) == 0
    return lax.shift_right_logical(x, n.bit_length() - 1)


def _sigmoid_pair(x):
    t = jnp.exp(-jnp.abs(x))
    r = 1.0 / (1.0 + t)
    tr = t * r
    pos = x >= 0
    return jnp.where(pos, r, tr), jnp.where(pos, tr, r)


def _mod_kernel(c_ref, w_ref, b_ref, o_ref):
    c = c_ref[...]
    sig, _ = _sigmoid_pair(c)
    s = c * sig
    w = w_ref[...]
    s1, s2 = _split2(s)
    w1, w2 = _split2(w)
    o_ref[...] = _dot(s1, w1) + (_dot(s1, w2) + _dot(s2, w1)) + b_ref[...]


def _adaln_mod(c_all, w_ada, b_ada):
    n, d = c_all.shape
    d3 = w_ada.shape[1]
    bn = d
    return pl.pallas_call(
        _mod_kernel,
        out_shape=jax.ShapeDtypeStruct((n, d3), F32),
        grid=(d3 // bn,),
        in_specs=[pl.BlockSpec((n, d), lambda j: (0, 0)),
                  pl.BlockSpec((d, bn), lambda j: (0, j)),
                  pl.BlockSpec((1, bn), lambda j: (0, j))],
        out_specs=pl.BlockSpec((n, bn), lambda j: (0, j)),
        compiler_params=_cparams(("arbitrary",)),
        name="adaln_mod",
    )(c_all, w_ada, b_ada.reshape(1, d3))


def _small_kernel(lbp_ref, q1_ref, k1_ref, q2_ref, k2_ref, lb_ref, lam_ref):
    p = lbp_ref[...]
    m = jnp.max(p, axis=0, keepdims=True)
    e = jnp.exp(p - m)
    lb_ref[...] = e[0:1, :] / jnp.sum(e, axis=0, keepdims=True)
    s1 = jnp.sum(q1_ref[...] * k1_ref[...], axis=-1, keepdims=True)
    s2 = jnp.sum(q2_ref[...] * k2_ref[...], axis=-1, keepdims=True)
    lam_ref[...] = jnp.exp(s1) - jnp.exp(s2) + LAMBDA_INIT


def _small_params(lb_param, lq1, lk1, lq2, lk2):
    return pl.pallas_call(
        _small_kernel,
        out_shape=(jax.ShapeDtypeStruct((1, lb_param.shape[1]), F32),
                   jax.ShapeDtypeStruct((1, 1), F32)),
        name="small_params",
    )(lb_param, lq1, lk1, lq2, lk2)


def _bucket_of(d):
    return max(m for m in range(N_BUCKETS) if d >= BUCKET_LO[m])


def _bias_select(d, tab_ref, h, d_min, d_max):
    m_lo, m_hi = _bucket_of(d_min), _bucket_of(d_max)
    val = jnp.full(d.shape, tab_ref[m_hi, h], F32)
    for m in range(m_hi - 1, m_lo - 1, -1):
        val = jnp.where(d < BUCKET_LO[m + 1], tab_ref[m, h], val)
    return val


def _bias_prompt_kernel(tab_ref, o_ref, *, tk, tq, scale):
    h = pl.program_id(0)
    kk = lax.broadcasted_iota(jnp.int32, (tk, tq), 0)
    qq = lax.broadcasted_iota(jnp.int32, (tk, tq), 1)
    for idx in range(o_ref.shape[1]):
        dl = idx - (tq // tk - 1)
        d = dl * tk + qq - kk
        d_min, d_max = dl * tk - (tk - 1), dl * tk + tq - 1
        val = _bias_select(jnp.maximum(d, 0), tab_ref, h, max(d_min, 0), max(d_max, 0)) * scale
        o_ref[0, idx] = jnp.where(d >= 0, val, NEG) if d_min < 0 else val


def _n_bias_tiles(tk, tq):
    n_const = 0
    while n_const * tk - (tk - 1) < SATURATED_DIST:
        n_const += 1
    return n_const + 1 + (tq // tk - 1)


def _bias_prompt(rel_table, tk, tq, scale):
    nd = _n_bias_tiles(tk, tq)
    return pl.pallas_call(
        functools.partial(_bias_prompt_kernel, tk=tk, tq=tq, scale=scale),
        out_shape=jax.ShapeDtypeStruct((N_HEADS, nd, tk, tq), F32),
        grid=(N_HEADS,),
        in_specs=[pl.BlockSpec(memory_space=pltpu.SMEM)],
        out_specs=pl.BlockSpec((1, nd, tk, tq), lambda h: (h, 0, 0, 0)),
        compiler_params=_cparams(("arbitrary",)),
        name="bias_prompt",
    )(rel_table)


def _bias_decode_kernel(tab_ref, o_ref, on_ref, *, rows, past, width):
    h = pl.program_id(0)
    i = lax.broadcasted_iota(jnp.int32, (rows, width), 0)
    lane = lax.broadcasted_iota(jnp.int32, (rows, width), 1)
    for g in range(past // width):
        d_min, d_max = past - (g + 1) * width + 1, past - g * width + rows - 1
        val = _bias_select(past + i - (g * width + lane), tab_ref, h, d_min, d_max)
        o_ref[g, 0, 0] = val
        o_ref[g, 0, 1] = val
    i2 = lax.broadcasted_iota(jnp.int32, (rows, LANES), 0)
    j2 = lax.broadcasted_iota(jnp.int32, (rows, LANES), 1)
    d2 = i2 - j2
    vn = jnp.where(d2 >= 0, _bias_select(jnp.maximum(d2, 0), tab_ref, h, 0, rows - 1), NEG)
    on_ref[0, 0] = vn
    on_ref[0, 1] = vn


def _bias_decode(rel_table, rows, past, width):
    ng = past // width
    return pl.pallas_call(
        functools.partial(_bias_decode_kernel, rows=rows, past=past, width=width),
        out_shape=(jax.ShapeDtypeStruct((ng, N_HEADS, 2, rows, width), F32),
                   jax.ShapeDtypeStruct((N_HEADS, 2, rows, LANES), F32)),
        grid=(N_HEADS,),
        in_specs=[pl.BlockSpec(memory_space=pltpu.SMEM)],
        out_specs=(pl.BlockSpec((ng, 1, 2, rows, width), lambda h: (0, h, 0, 0, 0)),
                   pl.BlockSpec((1, 2, rows, LANES), lambda h: (h, 0, 0, 0))),
        compiler_params=_cparams(("arbitrary",)),
        name="bias_decode",
    )(rel_table)


def _group_mean_sq(x, gmat):
    sq = x * x
    outs = []
    w = gmat.shape[0]
    for c0 in range(0, x.shape[1], w):
        outs.append(_dot(sq[:, c0:c0 + w].astype(BF16), gmat))
    return jnp.concatenate(outs, axis=1)


def _hgrn_levels(sub):
    levels, h = [], sub // 2
    while h >= 1:
        levels.append(h)
        h //= 2
    return levels


def _hgrn_masks(chunk, sub, n):
    n_sub = chunk // sub
    masks = {"wide": None}
    if n_sub > 1:
        rr = lax.broadcasted_iota(jnp.int32, (chunk, n_sub * chunk), 0)
        cc = lax.broadcasted_iota(jnp.int32, (chunk, n_sub * chunk), 1)
        own_anchor = _div_pow2(cc, chunk) == _div_pow2(rr, sub)
        earlier = _div_pow2(jnp.bitwise_and(cc, chunk - 1), sub) < _div_pow2(rr, sub)
        masks["wide"] = jnp.where(own_anchor, jnp.where(earlier, 1.0, 0.0), 0.0)
    rr = lax.broadcasted_iota(jnp.int32, (n, n), 0)
    cc = lax.broadcasted_iota(jnp.int32, (n, n), 1)
    for h in _hgrn_levels(sub):
        same = _div_pow2(rr, 2 * h) == _div_pow2(cc, 2 * h)
        t_second = jnp.bitwise_and(rr, 2 * h - 1) >= h
        j_first = jnp.bitwise_and(cc, 2 * h - 1) < h
        masks[h] = jnp.where(same, jnp.where(t_second, jnp.where(j_first, 1.0, 0.0), 0.0), 0.0)
    masks["diag"] = jnp.where(rr == cc, 1.0, 0.0)
    return masks


def _level_boundary(bc, h, chunk):
    pieces = []
    if 2 * h >= SUBLANES:
        for blk in range(chunk // (2 * h)):
            r = blk * 2 * h + h - 1
            pieces.append(jnp.broadcast_to(bc[r:r + 1, :], (2 * h, HEAD_W)))
    else:
        per = SUBLANES // (2 * h)
        sub_row = lax.broadcasted_iota(jnp.int32, (SUBLANES, HEAD_W), 0)
        for g8 in range(chunk // SUBLANES):
            val = None
            for k in range(per):
                r = g8 * SUBLANES + k * 2 * h + h - 1
                row = jnp.broadcast_to(bc[r:r + 1, :], (SUBLANES, HEAD_W))
                val = row if val is None else jnp.where(sub_row >= k * 2 * h, row, val)
            pieces.append(val)
    return pieces[0] if len(pieces) == 1 else jnp.concatenate(pieces, axis=0)


def _hgrn_near(b, f, q, k, v, masks, sub):
    n = b.shape[0]
    kb16 = k.astype(BF16)
    att = _dot_nt(q.astype(BF16), kb16) * masks["diag"]
    odd = jnp.bitwise_and(lax.broadcasted_iota(jnp.int32, (n, HEAD_W), 0), 1) == 1
    for h in _hgrn_levels(sub):
        if h == 1:
            att_h = _dot_nt((q * jnp.where(odd, f, 1.0)).astype(BF16), kb16)
        else:
            w = jnp.exp(-jnp.abs(b - _level_boundary(b, h, n)))
            att_h = _dot_nt((q * w).astype(BF16), (k * w).astype(BF16))
        att = att + att_h * masks[h]
    return _dot(att.astype(BF16), v.astype(BF16))


def _hgrn_chunk(bc, qc, kc, vc, st, mask_wide, chunk, sub):
    n_sub = chunk // sub
    o = _dot_nt((qc * jnp.exp(bc)).astype(BF16), st.astype(BF16))
    if n_sub > 1:
        vb = vc.astype(BF16)
        q_parts, k_parts = [], []
        for i in range(n_sub):
            lo, hi = i * sub, (i + 1) * sub
            anchor = bc[max(lo - 1, 0):max(lo - 1, 0) + 1, :]
            q_parts.append(qc[lo:hi, :] * jnp.exp(jnp.minimum(bc[lo:hi, :] - anchor, 0.0)))
            k_parts.append((kc * jnp.exp(jnp.minimum(anchor - bc, 0.0))).astype(BF16))
        qa = jnp.concatenate(q_parts, axis=0).astype(BF16)
        att_w = _dot_nt(qa, jnp.concatenate(k_parts, axis=0)) * mask_wide
        o = o + _dot(att_w.astype(BF16), jnp.concatenate([vb] * n_sub, axis=0))
    b_last = bc[chunk - 1:chunk, :]
    ke = kc * jnp.exp(b_last - bc)
    st_new = st * jnp.exp(b_last) + _dot_tn(vc.astype(BF16), ke.astype(BF16))
    return o, st_new


def _mix_in_kernel(*refs, nseq, rows, chunk, sub, per_seq_state, emit_attn_kv, q_scale):
    it = iter(refs)
    x_ref = next(it); mod_ref = next(it); nw_ref = next(it); win_ref = next(it); lb_ref = next(it)
    hgw_ref = next(it); qnw_ref = next(it); knw_ref = next(it)
    s0_ref = next(it) if per_seq_state else None
    ya_ref = next(it); q_ref = next(it); kf_ref = next(it); vf_ref = next(it); gb_ref = next(it)
    kb_ref = next(it) if emit_attn_kv else None
    vt_ref = next(it) if emit_attn_kv else None
    sfin_ref = next(it)
    st_scr = next(it); b_scr = next(it); q_scr = next(it); k_scr = next(it); v_scr = next(it); o_scr = next(it)
    f_scr = next(it)

    tm = nseq * rows
    d = x_ref.shape[-1]
    n_chunks = tm // chunk
    blk = min(tm, MIX_BLOCK_ROWS)
    assert tm % blk == 0 and blk % chunk == 0

    x3 = x_ref[...]
    ms = jnp.mean(x3 * x3, axis=-1, keepdims=True)
    shift = mod_ref[:, :, 0:d]
    scale = mod_ref[:, :, d:2 * d]
    h3 = x3 * lax.rsqrt(ms + EPS) * nw_ref[...] * (1.0 + scale) + shift
    h = h3.reshape(tm, d).astype(BF16)

    def proj(idx):
        return _dot(h, win_ref[:, idx * GROUP_W:(idx + 1) * GROUP_W])

    def group_b():
        gw = 2 * LANES
        gi = _div_pow2(lax.broadcasted_iota(jnp.int32, (gw, gw), 0), HALF)
        gj = _div_pow2(lax.broadcasted_iota(jnp.int32, (gw, gw), 1), HALF)
        gmat = jnp.where(gi == gj, 1.0 / HALF, 0.0).astype(BF16)

        qb = proj(5)
        qn = qb * lax.rsqrt(_group_mean_sq(qb, gmat) + EPS) * qnw_ref[...] * q_scale
        q_ref[...] = qn.reshape(nseq, rows, GROUP_W).astype(q_ref.dtype)
        kb = proj(6)
        kn = kb * lax.rsqrt(_group_mean_sq(kb, gmat) + EPS) * knw_ref[...]
        vb = proj(7)
        for hh in range(N_HEADS):
            hs = slice(hh * HEAD_W, (hh + 1) * HEAD_W)
            kf_ref[:, pl.ds(hh, rows, stride=N_HEADS), :] = kn[:, hs].reshape(nseq, rows, HEAD_W)
            vf_ref[:, pl.ds(hh, rows, stride=N_HEADS), :] = vb[:, hs].reshape(nseq, rows, HEAD_W)
        if emit_attn_kv:
            kb_ref[...] = kn.reshape(nseq, rows, GROUP_W).astype(BF16)
            vt_tile = vt_ref.shape[-1]
            for hh in range(N_HEADS):
                for j in range(tm // vt_tile):
                    vt_ref[0, hh, j, 0:HEAD_W, :] = (
                        vb[j * vt_tile:(j + 1) * vt_tile, hh * HEAD_W:(hh + 1) * HEAD_W].T.astype(BF16))
                    vt_ref[0, hh, j, HEAD_W:VT_ROWS, :] = jnp.ones((VT_ROWS - HEAD_W, vt_tile), BF16)
        zb = proj(8)
        sgz, _ = _sigmoid_pair(zb)
        gb_ref[...] = (zb * sgz).reshape(nseq, rows, GROUP_W)

    lb = lb_ref[...]
    fa = proj(1)
    sig, nsig = _sigmoid_pair(fa)
    f_gate = lb + (1.0 - lb) * sig
    f_scr[...] = f_gate
    logf = jnp.log(f_gate)
    k_scr[...] = (1.0 - lb) * nsig
    q_scr[...] = proj(0) * (HEAD_W ** -0.5)
    v_scr[...] = proj(2)
    ri = lax.broadcasted_iota(jnp.int32, (blk, blk), 0)
    ci = lax.broadcasted_iota(jnp.int32, (blk, blk), 1)
    same_chunk = _div_pow2(ri, chunk) == _div_pow2(ci, chunk)
    tri = jnp.where(ci <= ri, jnp.where(same_chunk, 1.0, 0.0), 0.0).astype(BF16)
    l1, l2 = _split2(logf)
    for r0 in range(0, tm, blk):
        b_scr[r0:r0 + blk, :] = _dot(tri, l1[r0:r0 + blk, :]) + _dot(tri, l2[r0:r0 + blk, :])

    if not per_seq_state:
        @pl.when(pl.program_id(1) == 0)
        def _():
            st_scr[...] = jnp.zeros_like(st_scr)

    near_rows = blk
    masks = _hgrn_masks(chunk, sub, near_rows)
    for hh in range(N_HEADS):
        hs = slice(hh * HEAD_W, (hh + 1) * HEAD_W)
        for r0 in range(0, tm, near_rows):
            nr = slice(r0, r0 + near_rows)
            o_scr[nr, hs] = _hgrn_near(b_scr[nr, hs], f_scr[nr, hs], q_scr[nr, hs], k_scr[nr, hs], v_scr[nr, hs],
                                       masks, sub)
        st = None if per_seq_state else st_scr[hh]
        for c in range(n_chunks):
            rs = slice(c * chunk, (c + 1) * chunk)
            if per_seq_state:
                st = s0_ref[c, hh].T
            o, st = _hgrn_chunk(b_scr[rs, hs], q_scr[rs, hs], k_scr[rs, hs], v_scr[rs, hs], st, masks["wide"],
                                chunk, sub)
            o_scr[rs, hs] = o_scr[rs, hs] + o
            if per_seq_state:
                sfin_ref[c, hh] = st.T
        if not per_seq_state:
            st_scr[hh] = st

    group_b()

    if not per_seq_state:
        @pl.when(pl.program_id(1) == pl.num_programs(1) - 1)
        def _():
            for hh in range(N_HEADS):
                sfin_ref[0, hh] = st_scr[hh].T

    sg, _ = _sigmoid_pair(proj(3))
    za = proj(4)
    sz, _ = _sigmoid_pair(za)
    outs = []
    for hh in range(N_HEADS):
        hs = slice(hh * HEAD_W, (hh + 1) * HEAD_W)
        oh = o_scr[:, hs] * sg[:, hs]
        mo = jnp.mean(oh * oh, axis=-1, keepdims=True)
        outs.append(oh * lax.rsqrt(mo + EPS))
    ya = jnp.concatenate(outs, axis=1) * hgw_ref[...] * (za * sz)
    ya_ref[...] = ya.reshape(nseq, rows, GROUP_W).astype(ya_ref.dtype)


def _mix_in(x, mod, norm_w, w_in, lb, hgw, qnw, knw, *, nseq, rows, chunk, sub, s0, attn_tile, q_dtype, q_scale):
    n_s, r_total, d = x.shape
    per_seq_state = s0 is not None
    emit = attn_tile is not None
    tm = nseq * rows
    if per_seq_state:
        assert rows == r_total and rows == chunk and n_s % nseq == 0
        grid = (n_s // nseq,)
        tok = lambda i: (i, 0, 0)
        seq = lambda i: (i, 0, 0)
        const2 = lambda i: (0, 0)
        st_map = lambda i: (i, 0, 0, 0)
        sem = ("arbitrary",)
    else:
        assert nseq == 1 and r_total % rows == 0 and rows % chunk == 0
        grid = (n_s, r_total // rows)
        tok = lambda b, t: (b, t, 0)
        seq = lambda b, t: (b, 0, 0)
        const2 = lambda b, t: (0, 0)
        st_map = lambda b, t: (b, 0, 0, 0)
        sem = ("parallel", "arbitrary")
    if emit:
        assert rows % attn_tile == 0
    d_in = w_in.shape[1]
    tok_spec = pl.BlockSpec((nseq, rows, GROUP_W), tok)
    tokhead_spec = pl.BlockSpec((nseq, rows * N_HEADS, HEAD_W), tok)
    in_specs = [pl.BlockSpec((nseq, rows, d), tok),
                pl.BlockSpec((nseq, 1, 3 * d), seq),
                pl.BlockSpec((1, d), const2),
                pl.BlockSpec((d, d_in), const2),
                pl.BlockSpec((1, GROUP_W), const2),
                pl.BlockSpec((1, GROUP_W), const2),
                pl.BlockSpec((1, GROUP_W), const2),
                pl.BlockSpec((1, GROUP_W), const2)]
    args = [x, mod, norm_w, w_in, lb, hgw, qnw, knw]
    if per_seq_state:
        in_specs.append(pl.BlockSpec((nseq, N_HEADS, HEAD_W, HEAD_W), st_map))
        args.append(s0)
    out_shape = [jax.ShapeDtypeStruct((n_s, r_total, GROUP_W), q_dtype),
                 jax.ShapeDtypeStruct((n_s, r_total, GROUP_W), q_dtype),
                 jax.ShapeDtypeStruct((n_s, r_total * N_HEADS, HEAD_W), F32),
                 jax.ShapeDtypeStruct((n_s, r_total * N_HEADS, HEAD_W), F32),
                 jax.ShapeDtypeStruct((n_s, r_total, GROUP_W), F32)]
    out_specs = [tok_spec, tok_spec, tokhead_spec, tokhead_spec, tok_spec]
    if emit:
        nt = r_total // attn_tile
        out_shape += [jax.ShapeDtypeStruct((n_s, r_total, GROUP_W), BF16),
                      jax.ShapeDtypeStruct((n_s, N_HEADS, nt, VT_ROWS, attn_tile), BF16)]
        out_specs += [tok_spec,
                      pl.BlockSpec((1, N_HEADS, rows // attn_tile, VT_ROWS, attn_tile),
                                   lambda b, t: (b, 0, t, 0, 0))]
    out_shape.append(jax.ShapeDtypeStruct((n_s, N_HEADS, HEAD_W, HEAD_W), F32))
    out_specs.append(pl.BlockSpec((nseq, N_HEADS, HEAD_W, HEAD_W), st_map))
    scratch = [pltpu.VMEM((N_HEADS, HEAD_W, HEAD_W), F32)] + [pltpu.VMEM((tm, GROUP_W), F32)] * 6
    kern = functools.partial(_mix_in_kernel, nseq=nseq, rows=rows, chunk=chunk, sub=sub,
                             per_seq_state=per_seq_state, emit_attn_kv=emit, q_scale=q_scale)
    return pl.pallas_call(
        kern, out_shape=tuple(out_shape), grid=grid, in_specs=in_specs, out_specs=tuple(out_specs),
        scratch_shapes=scratch, compiler_params=_cparams(sem),
        name="mix_in_sample" if per_seq_state else "mix_in_prompt",
    )(*args)


def _sub_norm_gate(o, snw, gate):
    ms = jnp.mean(o * o, axis=-1, keepdims=True)
    return o * lax.rsqrt(ms + EPS) * snw * (1.0 - LAMBDA_INIT) * gate


def _attn_kernel(lam_ref, q_ref, k_ref, vt_ref, bias_ref, g_ref, snw_ref, o_ref,
                 qs_scr, sa_scr, sb_scr, m_scr, acc_scr, *, tq, tk, n_bias):
    qi = pl.program_id(2)
    ratio = tq // tk
    n_tiles = ratio * (qi + 1)
    n_heads = qs_scr.shape[0]
    heads = range(n_heads)
    lane = lax.broadcasted_iota(jnp.int32, (tq, HEAD_W), 1)
    for hh in heads:
        q = q_ref[0, :, hh * HEAD_W:(hh + 1) * HEAD_W].astype(F32)
        qs_scr[hh, 0:tq, :] = jnp.where(lane < HALF, q, 0.0).astype(BF16)
        qs_scr[hh, tq:2 * tq, :] = jnp.where(lane >= HALF, q, 0.0).astype(BF16)
    m_scr[...] = jnp.full(m_scr.shape, NEG, F32)
    acc_scr[...] = jnp.zeros_like(acc_scr)

    def scores(hh, t, s_ref, q0=0):
        r0 = pl.multiple_of(t * tk, tk)
        kt = k_ref[0, pl.ds(r0, tk), hh * HEAD_W:(hh + 1) * HEAD_W]
        bt = bias_ref[hh, jnp.clip(ratio * qi - t + (ratio - 1), 0, n_bias - 1)]
        for c in range(2):
            cols = slice(c * tq + q0, (c + 1) * tq)
            s_ref[hh, :, cols] = _dot_nt(kt, qs_scr[hh, cols, :]) + bt[:, q0:tq]

    def softmax_pv(hh, t, s_ref, q0=0):
        vt = vt_ref[0, hh, t]
        for c in range(2):
            cols = slice(c * tq + q0, (c + 1) * tq)
            m_old = m_scr[hh, :, cols]
            m_new = jnp.maximum(m_old, jnp.max(s_ref[hh, :, cols], axis=0, keepdims=True))
            alpha = jnp.exp2(m_old - m_new)
            p = jnp.exp2(s_ref[hh, :, cols] - m_new)
            acc_scr[hh, :, cols] = acc_scr[hh, :, cols] * alpha + _dot(vt, p.astype(BF16))
            m_scr[hh, :, cols] = m_new

    for hh in heads:
        scores(hh, 0, sa_scr)

    def pair(j):
        t0 = 2 * j
        for hh in heads:
            scores(hh, t0 + 1, sb_scr)
        for hh in heads:
            softmax_pv(hh, t0, sa_scr)
        for hh in heads:
            scores(hh, t0 + 2, sa_scr)
        for hh in heads:
            softmax_pv(hh, t0 + 1, sb_scr)

    def body(jj, carry):
        pair(2 * jj)
        pair(2 * jj + 1)
        return carry

    lax.fori_loop(0, qi // 2, body, 0)

    @pl.when(qi % 2 == 1)
    def _():
        pair(qi - 1)

    q_last = tq - tk
    for hh in heads:
        scores(hh, n_tiles - 1, sb_scr, q_last)
    for hh in heads:
        softmax_pv(hh, n_tiles - 2, sa_scr)
    for hh in heads:
        softmax_pv(hh, n_tiles - 1, sb_scr, q_last)

    lam = lam_ref[0, 0]
    for hh in heads:
        hs = slice(hh * HEAD_W, (hh + 1) * HEAD_W)
        inv = 1.0 / acc_scr[hh, HEAD_W:HEAD_W + 1, :]
        acc = acc_scr[hh, 0:HEAD_W, :]
        o_t = acc[:, 0:tq] * inv[:, 0:tq] - lam * (acc[:, tq:2 * tq] * inv[:, tq:2 * tq])
        o_n = o_t * lax.rsqrt(jnp.mean(o_t * o_t, axis=0, keepdims=True) + EPS)
        o_ref[0, :, hs] = (o_n.T * snw_ref[...] * (1.0 - LAMBDA_INIT) * g_ref[0, :, hs]).astype(o_ref.dtype)


def _attn_prompt(lam, q, kb, vt, bias, gate, snw, tq, tk):
    b, l, _ = q.shape
    assert tq == 2 * tk and l % tq == 0
    nq = l // tq
    nt = l // tk
    nd = bias.shape[1]
    hp = ATTN_HEADS_PER_STEP
    assert N_HEADS % hp == 0
    return pl.pallas_call(
        functools.partial(_attn_kernel, tq=tq, tk=tk, n_bias=nd),
        out_shape=jax.ShapeDtypeStruct((b, l, GROUP_W), BF16),
        grid=(b, N_HEADS // hp, nq),
        in_specs=[pl.BlockSpec(memory_space=pltpu.SMEM),
                  pl.BlockSpec((1, tq, hp * HEAD_W), lambda bb, h, i: (bb, i, h)),
                  pl.BlockSpec((1, l, hp * HEAD_W), lambda bb, h, i: (bb, 0, h)),
                  pl.BlockSpec((1, hp, nt, VT_ROWS, tk), lambda bb, h, i: (bb, h, 0, 0, 0)),
                  pl.BlockSpec((hp, nd, tk, tq), lambda bb, h, i: (h, 0, 0, 0)),
                  pl.BlockSpec((1, tq, hp * HEAD_W), lambda bb, h, i: (bb, i, h)),
                  pl.BlockSpec((1, HEAD_W), lambda bb, h, i: (0, 0))],
        out_specs=pl.BlockSpec((1, tq, hp * HEAD_W), lambda bb, h, i: (bb, i, h)),
        scratch_shapes=[pltpu.VMEM((hp, 2 * tq, HEAD_W), BF16),
                        pltpu.VMEM((hp, tk, 2 * tq), F32),
                        pltpu.VMEM((hp, tk, 2 * tq), F32),
                        pltpu.VMEM((hp, 1, 2 * tq), F32),
                        pltpu.VMEM((hp, VT_ROWS, 2 * tq), F32)],
        compiler_params=_cparams(("parallel", "parallel", "arbitrary")),
        name="attn_prompt",
    )(lam, q, kb, vt, bias, gate, snw)


def _decode_kernel(pt_ref, lam_ref, q_ref, kn_ref, vn_ref, g_ref, snw_ref, bias_ref, biasn_ref, ck_hbm, cv_hbm,
                   o_ref, kbuf, vbuf, sem, qst_scr, m_scr, l_scr, acc_scr, kpg_scr, vpg_scr,
                   *, grp, ring, rows, page, n_seq, groups_per_seq):
    hr = 2 * rows
    total_groups = n_seq * groups_per_seq

    def group_copies(gidx, slot0):
        b = _div_pow2(gidx, groups_per_seq)
        p0 = jnp.bitwise_and(gidx, groups_per_seq - 1) * grp
        copies = []
        for j in range(grp):
            pid = pt_ref[b, p0 + j]
            copies.append(pltpu.make_async_copy(ck_hbm.at[pid], kbuf.at[slot0 + j], sem.at[0, slot0 + j]))
            copies.append(pltpu.make_async_copy(cv_hbm.at[pid], vbuf.at[slot0 + j], sem.at[1, slot0 + j]))
        return copies

    def start_group(gidx, slot0):
        for cp in group_copies(gidx, slot0):
            cp.start()

    def wait_group(gidx, slot0):
        for cp in group_copies(gidx, slot0):
            cp.wait()

    def head_page(buf, slot, hh):
        return buf[slot, pl.ds(hh, page, stride=N_HEADS), :].astype(BF16)

    def init_sequence(b):
        q = q_ref[b].astype(F32)
        lane = lax.broadcasted_iota(jnp.int32, (rows, HEAD_W), 1)
        kpg_scr[...] = jnp.zeros_like(kpg_scr)
        vpg_scr[...] = jnp.zeros_like(vpg_scr)
        for hh in range(N_HEADS):
            qh = q[:, hh * HEAD_W:(hh + 1) * HEAD_W]
            qst_scr[hh] = jnp.concatenate([jnp.where(lane < HALF, qh, 0.0), jnp.where(lane >= HALF, qh, 0.0)],
                                          axis=0).astype(BF16)
            kpg_scr[hh, 0:rows, :] = kn_ref[b, pl.ds(hh, rows, stride=N_HEADS), :]
            vpg_scr[hh, 0:rows, :] = vn_ref[b, pl.ds(hh, rows, stride=N_HEADS), :]
        m_scr[...] = jnp.full(m_scr.shape, NEG, F32)
        l_scr[...] = jnp.zeros_like(l_scr)
        acc_scr[...] = jnp.zeros_like(acc_scr)

    def update(s, n_blocks, value_fn):
        m_old = m_scr[...]
        m_new = jnp.maximum(m_old, jnp.max(s, axis=-1, keepdims=True))
        alpha = jnp.exp(m_old - m_new)
        p = jnp.exp(s - m_new)
        l_scr[...] = alpha * l_scr[...] + jnp.sum(p, axis=-1, keepdims=True)
        pb = p.astype(BF16)
        pv_rows = []
        for hh in range(N_HEADS):
            ph = pb[hh * hr:(hh + 1) * hr, :]
            pv = _dot(ph[:, 0:page], value_fn(hh, 0))
            for j in range(1, n_blocks):
                pv = pv + _dot(ph[:, j * page:(j + 1) * page], value_fn(hh, j))
            pv_rows.append(pv)
        acc_scr[...] = acc_scr[...] * alpha + jnp.concatenate(pv_rows, axis=0)
        m_scr[...] = m_new

    def scores(n_blocks, key_fn):
        s_rows = []
        for hh in range(N_HEADS):
            qh = qst_scr[hh]
            blocks = [_dot_nt(qh, key_fn(hh, j)) for j in range(n_blocks)]
            s_rows.append(blocks[0] if n_blocks == 1 else jnp.concatenate(blocks, axis=1))
        return jnp.concatenate(s_rows, axis=0)

    def consume_group(gs, slot0):
        s = scores(grp, lambda hh, j: head_page(kbuf, slot0 + j, hh)) + bias_ref[gs]
        update(s, grp, lambda hh, j: head_page(vbuf, slot0 + j, hh))

    def finish_sequence(b):
        s_new = scores(1, lambda hh, j: kpg_scr[hh].astype(BF16)) + biasn_ref[...]
        update(s_new, 1, lambda hh, j: vpg_scr[hh].astype(BF16))
        lam = lam_ref[0, 0]
        inv = 1.0 / l_scr[...]
        acc = acc_scr[...] * inv
        gate = g_ref[b]
        for hh in range(N_HEADS):
            hs = slice(hh * HEAD_W, (hh + 1) * HEAD_W)
            r0 = hh * hr
            o = acc[r0:r0 + rows, :] - lam * acc[r0 + rows:r0 + 2 * rows, :]
            o_ref[b, :, hs] = _sub_norm_gate(o, snw_ref[...], gate[:, hs]).astype(o_ref.dtype)

    for part in range(ring):
        start_group(part, part * grp)

    def seq_body(b, carry):
        init_sequence(b)

        def ring_body(it, c):
            g0 = b * groups_per_seq + ring * it
            for part in range(ring):
                gidx = g0 + part
                slot0 = part * grp
                wait_group(gidx, slot0)
                consume_group(ring * it + part, slot0)

                @pl.when(gidx + ring < total_groups)
                def _():
                    start_group(gidx + ring, slot0)
            return c

        lax.fori_loop(0, groups_per_seq // ring, ring_body, 0)
        finish_sequence(b)
        return carry

    lax.fori_loop(0, n_seq, seq_body, 0)


def _attn_decode(page_table, lam, q, k_new, v_new, gate, snw, bias_dec, bias_new, cache_k, cache_v, n_pages):
    nb, rows, _ = q.shape
    pages_per_seq = page_table.shape[1]
    page = cache_k.shape[1] // N_HEADS
    grp = n_pages
    groups_per_seq = pages_per_seq // grp
    assert pages_per_seq % grp == 0 and page == LANES
    ring = min(DECODE_RING_GROUPS, groups_per_seq)
    assert groups_per_seq % ring == 0 and groups_per_seq & (groups_per_seq - 1) == 0
    assert bias_dec.shape == (groups_per_seq, 2 * N_HEADS * rows, grp * page)
    n_rows = 2 * N_HEADS * rows
    smem = pl.BlockSpec(memory_space=pltpu.SMEM)
    vmem = pl.BlockSpec(memory_space=pltpu.VMEM)
    hbm = pl.BlockSpec(memory_space=pl.ANY)
    return pl.pallas_call(
        functools.partial(_decode_kernel, grp=grp, ring=ring, rows=rows, page=page, n_seq=nb,
                          groups_per_seq=groups_per_seq),
        out_shape=jax.ShapeDtypeStruct((nb, rows, GROUP_W), F32),
        in_specs=[smem, smem, vmem, vmem, vmem, vmem, vmem, vmem, vmem, hbm, hbm],
        out_specs=vmem,
        scratch_shapes=[pltpu.VMEM((ring * grp, page * N_HEADS, HEAD_W), F32),
                        pltpu.VMEM((ring * grp, page * N_HEADS, HEAD_W), F32),
                        pltpu.SemaphoreType.DMA((2, ring * grp)),
                        pltpu.VMEM((N_HEADS, 2 * rows, HEAD_W), BF16),
                        pltpu.VMEM((n_rows, 1), F32),
                        pltpu.VMEM((n_rows, 1), F32),
                        pltpu.VMEM((n_rows, HEAD_W), F32),
                        pltpu.VMEM((N_HEADS, page, HEAD_W), F32),
                        pltpu.VMEM((N_HEADS, page, HEAD_W), F32)],
        compiler_params=pltpu.CompilerParams(vmem_limit_bytes=VMEM_LIMIT_BYTES),
        name="attn_decode",
    )(page_table, lam, q, k_new, v_new, gate, snw, bias_dec, bias_new, cache_k, cache_v)


def _mix_out_kernel(ya_ref, yb_ref, x_ref, mod_ref, w_ref, o_ref, *, nseq, rows):
    d = x_ref.shape[-1]
    tm = nseq * rows
    ya = ya_ref[...].reshape(tm, GROUP_W).astype(BF16)
    yb = yb_ref[...].reshape(tm, GROUP_W).astype(BF16)
    out = _dot(ya, w_ref[0:GROUP_W, :]) + _dot(yb, w_ref[GROUP_W:2 * GROUP_W, :])
    gate = mod_ref[:, :, 2 * d:3 * d]
    o_ref[...] = x_ref[...] + gate * out.reshape(nseq, rows, d)


def _mix_out(ya, yb, x, mod, w_out, *, nseq, rows):
    n_s, r_total, d = x.shape
    grid = (n_s // nseq, r_total // rows)
    tok = lambda s, t: (s, t, 0)
    return pl.pallas_call(
        functools.partial(_mix_out_kernel, nseq=nseq, rows=rows),
        out_shape=jax.ShapeDtypeStruct(x.shape, F32),
        grid=grid,
        in_specs=[pl.BlockSpec((nseq, rows, GROUP_W), tok),
                  pl.BlockSpec((nseq, rows, GROUP_W), tok),
                  pl.BlockSpec((nseq, rows, d), tok),
                  pl.BlockSpec((nseq, 1, 3 * d), lambda s, t: (s, 0, 0)),
                  pl.BlockSpec((2 * GROUP_W, d), lambda s, t: (0, 0))],
        out_specs=pl.BlockSpec((nseq, rows, d), tok),
        compiler_params=_cparams(("parallel", "arbitrary")),
        name="mix_out",
    )(ya, yb, x, mod, w_out)


def _largest_tile(n, cap):
    t = cap
    while n % t:
        t //= 2
    return t


def kernel(x_prompt, x_sample, c_prompt, c_sample, cache_k, cache_v, state_hgrn, page_table, norm_w, w_ada,
           b_ada, w_in, w_out, lb_param, hg_norm_w, q_norm_w, k_norm_w, lam_q1, lam_k1, lam_q2, lam_k2,
           sub_norm_w, rel_table):
    depth = w_in.shape[0]
    assert depth == 1
    b, l, d = x_prompt.shape
    db, dl, _ = x_sample.shape
    assert d == 2 * GROUP_W and w_in.shape[2] == 9 * GROUP_W
    page = cache_k.shape[2]
    past = page_table.shape[1] * page
    assert dl % SUBLANES == 0 and cache_k.shape[3] * cache_k.shape[4] == GROUP_W

    w_in_b = w_in[0].astype(BF16)
    w_out_b = w_out[0].astype(BF16)
    nw = norm_w.reshape(1, d)
    hgw = jnp.tile(hg_norm_w.reshape(1, HEAD_W), (1, N_HEADS))
    qnw = jnp.tile(q_norm_w.reshape(1, HALF), (1, GROUP_W // HALF))
    knw = jnp.tile(k_norm_w.reshape(1, HALF), (1, GROUP_W // HALF))
    snw = sub_norm_w.reshape(1, HEAD_W)

    mod = _adaln_mod(jnp.concatenate([c_prompt, c_sample], axis=0), w_ada[0], b_ada[0])
    mod_p = mod[:b].reshape(b, 1, 3 * d)
    mod_s = mod[b:].reshape(db, 1, 3 * d)
    lb, lam = _small_params(lb_param, lam_q1, lam_k1, lam_q2, lam_k2)

    log2e = math.log2(math.e)
    ya, q, kf, vf, gb, kb, vt, st_p = _mix_in(
        x_prompt, mod_p, nw, w_in_b, lb, hgw, qnw, knw, nseq=1, rows=_largest_tile(l, MIX_ROWS),
        chunk=CHUNK if l % CHUNK == 0 else l, sub=SUB, s0=None, attn_tile=ATTN_TK, q_dtype=BF16,
        q_scale=HALF ** -0.5 * log2e)
    bias_p = _bias_prompt(rel_table, ATTN_TK, ATTN_TQ, log2e)
    yb = _attn_prompt(lam, q, kb, vt, bias_p, gb, snw, ATTN_TQ, ATTN_TK)
    y_prompt = _mix_out(ya, yb, x_prompt, mod_p, w_out_b, nseq=1, rows=_largest_tile(l, MIX_OUT_ROWS))

    nseq = _largest_tile(db, 8)
    ya_s, q_s, kf_s, vf_s, gb_s, st_s = _mix_in(
        x_sample, mod_s, nw, w_in_b, lb, hgw, qnw, knw, nseq=nseq, rows=dl,
        chunk=dl, sub=dl, s0=state_hgrn[0], attn_tile=None, q_dtype=F32, q_scale=HALF ** -0.5)
    grp = _largest_tile(page_table.shape[1] // 2, DECODE_GROUP_PAGES)
    bias_d, bias_n = _bias_decode(rel_table, dl, past, grp * page)
    n_rows = 2 * N_HEADS * dl
    yb_s = _attn_decode(page_table, lam, q_s, kf_s, vf_s, gb_s, snw,
                        bias_d.reshape(-1, n_rows, grp * page), bias_n.reshape(n_rows, LANES),
                        cache_k.reshape(-1, page * N_HEADS, HEAD_W), cache_v.reshape(-1, page * N_HEADS, HEAD_W),
                        n_pages=grp)
    y_sample = _mix_out(ya_s, yb_s, x_sample, mod_s, w_out_b, nseq=nseq, rows=dl)

    hb = N_HEADS
    return (y_prompt, y_sample,
            kf.reshape(1, b, l, hb, HEAD_W), vf.reshape(1, b, l, hb, HEAD_W), st_p[None],
            kf_s.reshape(1, db, dl, hb, HEAD_W), vf_s.reshape(1, db, dl, hb, HEAD_W), st_s[None])
```

```python
import functools
import math

import jax
import jax.numpy as jnp
from jax import lax
from jax.experimental import pallas as pl
from jax.experimental.pallas import tpu as pltpu

F32 = jnp.float32
BF16 = jnp.bfloat16

N_HEADS = 4
HEAD_W = 128
HALF = 64
GROUP_W = N_HEADS * HEAD_W
N_BUCKETS = 32
MAX_EXACT = N_BUCKETS // 2
MAX_DIST = 1024
CHUNK = 64
SUB = 16
EPS = 1e-6
LAMBDA_INIT = 0.8 - 0.6 * math.exp(-0.3 * 0)
NEG = -1e30
ATTN_TQ = 512
ATTN_TK = 256
MIX_ROWS = 1024
MIX_BLOCK_ROWS = 256
MIX_OUT_ROWS = 1024
ATTN_HEADS_PER_STEP = 1
VT_ROWS = HEAD_W + 16
DECODE_GROUP_PAGES = 8
DECODE_RING_GROUPS = 4

LANES = 128
SUBLANES = 8
VMEM_LIMIT_BYTES = 56 * 1024 * 1024


def _bucket_lower_bounds():
    los = list(range(MAX_EXACT))
    span = N_BUCKETS - MAX_EXACT
    ratio = MAX_DIST // MAX_EXACT
    for m in range(MAX_EXACT, N_BUCKETS):
        d = MAX_EXACT
        while d ** span < (MAX_EXACT ** span) * (ratio ** (m - MAX_EXACT)):
            d += 1
        los.append(d)
    return los


BUCKET_LO = _bucket_lower_bounds()
SATURATED_DIST = BUCKET_LO[-1]


def _cparams(sem):
    return pltpu.CompilerParams(dimension_semantics=sem, vmem_limit_bytes=VMEM_LIMIT_BYTES)


def _split2(x):
    hi = x.astype(BF16)
    lo = (x - hi.astype(F32)).astype(BF16)
    return hi, lo


def _dot(a, b):
    return jnp.dot(a, b, preferred_element_type=F32)


def _dot_nt(a, b):
    return lax.dot_general(a, b, (((1,), (1,)), ((), ())), preferred_element_type=F32)


def _dot_tn(a, b):
    return lax.dot_general(a, b, (((0,), (0,)), ((), ())), preferred_element_type=F32)


def _div_pow2(x, n):
    assert n > 0 and n & (n - 1) == 0
    return lax.shift_right_logical(x, n.bit_length() - 1)


def _sigmoid_pair(x):
    t = jnp.exp(-jnp.abs(x))
    r = 1.0 / (1.0 + t)
    tr = t * r
    pos = x >= 0
    return jnp.where(pos, r, tr), jnp.where(pos, tr, r)


def _mod_kernel(c_ref, w_ref, b_ref, o_ref):
    c = c_ref[...]
    sig, _ = _sigmoid_pair(c)
    s = c * sig
    w = w_ref[...]
    s1, s2 = _split2(s)
    w1, w2 = _split2(w)
    o_ref[...] = _dot(s1, w1) + (_dot(s1, w2) + _dot(s2, w1)) + b_ref[...]


def _adaln_mod(c_all, w_ada, b_ada):
    n, d = c_all.shape
    d3 = w_ada.shape[1]
    bn = d
    return pl.pallas_call(
        _mod_kernel,
        out_shape=jax.ShapeDtypeStruct((n, d3), F32),
        grid=(d3 // bn,),
        in_specs=[pl.BlockSpec((n, d), lambda j: (0, 0)),
                  pl.BlockSpec((d, bn), lambda j: (0, j)),
                  pl.BlockSpec((1, bn), lambda j: (0, j))],
        out_specs=pl.BlockSpec((n, bn), lambda j: (0, j)),
        compiler_params=_cparams(("arbitrary",)),
        name="adaln_mod",
    )(c_all, w_ada, b_ada.reshape(1, d3))


def _small_kernel(lbp_ref, q1_ref, k1_ref, q2_ref, k2_ref, lb_ref, lam_ref):
    p = lbp_ref[...]
    m = jnp.max(p, axis=0, keepdims=True)
    e = jnp.exp(p - m)
    lb_ref[...] = e[0:1, :] / jnp.sum(e, axis=0, keepdims=True)
    s1 = jnp.sum(q1_ref[...] * k1_ref[...], axis=-1, keepdims=True)
    s2 = jnp.sum(q2_ref[...] * k2_ref[...], axis=-1, keepdims=True)
    lam_ref[...] = jnp.exp(s1) - jnp.exp(s2) + LAMBDA_INIT


def _small_params(lb_param, lq1, lk1, lq2, lk2):
    return pl.pallas_call(
        _small_kernel,
        out_shape=(jax.ShapeDtypeStruct((1, lb_param.shape[1]), F32),
                   jax.ShapeDtypeStruct((1, 1), F32)),
        name="small_params",
    )(lb_param, lq1, lk1, lq2, lk2)


def _bucket_of(d):
    return max(m for m in range(N_BUCKETS) if d >= BUCKET_LO[m])


def _bias_select(d, tab_ref, h, d_min, d_max):
    m_lo, m_hi = _bucket_of(d_min), _bucket_of(d_max)
    val = jnp.full(d.shape, tab_ref[m_hi, h], F32)
    for m in range(m_hi - 1, m_lo - 1, -1):
        val = jnp.where(d < BUCKET_LO[m + 1], tab_ref[m, h], val)
    return val


def _bias_prompt_kernel(tab_ref, o_ref, *, tk, tq, scale):
    h = pl.program_id(0)
    kk = lax.broadcasted_iota(jnp.int32, (tk, tq), 0)
    qq = lax.broadcasted_iota(jnp.int32, (tk, tq), 1)
    for idx in range(o_ref.shape[1]):
        dl = idx - (tq // tk - 1)
        d = dl * tk + qq - kk
        d_min, d_max = dl * tk - (tk - 1), dl * tk + tq - 1
        val = _bias_select(jnp.maximum(d, 0), tab_ref, h, max(d_min, 0), max(d_max, 0)) * scale
        o_ref[0, idx] = jnp.where(d >= 0, val, NEG) if d_min < 0 else val


def _n_bias_tiles(tk, tq):
    n_const = 0
    while n_const * tk - (tk - 1) < SATURATED_DIST:
        n_const += 1
    return n_const + 1 + (tq // tk - 1)


def _bias_prompt(rel_table, tk, tq, scale):
    nd = _n_bias_tiles(tk, tq)
    return pl.pallas_call(
        functools.partial(_bias_prompt_kernel, tk=tk, tq=tq, scale=scale),
        out_shape=jax.ShapeDtypeStruct((N_HEADS, nd, tk, tq), F32),
        grid=(N_HEADS,),
        in_specs=[pl.BlockSpec(memory_space=pltpu.SMEM)],
        out_specs=pl.BlockSpec((1, nd, tk, tq), lambda h: (h, 0, 0, 0)),
        compiler_params=_cparams(("arbitrary",)),
        name="bias_prompt",
    )(rel_table)


def _bias_decode_kernel(tab_ref, o_ref, on_ref, *, rows, past, width):
    h = pl.program_id(0)
    i = lax.broadcasted_iota(jnp.int32, (rows, width), 0)
    lane = lax.broadcasted_iota(jnp.int32, (rows, width), 1)
    for g in range(past // width):
        d_min, d_max = past - (g + 1) * width + 1, past - g * width + rows - 1
        val = _bias_select(past + i - (g * width + lane), tab_ref, h, d_min, d_max)
        o_ref[g, 0, 0] = val
        o_ref[g, 0, 1] = val
    i2 = lax.broadcasted_iota(jnp.int32, (rows, LANES), 0)
    j2 = lax.broadcasted_iota(jnp.int32, (rows, LANES), 1)
    d2 = i2 - j2
    vn = jnp.where(d2 >= 0, _bias_select(jnp.maximum(d2, 0), tab_ref, h, 0, rows - 1), NEG)
    on_ref[0, 0] = vn
    on_ref[0, 1] = vn


def _bias_decode(rel_table, rows, past, width):
    ng = past // width
    return pl.pallas_call(
        functools.partial(_bias_decode_kernel, rows=rows, past=past, width=width),
        out_shape=(jax.ShapeDtypeStruct((ng, N_HEADS, 2, rows, width), F32),
                   jax.ShapeDtypeStruct((N_HEADS, 2, rows, LANES), F32)),
        grid=(N_HEADS,),
        in_specs=[pl.BlockSpec(memory_space=pltpu.SMEM)],
        out_specs=(pl.BlockSpec((ng, 1, 2, rows, width), lambda h: (0, h, 0, 0, 0)),
                   pl.BlockSpec((1, 2, rows, LANES), lambda h: (h, 0, 0, 0))),
        compiler_params=_cparams(("arbitrary",)),
        name="bias_decode",
    )(rel_table)


def _group_mean_sq(x, gmat):
    sq = x * x
    outs = []
    w = gmat.shape[0]
    for c0 in range(0, x.shape[1], w):
        outs.append(_dot(sq[:, c0:c0 + w].astype(BF16), gmat))
    return jnp.concatenate(outs, axis=1)


def _hgrn_levels(sub):
    levels, h = [], sub // 2
    while h >= 1:
        levels.append(h)
        h //= 2
    return levels


def _hgrn_masks(chunk, sub, n):
    n_sub = chunk // sub
    masks = {"wide": None}
    if n_sub > 1:
        rr = lax.broadcasted_iota(jnp.int32, (chunk, n_sub * chunk), 0)
        cc = lax.broadcasted_iota(jnp.int32, (chunk, n_sub * chunk), 1)
        own_anchor = _div_pow2(cc, chunk) == _div_pow2(rr, sub)
        earlier = _div_pow2(jnp.bitwise_and(cc, chunk - 1), sub) < _div_pow2(rr, sub)
        masks["wide"] = jnp.where(own_anchor, jnp.where(earlier, 1.0, 0.0), 0.0)
    rr = lax.broadcasted_iota(jnp.int32, (n, n), 0)
    cc = lax.broadcasted_iota(jnp.int32, (n, n), 1)
    for h in _hgrn_levels(sub):
        same = _div_pow2(rr, 2 * h) == _div_pow2(cc, 2 * h)
        t_second = jnp.bitwise_and(rr, 2 * h - 1) >= h
        j_first = jnp.bitwise_and(cc, 2 * h - 1) < h
        masks[h] = jnp.where(same, jnp.where(t_second, jnp.where(j_first, 1.0, 0.0), 0.0), 0.0)
    masks["diag"] = jnp.where(rr == cc, 1.0, 0.0)
    return masks


def _level_boundary(bc, h, chunk):
    pieces = []
    if 2 * h >= SUBLANES:
        for blk in range(chunk // (2 * h)):
            r = blk * 2 * h + h - 1
            pieces.append(jnp.broadcast_to(bc[r:r + 1, :], (2 * h, HEAD_W)))
    else:
        per = SUBLANES // (2 * h)
        sub_row = lax.broadcasted_iota(jnp.int32, (SUBLANES, HEAD_W), 0)
        for g8 in range(chunk // SUBLANES):
            val = None
            for k in range(per):
                r = g8 * SUBLANES + k * 2 * h + h - 1
                row = jnp.broadcast_to(bc[r:r + 1, :], (SUBLANES, HEAD_W))
                val = row if val is None else jnp.where(sub_row >= k * 2 * h, row, val)
            pieces.append(val)
    return pieces[0] if len(pieces) == 1 else jnp.concatenate(pieces, axis=0)


def _hgrn_near(b, f, q, k, v, masks, sub):
    n = b.shape[0]
    kb16 = k.astype(BF16)
    att = _dot_nt(q.astype(BF16), kb16) * masks["diag"]
    odd = jnp.bitwise_and(lax.broadcasted_iota(jnp.int32, (n, HEAD_W), 0), 1) == 1
    for h in _hgrn_levels(sub):
        if h == 1:
            att_h = _dot_nt((q * jnp.where(odd, f, 1.0)).astype(BF16), kb16)
        else:
            w = jnp.exp(-jnp.abs(b - _level_boundary(b, h, n)))
            att_h = _dot_nt((q * w).astype(BF16), (k * w).astype(BF16))
        att = att + att_h * masks[h]
    return _dot(att.astype(BF16), v.astype(BF16))


def _hgrn_chunk(bc, qc, kc, vc, st, mask_wide, chunk, sub):
    n_sub = chunk // sub
    o = _dot_nt((qc * jnp.exp(bc)).astype(BF16), st.astype(BF16))
    if n_sub > 1:
        vb = vc.astype(BF16)
        q_parts, k_parts = [], []
        for i in range(n_sub):
            lo, hi = i * sub, (i + 1) * sub
            anchor = bc[max(lo - 1, 0):max(lo - 1, 0) + 1, :]
            q_parts.append(qc[lo:hi, :] * jnp.exp(jnp.minimum(bc[lo:hi, :] - anchor, 0.0)))
            k_parts.append((kc * jnp.exp(jnp.minimum(anchor - bc, 0.0))).astype(BF16))
        qa = jnp.concatenate(q_parts, axis=0).astype(BF16)
        att_w = _dot_nt(qa, jnp.concatenate(k_parts, axis=0)) * mask_wide
        o = o + _dot(att_w.astype(BF16), jnp.concatenate([vb] * n_sub, axis=0))
    b_last = bc[chunk - 1:chunk, :]
    ke = kc * jnp.exp(b_last - bc)
    st_new = st * jnp.exp(b_last) + _dot_tn(vc.astype(BF16), ke.astype(BF16))
    return o, st_new


def _mix_in_kernel(*refs, nseq, rows, chunk, sub, per_seq_state, emit_attn_kv, q_scale):
    it = iter(refs)
    x_ref = next(it); mod_ref = next(it); nw_ref = next(it); win_ref = next(it); lb_ref = next(it)
    hgw_ref = next(it); qnw_ref = next(it); knw_ref = next(it)
    s0_ref = next(it) if per_seq_state else None
    ya_ref = next(it); q_ref = next(it); kf_ref = next(it); vf_ref = next(it); gb_ref = next(it)
    kb_ref = next(it) if emit_attn_kv else None
    vt_ref = next(it) if emit_attn_kv else None
    sfin_ref = next(it)
    st_scr = next(it); b_scr = next(it); q_scr = next(it); k_scr = next(it); v_scr = next(it); o_scr = next(it)
    f_scr = next(it)

    tm = nseq * rows
    d = x_ref.shape[-1]
    n_chunks = tm // chunk
    blk = min(tm, MIX_BLOCK_ROWS)
    assert tm % blk == 0 and blk % chunk == 0

    x3 = x_ref[...]
    ms = jnp.mean(x3 * x3, axis=-1, keepdims=True)
    shift = mod_ref[:, :, 0:d]
    scale = mod_ref[:, :, d:2 * d]
    h3 = x3 * lax.rsqrt(ms + EPS) * nw_ref[...] * (1.0 + scale) + shift
    h = h3.reshape(tm, d).astype(BF16)

    def proj(idx):
        return _dot(h, win_ref[:, idx * GROUP_W:(idx + 1) * GROUP_W])

    def group_b():
        gw = 2 * LANES
        gi = _div_pow2(lax.broadcasted_iota(jnp.int32, (gw, gw), 0), HALF)
        gj = _div_pow2(lax.broadcasted_iota(jnp.int32, (gw, gw), 1), HALF)
        gmat = jnp.where(gi == gj, 1.0 / HALF, 0.0).astype(BF16)

        qb = proj(5)
        qn = qb * lax.rsqrt(_group_mean_sq(qb, gmat) + EPS) * qnw_ref[...] * q_scale
        q_ref[...] = qn.reshape(nseq, rows, GROUP_W).astype(q_ref.dtype)
        kb = proj(6)
        kn = kb * lax.rsqrt(_group_mean_sq(kb, gmat) + EPS) * knw_ref[...]
        vb = proj(7)
        for hh in range(N_HEADS):
            hs = slice(hh * HEAD_W, (hh + 1) * HEAD_W)
            kf_ref[:, pl.ds(hh, rows, stride=N_HEADS), :] = kn[:, hs].reshape(nseq, rows, HEAD_W)
            vf_ref[:, pl.ds(hh, rows, stride=N_HEADS), :] = vb[:, hs].reshape(nseq, rows, HEAD_W)
        if emit_attn_kv:
            kb_ref[...] = kn.reshape(nseq, rows, GROUP_W).astype(BF16)
            vt_tile = vt_ref.shape[-1]
            for hh in range(N_HEADS):
                for j in range(tm // vt_tile):
                    vt_ref[0, hh, j, 0:HEAD_W, :] = (
                        vb[j * vt_tile:(j + 1) * vt_tile, hh * HEAD_W:(hh + 1) * HEAD_W].T.astype(BF16))
                    vt_ref[0, hh, j, HEAD_W:VT_ROWS, :] = jnp.ones((VT_ROWS - HEAD_W, vt_tile), BF16)
        zb = proj(8)
        sgz, _ = _sigmoid_pair(zb)
        gb_ref[...] = (zb * sgz).reshape(nseq, rows, GROUP_W)

    lb = lb_ref[...]
    fa = proj(1)
    sig, nsig = _sigmoid_pair(fa)
    f_gate = lb + (1.0 - lb) * sig
    f_scr[...] = f_gate
    logf = jnp.log(f_gate)
    k_scr[...] = (1.0 - lb) * nsig
    q_scr[...] = proj(0) * (HEAD_W ** -0.5)
    v_scr[...] = proj(2)
    ri = lax.broadcasted_iota(jnp.int32, (blk, blk), 0)
    ci = lax.broadcasted_iota(jnp.int32, (blk, blk), 1)
    same_chunk = _div_pow2(ri, chunk) == _div_pow2(ci, chunk)
    tri = jnp.where(ci <= ri, jnp.where(same_chunk, 1.0, 0.0), 0.0).astype(BF16)
    l1, l2 = _split2(logf)
    for r0 in range(0, tm, blk):
        b_scr[r0:r0 + blk, :] = _dot(tri, l1[r0:r0 + blk, :]) + _dot(tri, l2[r0:r0 + blk, :])

    if not per_seq_state:
        @pl.when(pl.program_id(1) == 0)
        def _():
            st_scr[...] = jnp.zeros_like(st_scr)

    near_rows = blk
    masks = _hgrn_masks(chunk, sub, near_rows)
    for hh in range(N_HEADS):
        hs = slice(hh * HEAD_W, (hh + 1) * HEAD_W)
        for r0 in range(0, tm, near_rows):
            nr = slice(r0, r0 + near_rows)
            o_scr[nr, hs] = _hgrn_near(b_scr[nr, hs], f_scr[nr, hs], q_scr[nr, hs], k_scr[nr, hs], v_scr[nr, hs],
                                       masks, sub)
        st = None if per_seq_state else st_scr[hh]
        for c in range(n_chunks):
            rs = slice(c * chunk, (c + 1) * chunk)
            if per_seq_state:
                st = s0_ref[c, hh].T
            o, st = _hgrn_chunk(b_scr[rs, hs], q_scr[rs, hs], k_scr[rs, hs], v_scr[rs, hs], st, masks["wide"],
                                chunk, sub)
            o_scr[rs, hs] = o_scr[rs, hs] + o
            if per_seq_state:
                sfin_ref[c, hh] = st.T
        if not per_seq_state:
            st_scr[hh] = st

    group_b()

    if not per_seq_state:
        @pl.when(pl.program_id(1) == pl.num_programs(1) - 1)
        def _():
            for hh in range(N_HEADS):
                sfin_ref[0, hh] = st_scr[hh].T

    sg, _ = _sigmoid_pair(proj(3))
    za = proj(4)
    sz, _ = _sigmoid_pair(za)
    outs = []
    for hh in range(N_HEADS):
        hs = slice(hh * HEAD_W, (hh + 1) * HEAD_W)
        oh = o_scr[:, hs] * sg[:, hs]
        mo = jnp.mean(oh * oh, axis=-1, keepdims=True)
        outs.append(oh * lax.rsqrt(mo + EPS))
    ya = jnp.concatenate(outs, axis=1) * hgw_ref[...] * (za * sz)
    ya_ref[...] = ya.reshape(nseq, rows, GROUP_W).astype(ya_ref.dtype)


def _mix_in(x, mod, norm_w, w_in, lb, hgw, qnw, knw, *, nseq, rows, chunk, sub, s0, attn_tile, q_dtype, q_scale):
    n_s, r_total, d = x.shape
    per_seq_state = s0 is not None
    emit = attn_tile is not None
    tm = nseq * rows
    if per_seq_state:
        assert rows == r_total and rows == chunk and n_s % nseq == 0
        grid = (n_s // nseq,)
        tok = lambda i: (i, 0, 0)
        seq = lambda i: (i, 0, 0)
        const2 = lambda i: (0, 0)
        st_map = lambda i: (i, 0, 0, 0)
        sem = ("arbitrary",)
    else:
        assert nseq == 1 and r_total % rows == 0 and rows % chunk == 0
        grid = (n_s, r_total // rows)
        tok = lambda b, t: (b, t, 0)
        seq = lambda b, t: (b, 0, 0)
        const2 = lambda b, t: (0, 0)
        st_map = lambda b, t: (b, 0, 0, 0)
        sem = ("parallel", "arbitrary")
    if emit:
        assert rows % attn_tile == 0
    d_in = w_in.shape[1]
    tok_spec = pl.BlockSpec((nseq, rows, GROUP_W), tok)
    tokhead_spec = pl.BlockSpec((nseq, rows * N_HEADS, HEAD_W), tok)
    in_specs = [pl.BlockSpec((nseq, rows, d), tok),
                pl.BlockSpec((nseq, 1, 3 * d), seq),
                pl.BlockSpec((1, d), const2),
                pl.BlockSpec((d, d_in), const2),
                pl.BlockSpec((1, GROUP_W), const2),
                pl.BlockSpec((1, GROUP_W), const2),
                pl.BlockSpec((1, GROUP_W), const2),
                pl.BlockSpec((1, GROUP_W), const2)]
    args = [x, mod, norm_w, w_in, lb, hgw, qnw, knw]
    if per_seq_state:
        in_specs.append(pl.BlockSpec((nseq, N_HEADS, HEAD_W, HEAD_W), st_map))
        args.append(s0)
    out_shape = [jax.ShapeDtypeStruct((n_s, r_total, GROUP_W), q_dtype),
                 jax.ShapeDtypeStruct((n_s, r_total, GROUP_W), q_dtype),
                 jax.ShapeDtypeStruct((n_s, r_total * N_HEADS, HEAD_W), F32),
                 jax.ShapeDtypeStruct((n_s, r_total * N_HEADS, HEAD_W), F32),
                 jax.ShapeDtypeStruct((n_s, r_total, GROUP_W), F32)]
    out_specs = [tok_spec, tok_spec, tokhead_spec, tokhead_spec, tok_spec]
    if emit:
        nt = r_total // attn_tile
        out_shape += [jax.ShapeDtypeStruct((n_s, r_total, GROUP_W), BF16),
                      jax.ShapeDtypeStruct((n_s, N_HEADS, nt, VT_ROWS, attn_tile), BF16)]
        out_specs += [tok_spec,
                      pl.BlockSpec((1, N_HEADS, rows // attn_tile, VT_ROWS, attn_tile),
                                   lambda b, t: (b, 0, t, 0, 0))]
    out_shape.append(jax.ShapeDtypeStruct((n_s, N_HEADS, HEAD_W, HEAD_W), F32))
    out_specs.append(pl.BlockSpec((nseq, N_HEADS, HEAD_W, HEAD_W), st_map))
    scratch = [pltpu.VMEM((N_HEADS, HEAD_W, HEAD_W), F32)] + [pltpu.VMEM((tm, GROUP_W), F32)] * 6
    kern = functools.partial(_mix_in_kernel, nseq=nseq, rows=rows, chunk=chunk, sub=sub,
                             per_seq_state=per_seq_state, emit_attn_kv=emit, q_scale=q_scale)
    return pl.pallas_call(
        kern, out_shape=tuple(out_shape), grid=grid, in_specs=in_specs, out_specs=tuple(out_specs),
        scratch_shapes=scratch, compiler_params=_cparams(sem),
        name="mix_in_sample" if per_seq_state else "mix_in_prompt",
    )(*args)


def _sub_norm_gate(o, snw, gate):
    ms = jnp.mean(o * o, axis=-1, keepdims=True)
    return o * lax.rsqrt(ms + EPS) * snw * (1.0 - LAMBDA_INIT) * gate


def _attn_kernel(lam_ref, q_ref, k_ref, vt_ref, bias_ref, g_ref, snw_ref, o_ref,
                 qs_scr, sa_scr, sb_scr, m_scr, acc_scr, *, tq, tk, n_bias):
    qi = pl.program_id(2)
    ratio = tq // tk
    n_tiles = ratio * (qi + 1)
    n_heads = qs_scr.shape[0]
    heads = range(n_heads)
    lane = lax.broadcasted_iota(jnp.int32, (tq, HEAD_W), 1)
    for hh in heads:
        q = q_ref[0, :, hh * HEAD_W:(hh + 1) * HEAD_W].astype(F32)
        qs_scr[hh, 0:tq, :] = jnp.where(lane < HALF, q, 0.0).astype(BF16)
        qs_scr[hh, tq:2 * tq, :] = jnp.where(lane >= HALF, q, 0.0).astype(BF16)
    m_scr[...] = jnp.full(m_scr.shape, NEG, F32)
    acc_scr[...] = jnp.zeros_like(acc_scr)

    def scores(hh, t, s_ref, q0=0):
        r0 = pl.multiple_of(t * tk, tk)
        kt = k_ref[0, pl.ds(r0, tk), hh * HEAD_W:(hh + 1) * HEAD_W]
        bt = bias_ref[hh, jnp.clip(ratio * qi - t + (ratio - 1), 0, n_bias - 1)]
        for c in range(2):
            cols = slice(c * tq + q0, (c + 1) * tq)
            s_ref[hh, :, cols] = _dot_nt(kt, qs_scr[hh, cols, :]) + bt[:, q0:tq]

    def softmax_pv(hh, t, s_ref, q0=0):
        vt = vt_ref[0, hh, t]
        for c in range(2):
            cols = slice(c * tq + q0, (c + 1) * tq)
            m_old = m_scr[hh, :, cols]
            m_new = jnp.maximum(m_old, jnp.max(s_ref[hh, :, cols], axis=0, keepdims=True))
            alpha = jnp.exp2(m_old - m_new)
            p = jnp.exp2(s_ref[hh, :, cols] - m_new)
            acc_scr[hh, :, cols] = acc_scr[hh, :, cols] * alpha + _dot(vt, p.astype(BF16))
            m_scr[hh, :, cols] = m_new

    for hh in heads:
        scores(hh, 0, sa_scr)

    def pair(j):
        t0 = 2 * j
        for hh in heads:
            scores(hh, t0 + 1, sb_scr)
        for hh in heads:
            softmax_pv(hh, t0, sa_scr)
        for hh in heads:
            scores(hh, t0 + 2, sa_scr)
        for hh in heads:
            softmax_pv(hh, t0 + 1, sb_scr)

    def body(jj, carry):
        pair(2 * jj)
        pair(2 * jj + 1)
        return carry

    lax.fori_loop(0, qi // 2, body, 0)

    @pl.when(qi % 2 == 1)
    def _():
        pair(qi - 1)

    q_last = tq - tk
    for hh in heads:
        scores(hh, n_tiles - 1, sb_scr, q_last)
    for hh in heads:
        softmax_pv(hh, n_tiles - 2, sa_scr)
    for hh in heads:
        softmax_pv(hh, n_tiles - 1, sb_scr, q_last)

    lam = lam_ref[0, 0]
    for hh in heads:
        hs = slice(hh * HEAD_W, (hh + 1) * HEAD_W)
        inv = 1.0 / acc_scr[hh, HEAD_W:HEAD_W + 1, :]
        acc = acc_scr[hh, 0:HEAD_W, :]
        o_t = acc[:, 0:tq] * inv[:, 0:tq] - lam * (acc[:, tq:2 * tq] * inv[:, tq:2 * tq])
        o_n = o_t * lax.rsqrt(jnp.mean(o_t * o_t, axis=0, keepdims=True) + EPS)
        o_ref[0, :, hs] = (o_n.T * snw_ref[...] * (1.0 - LAMBDA_INIT) * g_ref[0, :, hs]).astype(o_ref.dtype)


def _attn_prompt(lam, q, kb, vt, bias, gate, snw, tq, tk):
    b, l, _ = q.shape
    assert tq == 2 * tk and l % tq == 0
    nq = l // tq
    nt = l // tk
    nd = bias.shape[1]
    hp = ATTN_HEADS_PER_STEP
    assert N_HEADS % hp == 0
    return pl.pallas_call(
        functools.partial(_attn_kernel, tq=tq, tk=tk, n_bias=nd),
        out_shape=jax.ShapeDtypeStruct((b, l, GROUP_W), BF16),
        grid=(b, N_HEADS // hp, nq),
        in_specs=[pl.BlockSpec(memory_space=pltpu.SMEM),
                  pl.BlockSpec((1, tq, hp * HEAD_W), lambda bb, h, i: (bb, i, h)),
                  pl.BlockSpec((1, l, hp * HEAD_W), lambda bb, h, i: (bb, 0, h)),
                  pl.BlockSpec((1, hp, nt, VT_ROWS, tk), lambda bb, h, i: (bb, h, 0, 0, 0)),
                  pl.BlockSpec((hp, nd, tk, tq), lambda bb, h, i: (h, 0, 0, 0)),
                  pl.BlockSpec((1, tq, hp * HEAD_W), lambda bb, h, i: (bb, i, h)),
                  pl.BlockSpec((1, HEAD_W), lambda bb, h, i: (0, 0))],
        out_specs=pl.BlockSpec((1, tq, hp * HEAD_W), lambda bb, h, i: (bb, i, h)),
        scratch_shapes=[pltpu.VMEM((hp, 2 * tq, HEAD_W), BF16),
                        pltpu.VMEM((hp, tk, 2 * tq), F32),
                        pltpu.VMEM((hp, tk, 2 * tq), F32),
                        pltpu.VMEM((hp, 1, 2 * tq), F32),
                        pltpu.VMEM((hp, VT_ROWS, 2 * tq), F32)],
        compiler_params=_cparams(("parallel", "parallel", "arbitrary")),
        name="attn_prompt",
    )(lam, q, kb, vt, bias, gate, snw)


def _decode_kernel(pt_ref, lam_ref, q_ref, kn_ref, vn_ref, g_ref, snw_ref, bias_ref, biasn_ref, ck_hbm, cv_hbm,
                   o_ref, kbuf, vbuf, sem, qst_scr, m_scr, l_scr, acc_scr, kpg_scr, vpg_scr,
                   *, grp, ring, rows, page, n_seq, groups_per_seq):
    hr = 2 * rows
    total_groups = n_seq * groups_per_seq

    def group_copies(gidx, slot0):
        b = _div_pow2(gidx, groups_per_seq)
        p0 = jnp.bitwise_and(gidx, groups_per_seq - 1) * grp
        copies = []
        for j in range(grp):
            pid = pt_ref[b, p0 + j]
            copies.append(pltpu.make_async_copy(ck_hbm.at[pid], kbuf.at[slot0 + j], sem.at[0, slot0 + j]))
            copies.append(pltpu.make_async_copy(cv_hbm.at[pid], vbuf.at[slot0 + j], sem.at[1, slot0 + j]))
        return copies

    def start_group(gidx, slot0):
        for n, cp in enumerate(group_copies(gidx, slot0)):
            cp.start(priority=n % 2)

    def wait_group(gidx, slot0):
        for cp in group_copies(gidx, slot0):
            cp.wait()

    def head_page(buf, slot, hh):
        return buf[slot, pl.ds(hh, page, stride=N_HEADS), :].astype(BF16)

    def init_sequence(b):
        q = q_ref[b].astype(F32)
        lane = lax.broadcasted_iota(jnp.int32, (rows, HEAD_W), 1)
        kpg_scr[...] = jnp.zeros_like(kpg_scr)
        vpg_scr[...] = jnp.zeros_like(vpg_scr)
        for hh in range(N_HEADS):
            qh = q[:, hh * HEAD_W:(hh + 1) * HEAD_W]
            qst_scr[hh] = jnp.concatenate([jnp.where(lane < HALF, qh, 0.0), jnp.where(lane >= HALF, qh, 0.0)],
                                          axis=0).astype(BF16)
            kpg_scr[hh, 0:rows, :] = kn_ref[b, pl.ds(hh, rows, stride=N_HEADS), :]
            vpg_scr[hh, 0:rows, :] = vn_ref[b, pl.ds(hh, rows, stride=N_HEADS), :]
        m_scr[...] = jnp.full(m_scr.shape, NEG, F32)
        l_scr[...] = jnp.zeros_like(l_scr)
        acc_scr[...] = jnp.zeros_like(acc_scr)

    def update(s, n_blocks, value_fn):
        m_old = m_scr[...]
        m_new = jnp.maximum(m_old, jnp.max(s, axis=-1, keepdims=True))
        alpha = jnp.exp(m_old - m_new)
        p = jnp.exp(s - m_new)
        l_scr[...] = alpha * l_scr[...] + jnp.sum(p, axis=-1, keepdims=True)
        pb = p.astype(BF16)
        pv_rows = []
        for hh in range(N_HEADS):
            ph = pb[hh * hr:(hh + 1) * hr, :]
            pv = _dot(ph[:, 0:page], value_fn(hh, 0))
            for j in range(1, n_blocks):
                pv = pv + _dot(ph[:, j * page:(j + 1) * page], value_fn(hh, j))
            pv_rows.append(pv)
        acc_scr[...] = acc_scr[...] * alpha + jnp.concatenate(pv_rows, axis=0)
        m_scr[...] = m_new

    def scores(n_blocks, key_fn):
        s_rows = []
        for hh in range(N_HEADS):
            qh = qst_scr[hh]
            blocks = [_dot_nt(qh, key_fn(hh, j)) for j in range(n_blocks)]
            s_rows.append(blocks[0] if n_blocks == 1 else jnp.concatenate(blocks, axis=1))
        return jnp.concatenate(s_rows, axis=0)

    def consume_group(gs, slot0):
        s = scores(grp, lambda hh, j: head_page(kbuf, slot0 + j, hh)) + bias_ref[gs]
        update(s, grp, lambda hh, j: head_page(vbuf, slot0 + j, hh))

    def finish_sequence(b):
        s_new = scores(1, lambda hh, j: kpg_scr[hh].astype(BF16)) + biasn_ref[...]
        update(s_new, 1, lambda hh, j: vpg_scr[hh].astype(BF16))
        lam = lam_ref[0, 0]
        inv = 1.0 / l_scr[...]
        acc = acc_scr[...] * inv
        gate = g_ref[b]
        for hh in range(N_HEADS):
            hs = slice(hh * HEAD_W, (hh + 1) * HEAD_W)
            r0 = hh * hr
            o = acc[r0:r0 + rows, :] - lam * acc[r0 + rows:r0 + 2 * rows, :]
            o_ref[b, :, hs] = _sub_norm_gate(o, snw_ref[...], gate[:, hs]).astype(o_ref.dtype)

    for part in range(ring):
        start_group(part, part * grp)

    def seq_body(b, carry):
        init_sequence(b)

        def ring_body(it, c):
            g0 = b * groups_per_seq + ring * it
            for part in range(ring):
                gidx = g0 + part
                slot0 = part * grp
                wait_group(gidx, slot0)
                consume_group(ring * it + part, slot0)

                @pl.when(gidx + ring < total_groups)
                def _():
                    start_group(gidx + ring, slot0)
            return c

        lax.fori_loop(0, groups_per_seq // ring, ring_body, 0)
        finish_sequence(b)
        return carry

    lax.fori_loop(0, n_seq, seq_body, 0)


def _attn_decode(page_table, lam, q, k_new, v_new, gate, snw, bias_dec, bias_new, cache_k, cache_v, n_pages):
    nb, rows, _ = q.shape
    pages_per_seq = page_table.shape[1]
    page = cache_k.shape[1] // N_HEADS
    grp = n_pages
    groups_per_seq = pages_per_seq // grp
    assert pages_per_seq % grp == 0 and page == LANES
    ring = min(DECODE_RING_GROUPS, groups_per_seq)
    assert groups_per_seq % ring == 0 and groups_per_seq & (groups_per_seq - 1) == 0
    assert bias_dec.shape == (groups_per_seq, 2 * N_HEADS * rows, grp * page)
    n_rows = 2 * N_HEADS * rows
    smem = pl.BlockSpec(memory_space=pltpu.SMEM)
    vmem = pl.BlockSpec(memory_space=pltpu.VMEM)
    hbm = pl.BlockSpec(memory_space=pl.ANY)
    return pl.pallas_call(
        functools.partial(_decode_kernel, grp=grp, ring=ring, rows=rows, page=page, n_seq=nb,
                          groups_per_seq=groups_per_seq),
        out_shape=jax.ShapeDtypeStruct((nb, rows, GROUP_W), F32),
        in_specs=[smem, smem, vmem, vmem, vmem, vmem, vmem, vmem, vmem, hbm, hbm],
        out_specs=vmem,
        scratch_shapes=[pltpu.VMEM((ring * grp, page * N_HEADS, HEAD_W), F32),
                        pltpu.VMEM((ring * grp, page * N_HEADS, HEAD_W), F32),
                        pltpu.SemaphoreType.DMA((2, ring * grp)),
                        pltpu.VMEM((N_HEADS, 2 * rows, HEAD_W), BF16),
                        pltpu.VMEM((n_rows, 1), F32),
                        pltpu.VMEM((n_rows, 1), F32),
                        pltpu.VMEM((n_rows, HEAD_W), F32),
                        pltpu.VMEM((N_HEADS, page, HEAD_W), F32),
                        pltpu.VMEM((N_HEADS, page, HEAD_W), F32)],
        compiler_params=pltpu.CompilerParams(vmem_limit_bytes=VMEM_LIMIT_BYTES),
        name="attn_decode",
    )(page_table, lam, q, k_new, v_new, gate, snw, bias_dec, bias_new, cache_k, cache_v)


def _mix_out_kernel(ya_ref, yb_ref, x_ref, mod_ref, w_ref, o_ref, *, nseq, rows):
    d = x_ref.shape[-1]
    tm = nseq * rows
    ya = ya_ref[...].reshape(tm, GROUP_W).astype(BF16)
    yb = yb_ref[...].reshape(tm, GROUP_W).astype(BF16)
    out = _dot(ya, w_ref[0:GROUP_W, :]) + _dot(yb, w_ref[GROUP_W:2 * GROUP_W, :])
    gate = mod_ref[:, :, 2 * d:3 * d]
    o_ref[...] = x_ref[...] + gate * out.reshape(nseq, rows, d)


def _mix_out(ya, yb, x, mod, w_out, *, nseq, rows):
    n_s, r_total, d = x.shape
    grid = (n_s // nseq, r_total // rows)
    tok = lambda s, t: (s, t, 0)
    return pl.pallas_call(
        functools.partial(_mix_out_kernel, nseq=nseq, rows=rows),
        out_shape=jax.ShapeDtypeStruct(x.shape, F32),
        grid=grid,
        in_specs=[pl.BlockSpec((nseq, rows, GROUP_W), tok),
                  pl.BlockSpec((nseq, rows, GROUP_W), tok),
                  pl.BlockSpec((nseq, rows, d), tok),
                  pl.BlockSpec((nseq, 1, 3 * d), lambda s, t: (s, 0, 0)),
                  pl.BlockSpec((2 * GROUP_W, d), lambda s, t: (0, 0))],
        out_specs=pl.BlockSpec((nseq, rows, d), tok),
        compiler_params=_cparams(("parallel", "arbitrary")),
        name="mix_out",
    )(ya, yb, x, mod, w_out)


def _largest_tile(n, cap):
    t = cap
    while n % t:
        t //= 2
    return t


def kernel(x_prompt, x_sample, c_prompt, c_sample, cache_k, cache_v, state_hgrn, page_table, norm_w, w_ada,
           b_ada, w_in, w_out, lb_param, hg_norm_w, q_norm_w, k_norm_w, lam_q1, lam_k1, lam_q2, lam_k2,
           sub_norm_w, rel_table):
    depth = w_in.shape[0]
    assert depth == 1
    b, l, d = x_prompt.shape
    db, dl, _ = x_sample.shape
    assert d == 2 * GROUP_W and w_in.shape[2] == 9 * GROUP_W
    page = cache_k.shape[2]
    past = page_table.shape[1] * page
    assert dl % SUBLANES == 0 and cache_k.shape[3] * cache_k.shape[4] == GROUP_W

    w_in_b = w_in[0].astype(BF16)
    w_out_b = w_out[0].astype(BF16)
    nw = norm_w.reshape(1, d)
    hgw = jnp.tile(hg_norm_w.reshape(1, HEAD_W), (1, N_HEADS))
    qnw = jnp.tile(q_norm_w.reshape(1, HALF), (1, GROUP_W // HALF))
    knw = jnp.tile(k_norm_w.reshape(1, HALF), (1, GROUP_W // HALF))
    snw = sub_norm_w.reshape(1, HEAD_W)

    mod = _adaln_mod(jnp.concatenate([c_prompt, c_sample], axis=0), w_ada[0], b_ada[0])
    mod_p = mod[:b].reshape(b, 1, 3 * d)
    mod_s = mod[b:].reshape(db, 1, 3 * d)
    lb, lam = _small_params(lb_param, lam_q1, lam_k1, lam_q2, lam_k2)

    log2e = math.log2(math.e)
    ya, q, kf, vf, gb, kb, vt, st_p = _mix_in(
        x_prompt, mod_p, nw, w_in_b, lb, hgw, qnw, knw, nseq=1, rows=_largest_tile(l, MIX_ROWS),
        chunk=CHUNK if l % CHUNK == 0 else l, sub=SUB, s0=None, attn_tile=ATTN_TK, q_dtype=BF16,
        q_scale=HALF ** -0.5 * log2e)
    bias_p = _bias_prompt(rel_table, ATTN_TK, ATTN_TQ, log2e)
    yb = _attn_prompt(lam, q, kb, vt, bias_p, gb, snw, ATTN_TQ, ATTN_TK)
    y_prompt = _mix_out(ya, yb, x_prompt, mod_p, w_out_b, nseq=1, rows=_largest_tile(l, MIX_OUT_ROWS))

    nseq = _largest_tile(db, 8)
    ya_s, q_s, kf_s, vf_s, gb_s, st_s = _mix_in(
        x_sample, mod_s, nw, w_in_b, lb, hgw, qnw, knw, nseq=nseq, rows=dl,
        chunk=dl, sub=dl, s0=state_hgrn[0], attn_tile=None, q_dtype=F32, q_scale=HALF ** -0.5)
    grp = _largest_tile(page_table.shape[1] // 2, DECODE_GROUP_PAGES)
    bias_d, bias_n = _bias_decode(rel_table, dl, past, grp * page)
    n_rows = 2 * N_HEADS * dl
    yb_s = _attn_decode(page_table, lam, q_s, kf_s, vf_s, gb_s, snw,
                        bias_d.reshape(-1, n_rows, grp * page), bias_n.reshape(n_rows, LANES),
                        cache_k.reshape(-1, page * N_HEADS, HEAD_W), cache_v.reshape(-1, page * N_HEADS, HEAD_W),
                        n_pages=grp)
    y_sample = _mix_out(ya_s, yb_s, x_sample, mod_s, w_out_b, nseq=nseq, rows=dl)

    hb = N_HEADS
    return (y_prompt, y_sample,
            kf.reshape(1, b, l, hb, HEAD_W), vf.reshape(1, b, l, hb, HEAD_W), st_p[None],
            kf_s.reshape(1, db, dl, hb, HEAD_W), vf_s.reshape(1, db, dl, hb, HEAD_W), st_s[None])
```
